```python
import math
import jax, jax.numpy as jnp
from jax import lax
import numpy as np


D_MODEL = 2048
BATCH = 4
SEQ = 4096
DEPTH = 2

HEAD_DIM = 64
N_HEADS = D_MODEL // HEAD_DIM
H_A = 12
H_B = 10
H_C = N_HEADS - H_A - H_B
W_A = H_A * HEAD_DIM
W_B = H_B * HEAD_DIM
W_C = H_C * HEAD_DIM
IN_COLS = 3 * (W_A + W_B + W_C) + H_C
DIL_PATTERNS = ((128, 1), (512, 4), (2048, 16))
MOBA_BLOCK = 256
MOBA_TOPK = 3
MOBA_QBLOCK = 32
FOX_QBLOCK = 128
N_BUCKETS = 32
REL_MAX_DIST = 2048
D_FF = 5632
CONV_WIDTH = 3
DEEPNORM_ALPHA = (2 * DEPTH) ** 0.25
DEEPNORM_BETA = (8 * DEPTH) ** -0.25
LN_EPS = 1e-5
NEG = -1e30
ATTN_SCALE = HEAD_DIM ** -0.5

kernel_name = 'hybrid_dilated_moba_fox_block'


def layer_norm(x, g, b):
    xf = x.astype(jnp.float32)
    mu = jnp.mean(xf, axis=-1, keepdims=True)
    var = jnp.mean(jnp.square(xf - mu), axis=-1, keepdims=True)
    return ((xf - mu) * lax.rsqrt(var + LN_EPS) * g + b).astype(x.dtype)


def rms_norm(x, g):
    xf = x.astype(jnp.float32)
    return (xf * lax.rsqrt(jnp.mean(xf * xf, axis=-1, keepdims=True) + LN_EPS) * g).astype(x.dtype)


def t5_bucket(dist):
    n = jnp.maximum(dist, 0)
    max_exact = N_BUCKETS // 2
    nf = jnp.maximum(n, 1).astype(jnp.float32)
    large = max_exact + (jnp.log(nf / max_exact) / math.log(REL_MAX_DIST / max_exact)
                         * (N_BUCKETS - max_exact)).astype(jnp.int32)
    large = jnp.minimum(large, N_BUCKETS - 1)
    return jnp.where(n < max_exact, n, large)


def _chunks(t, nc, qb):
    b, h = t.shape[0], t.shape[1]
    return jnp.moveaxis(t.reshape(b, h, nc, qb, *t.shape[3:]), 2, 0)


def dilated_window_attention(q, k, v, rel_tab, window, dilation):
    B, H, S, hd = q.shape
    W = window // dilation
    L = S // dilation
    nb = -(-L // W)
    Lp = nb * W

    def to_blocks(t):
        t = t.reshape(B, H, L, dilation, hd).transpose(0, 1, 3, 2, 4)
        t = jnp.pad(t, ((0, 0), (0, 0), (0, 0), (0, Lp - L), (0, 0)))
        return t.reshape(B, H, dilation, nb, W, hd)

    def with_prev(t):
        prev = jnp.pad(t, ((0, 0), (0, 0), (0, 0), (1, 0), (0, 0), (0, 0)))[:, :, :, :-1]
        return jnp.concatenate([prev, t], axis=4)

    qb = to_blocks(q)
    kk = with_prev(to_blocks(k))
    vv = with_prev(to_blocks(v))
    a = jnp.arange(W)[:, None]
    j = jnp.arange(2 * W)[None, :]
    dist = a + W - j
    band = (dist >= 0) & (dist <= W)
    first = (jnp.arange(nb)[:, None, None] == 0) & (j[None] < W)
    mask = band[None] & ~first
    bias = jnp.moveaxis(rel_tab[t5_bucket(dist * dilation)], -1, 0)
    s = (jnp.einsum('bhrnqd,bhrnkd->bhrnqk', qb, kk).astype(jnp.float32) * ATTN_SCALE
         + bias[:, None, None].astype(jnp.float32))
    s = jnp.where(mask, s, NEG)
    m = jnp.max(s, axis=-1, keepdims=True)
    p = jnp.exp(s - m)
    l = jnp.sum(p, axis=-1, keepdims=True)
    o = jnp.einsum('bhrnqk,bhrnkd->bhrnqd', p.astype(v.dtype), vv).astype(jnp.float32) / l
    lse = (m + jnp.log(l))[..., 0]

    def from_blocks(t):
        t = t.reshape(B, H, dilation, Lp, *t.shape[5:])[:, :, :, :L]
        t = jnp.swapaxes(t, 2, 3)
        return t.reshape(B, H, S, *t.shape[4:])

    return from_blocks(o), from_blocks(lse)


def dilated_mixture(q, k, v, rel_tab):
    res = [dilated_window_attention(q, k, v, rel_tab, w, d) for (w, d) in DIL_PATTERNS]
    outs = jnp.stack([r[0] for r in res])
    lses = jnp.stack([r[1] for r in res])
    wts = jax.nn.softmax(lses, axis=0)
    return jnp.einsum('gbhs,gbhsd->bhsd', wts, outs).astype(q.dtype)


def moba_attention(q, k, v, rel_tab):
    B, H, S, hd = q.shape
    blk = MOBA_BLOCK
    nblk = -(-S // blk)
    Sp = nblk * blk
    pad = ((0, 0), (0, 0), (0, Sp - S), (0, 0))
    kb = jnp.pad(k, pad).reshape(B, H, nblk, blk, hd)
    vb = jnp.pad(v, pad).reshape(B, H, nblk, blk, hd)
    kmean = jnp.mean(kb.astype(jnp.float32), axis=3)
    gate = jnp.einsum('bhsd,bhnd->bhsn', q.astype(jnp.float32), kmean)
    qblk = jnp.arange(S) // blk
    past = jnp.arange(nblk)[None, :] < qblk[:, None]
    gate = jnp.where(past, gate, NEG)
    n_sel = min(MOBA_TOPK, nblk)
    _, idx = lax.top_k(gate, n_sel)
    valid = idx < qblk[:, None]
    QB = MOBA_QBLOCK
    nc = S // QB
    bi = jnp.arange(B)[:, None, None, None]
    hi = jnp.arange(H)[None, :, None, None]
    hi5 = jnp.arange(H)[None, :, None, None, None]
    tab_t = rel_tab.T

    def step(xs):
        ci, q_c, idx_c, valid_c = xs
        qpos = ci * QB + jnp.arange(QB)
        own = (ci * QB) // blk
        k_own = lax.dynamic_index_in_dim(kb, own, axis=2, keepdims=False)
        v_own = lax.dynamic_index_in_dim(vb, own, axis=2, keepdims=False)
        d_own = qpos[:, None] - (own * blk + jnp.arange(blk))[None, :]
        s_own = (jnp.einsum('bhqd,bhkd->bhqk', q_c, k_own).astype(jnp.float32) * ATTN_SCALE
                 + tab_t[:, t5_bucket(d_own)].astype(jnp.float32))
        s_own = jnp.where(d_own >= 0, s_own, NEG)
        k_sel = kb[bi, hi, idx_c]
        v_sel = vb[bi, hi, idx_c]
        d_sel = qpos[:, None, None] - (idx_c[..., None] * blk + jnp.arange(blk))
        s_sel = (jnp.einsum('bhqd,bhqnkd->bhqnk', q_c, k_sel).astype(jnp.float32) * ATTN_SCALE
                 + tab_t[hi5, t5_bucket(d_sel)].astype(jnp.float32))
        s_sel = jnp.where(valid_c[..., None], s_sel, NEG).reshape(B, H, QB, n_sel * blk)
        p = jax.nn.softmax(jnp.concatenate([s_sel, s_own], axis=-1), axis=-1).astype(v.dtype)
        p_sel = p[..., :n_sel * blk].reshape(B, H, QB, n_sel, blk)
        p_own = p[..., n_sel * blk:]
        return (jnp.einsum('bhqnk,bhqnkd->bhqd', p_sel, v_sel)
                + jnp.einsum('bhqk,bhkd->bhqd', p_own, v_own))

    out = lax.map(step, (jnp.arange(nc), _chunks(q, nc, QB), _chunks(idx, nc, QB),
                         _chunks(valid, nc, QB)))
    return jnp.moveaxis(out, 0, 2).reshape(B, H, S, hd)


def forgetting_attention(q, k, v, f_logit):
    B, H, S, hd = q.shape
    cum = lax.cumsum(jax.nn.log_sigmoid(f_logit.astype(jnp.float32)), axis=2)
    QB = FOX_QBLOCK
    nc = S // QB
    kpos = jnp.arange(S)

    def step(xs):
        ci, q_c, cum_c = xs
        qpos = ci * QB + jnp.arange(QB)
        s = (jnp.einsum('bhqd,bhkd->bhqk', q_c, k).astype(jnp.float32) * ATTN_SCALE
             + (cum_c[..., None] - cum[:, :, None, :]))
        s = jnp.where(kpos[None, :] <= qpos[:, None], s, NEG)
        p = jax.nn.softmax(s, axis=-1).astype(v.dtype)
        return jnp.einsum('bhqk,bhkd->bhqd', p, v)

    out = lax.map(step, (jnp.arange(nc), _chunks(q, nc, QB), _chunks(cum, nc, QB)))
    return jnp.moveaxis(out, 0, 2).reshape(B, H, S, hd)


def hybrid_mixer(h, w_in, b_f, g_mix_a, g_mix_b, g_mix_c, w_out, rel_bias):
    B, S, _ = h.shape
    proj = h @ w_in
    sizes = [W_A] * 3 + [W_B] * 3 + [W_C] * 3 + [H_C]
    offs = np.cumsum(sizes)[:-1].tolist()
    qa, ka, va, qb, kb, vb, qc, kc, vc, f = jnp.split(proj, offs, axis=-1)

    def heads(t, n):
        return t.reshape(B, S, n, HEAD_DIM).transpose(0, 2, 1, 3)

    def merge(o):
        return o.transpose(0, 2, 1, 3).reshape(B, S, -1)

    o_a = dilated_mixture(heads(qa, H_A), heads(ka, H_A), heads(va, H_A), rel_bias[:, :H_A])
    o_b = moba_attention(heads(qb, H_B), heads(kb, H_B), heads(vb, H_B), rel_bias[:, H_A:])
    f_logit = (f + b_f).transpose(0, 2, 1)
    o_c = forgetting_attention(heads(qc, H_C), heads(kc, H_C), heads(vc, H_C), f_logit)
    y = jnp.concatenate([rms_norm(merge(o_a), g_mix_a), rms_norm(merge(o_b), g_mix_b),
                         rms_norm(merge(o_c), g_mix_c)], axis=-1)
    return y @ w_out


def conv_ffn(h, w_up, conv_w, conv_b, w_down):
    u = h @ w_up
    u = lax.conv_general_dilated(u, conv_w[:, None, :], window_strides=(1,),
                                 padding=((CONV_WIDTH - 1, 0),),
                                 dimension_numbers=('NWC', 'WIO', 'NWC'),
                                 feature_group_count=u.shape[-1]) + conv_b
    a, b = jnp.split(u, 2, axis=-1)
    return (jax.nn.silu(a) * b) @ w_down


def setup_inputs(seed: int = 0) -> dict:
    key = jax.random.key(seed)
    ks = jax.random.split(key, 20)
    f32 = jnp.float32

    def nrm(k, shape, s):
        return jax.random.normal(k, shape, f32) * s

    return {
        'x': nrm(ks[0], (BATCH, SEQ, D_MODEL), 1.0),
        'c': nrm(ks[1], (BATCH, D_MODEL), 1.0),
        'rel_bias': nrm(ks[2], (N_BUCKETS, H_A + H_B), 0.2),
        'w_ada': nrm(ks[3], (DEPTH, D_MODEL, 6 * D_MODEL), 0.02),
        'b_ada': nrm(ks[4], (DEPTH, 6 * D_MODEL), 0.01),
        'w_in': nrm(ks[5], (DEPTH, D_MODEL, IN_COLS), D_MODEL ** -0.5),
        'b_f': 3.0 + nrm(ks[6], (DEPTH, H_C), 0.1),
        'g_mix_a': 1.0 + nrm(ks[7], (DEPTH, W_A), 0.02),
        'g_mix_b': 1.0 + nrm(ks[8], (DEPTH, W_B), 0.02),
        'g_mix_c': 1.0 + nrm(ks[9], (DEPTH, W_C), 0.02),
        'w_out': nrm(ks[10], (DEPTH, D_MODEL, D_MODEL), DEEPNORM_BETA * D_MODEL ** -0.5),
        'ln1_g': 1.0 + nrm(ks[11], (DEPTH, D_MODEL), 0.02),
        'ln1_b': nrm(ks[12], (DEPTH, D_MODEL), 0.02),
        'w_up': nrm(ks[13], (DEPTH, D_MODEL, 2 * D_FF), D_MODEL ** -0.5),
        'conv_w': nrm(ks[14], (DEPTH, CONV_WIDTH, 2 * D_FF), CONV_WIDTH ** -0.5),
        'conv_b': nrm(ks[15], (DEPTH, 2 * D_FF), 0.01),
        'w_down': nrm(ks[16], (DEPTH, D_FF, D_MODEL), DEEPNORM_BETA * D_FF ** -0.5),
        'ln2_g': 1.0 + nrm(ks[17], (DEPTH, D_MODEL), 0.02),
        'ln2_b': nrm(ks[18], (DEPTH, D_MODEL), 0.02),
    }


def reference(x, c, rel_bias, w_ada, b_ada, w_in, b_f, g_mix_a, g_mix_b, g_mix_c, w_out,
              ln1_g, ln1_b, w_up, conv_w, conv_b, w_down, ln2_g, ln2_b):
    cond = jax.nn.silu(c)
    for l in range(DEPTH):
        mod = cond @ w_ada[l] + b_ada[l]
        sh1, sc1, g1, sh2, sc2, g2 = [m[:, None, :] for m in jnp.split(mod, 6, axis=-1)]
        h = x * (1 + sc1) + sh1
        y = hybrid_mixer(h, w_in[l], b_f[l], g_mix_a[l], g_mix_b[l], g_mix_c[l], w_out[l], rel_bias)
        x = layer_norm(DEEPNORM_ALPHA * x + (1 + g1) * y, ln1_g[l], ln1_b[l])
        h = x * (1 + sc2) + sh2
        y = conv_ffn(h, w_up[l], conv_w[l], conv_b[l], w_down[l])
        x = layer_norm(DEEPNORM_ALPHA * x + (1 + g2) * y, ln2_g[l], ln2_b[l])
    return x
```

```python
import functools
import math

import numpy as np
import jax
import jax.numpy as jnp
from jax import lax
from jax.experimental import pallas as pl
from jax.experimental.pallas import tpu as pltpu

F32 = jnp.float32
BF16 = jnp.bfloat16

HEAD_DIM = 64
LANES = 128
H_A, H_B, H_C = 12, 10, 10
W_A, W_B, W_C = H_A * HEAD_DIM, H_B * HEAD_DIM, H_C * HEAD_DIM
DIL_PATTERNS = ((128, 1), (512, 4), (2048, 16))
DIL_W = 128
MOBA_BLOCK = 256
MOBA_TOPK = 3
N_BUCKETS = 32
REL_MAX_DIST = 2048
CONV_WIDTH = 3
LN_EPS = 1e-5
NEG = -1e30
ATTN_SCALE = HEAD_DIM ** -0.5
VMEM_LIMIT = 56 * 1024 * 1024

NT_DIMS = (((1,), (1,)), ((), ()))


def _cparams(*sem):
    return pltpu.CompilerParams(dimension_semantics=sem, vmem_limit_bytes=VMEM_LIMIT)


def _t5_bucket(dist):
    n = jnp.maximum(dist, 0)
    max_exact = N_BUCKETS // 2
    nf = jnp.maximum(n, 1).astype(F32)
    large = max_exact + (jnp.log(nf / max_exact) / math.log(REL_MAX_DIST / max_exact)
                         * (N_BUCKETS - max_exact)).astype(jnp.int32)
    large = jnp.minimum(large, N_BUCKETS - 1)
    return jnp.where(n < max_exact, n, large)


def _ada_kernel(c_ref, w_ref, b_ref, o_ref):
    c = c_ref[...]
    cond = c * (1.0 / (1.0 + jnp.exp(-c)))
    o_ref[...] = jnp.dot(cond, w_ref[...], precision=lax.Precision.HIGHEST,
                         preferred_element_type=F32) + b_ref[...]


def _ada_mod(c, w_ada, b_ada):
    depth, d, n = w_ada.shape
    bsz = c.shape[0]
    tn = 1024
    return pl.pallas_call(
        _ada_kernel,
        grid=(depth, n // tn),
        in_specs=[pl.BlockSpec((bsz, d), lambda l, j: (0, 0)),
                  pl.BlockSpec((None, d, tn), lambda l, j: (l, 0, j)),
                  pl.BlockSpec((None, 1, tn), lambda l, j: (l, 0, j))],
        out_specs=pl.BlockSpec((None, bsz, tn), lambda l, j: (l, 0, j)),
        out_shape=jax.ShapeDtypeStruct((depth, bsz, n), F32),
        compiler_params=_cparams("parallel", "parallel"),
    )(c, w_ada, b_ada.reshape(depth, 1, n))


def _modulate_kernel(x_ref, sc_ref, sh_ref, o_ref):
    o_ref[...] = (x_ref[...] * (1.0 + sc_ref[...]) + sh_ref[...]).astype(o_ref.dtype)


def _modulate(x, sc, sh):
    bsz, s, d = x.shape
    tm = 512
    vec = pl.BlockSpec((None, 1, d), lambda b, i: (b, 0, 0))
    return pl.pallas_call(
        _modulate_kernel,
        grid=(bsz, s // tm),
        in_specs=[pl.BlockSpec((None, tm, d), lambda b, i: (b, i, 0)), vec, vec],
        out_specs=pl.BlockSpec((None, tm, d), lambda b, i: (b, i, 0)),
        out_shape=jax.ShapeDtypeStruct((bsz, s, d), BF16),
        compiler_params=_cparams("parallel", "parallel"),
    )(x, sc, sh)


def _inproj_kernel(a_ref, w_ref, wf_ref, o_ref, f_ref):
    a = a_ref[...]
    o_ref[...] = jnp.dot(a, w_ref[...], preferred_element_type=F32)

    @pl.when(pl.program_id(1) == 0)
    def _():
        f_ref[...] = jnp.dot(a, wf_ref[...], preferred_element_type=F32)


def _inproj(h, w_main, w_f):
    t, d = h.shape
    n = w_main.shape[1]
    tm, tn = 512, 1024
    return pl.pallas_call(
        _inproj_kernel,
        grid=(t // tm, n // tn),
        in_specs=[pl.BlockSpec((tm, d), lambda i, j: (i, 0)),
                  pl.BlockSpec((d, tn), lambda i, j: (0, j)),
                  pl.BlockSpec((d, LANES), lambda i, j: (0, 0))],
        out_specs=[pl.BlockSpec((tm, tn), lambda i, j: (i, j)),
                   pl.BlockSpec((tm, LANES), lambda i, j: (i, 0))],
        out_shape=[jax.ShapeDtypeStruct((t, n), F32),
                   jax.ShapeDtypeStruct((t, LANES), F32)],
        compiler_params=_cparams("parallel", "arbitrary"),
    )(h, w_main, w_f)


def _head_masks():
    lane = lax.broadcasted_iota(jnp.int32, (1, LANES), 1)
    return lane < HEAD_DIM


def _dilated_kernel(q_ref, k_ref, v_ref, bm_ref, o_ref, m_s, l_s, acc_s, *, seq):
    lo = _head_masks()
    w = DIL_W
    for g, (_, dil) in enumerate(DIL_PATTERNS):
        nb = (seq // dil) // w

        def rows(start, size, dil=dil):
            return pl.ds(start, size) if dil == 1 else pl.ds(start, size, stride=dil)

        def block(idx, carry, g=g, dil=dil, nb=nb, rows=rows):
            r = idx // nb
            n = idx % nb
            q_rows = rows(n * w * dil + r, w)
            kv_rows = rows(jnp.maximum(n - 1, 0) * w * dil + r, 2 * w)
            var = jnp.where(n == 0, 1, 0)
            q = q_ref[q_rows, :] * ATTN_SCALE
            k = k_ref[kv_rows, :].astype(BF16)
            v = v_ref[kv_rows, :].astype(BF16)
            stats = []
            for h in range(2):
                sel = lo if h == 0 else jnp.logical_not(lo)
                qh = jnp.where(sel, q, 0.0).astype(BF16)
                s = lax.dot_general(qh, k, NT_DIMS, preferred_element_type=F32)
                s = s + bm_ref[g, var, h]
                mb = jnp.max(s, axis=1, keepdims=True)
                p = jnp.exp(s - mb)
                lb = jnp.sum(p, axis=1, keepdims=True)
                pv = jnp.dot(p.astype(BF16), v, preferred_element_type=F32)
                stats.append((mb, lb, pv))
            mb = jnp.where(lo, stats[0][0], stats[1][0])
            lb = jnp.where(lo, stats[0][1], stats[1][1])
            pv = jnp.where(lo, stats[0][2], stats[1][2])
            if g == 0:
                m_s[q_rows, :] = mb
                l_s[q_rows, :] = lb
                acc_s[q_rows, :] = pv
            else:
                m_old = m_s[q_rows, :]
                m_new = jnp.maximum(m_old, mb)
                a_old = jnp.exp(m_old - m_new)
                a_blk = jnp.exp(mb - m_new)
                m_s[q_rows, :] = m_new
                l_s[q_rows, :] = a_old * l_s[q_rows, :] + a_blk * lb
                acc_s[q_rows, :] = a_old * acc_s[q_rows, :] + a_blk * pv
            return carry

        lax.fori_loop(0, nb * dil, block, 0)
    o_ref[...] = acc_s[...] / l_s[...]


def _dilated_attention(qkv, bm_a, bsz, seq):
    npair = H_A // 2
    kern = functools.partial(_dilated_kernel, seq=seq)

    def col(off):
        return pl.BlockSpec((None, seq, LANES), lambda p, b, off=off: (b, 0, off + p))

    return pl.pallas_call(
        kern,
        grid=(npair, bsz),
        in_specs=[col(0), col(npair), col(2 * npair),
                  pl.BlockSpec((len(DIL_PATTERNS), 2, 2, DIL_W, 2 * DIL_W),
                               lambda p, b: (0, 0, p, 0, 0))],
        out_specs=pl.BlockSpec((None, seq, LANES), lambda p, b: (b, 0, p)),
        out_shape=jax.ShapeDtypeStruct((bsz, seq, W_A), F32),
        scratch_shapes=[pltpu.VMEM((seq, LANES), F32)] * 3,
        compiler_params=_cparams("parallel", "parallel"),
    )(qkv, qkv, qkv, bm_a)


def _dilated_bias_tiles(rel_a):
    w = DIL_W
    a = jnp.arange(w)[:, None]
    j = jnp.arange(2 * w)[None, :]
    tiles = []
    for (_, dil) in DIL_PATTERNS:
        per_variant = []
        for dist, mask in ((a + w - j, (a + w - j >= 0) & (a + w - j <= w)),
                           (a - j, a - j >= 0)):
            bias = jnp.moveaxis(rel_a[_t5_bucket(dist * dil)], -1, 0)
            per_variant.append(jnp.where(mask[None], bias, NEG))
        tiles.append(jnp.stack(per_variant))
    return jnp.stack(tiles).astype(F32)


def _softmax_step(s, v_bf, m_ref, l_ref, acc_ref, first):
    reps = s.shape[1] // LANES
    mb = jnp.max(s, axis=1, keepdims=True)
    if first:
        m_new = jnp.broadcast_to(mb, m_ref.shape)
    else:
        m_old = m_ref[...]
        m_new = jnp.maximum(m_old, mb)
        alpha = jnp.exp(m_old - m_new)
    p = jnp.exp(s - jnp.tile(m_new, (1, reps)))
    lb = jnp.sum(p, axis=1, keepdims=True)
    pv = jnp.dot(p.astype(BF16), v_bf, preferred_element_type=F32)
    if first:
        l_ref[...] = jnp.broadcast_to(lb, l_ref.shape)
        acc_ref[...] = pv
    else:
        l_ref[...] = alpha * l_ref[...] + lb
        acc_ref[...] = alpha * acc_ref[...] + pv
    m_ref[...] = m_new


def _moba_kernel(q_ref, k_ref, v_ref, bt_ref, o_ref,
                 kbf_s, vbf_s, kmean_s, m_s, l_s, acc_s, *, nblk, ntile):
    blk = MOBA_BLOCK
    qi = pl.program_id(2)
    lo = _head_masks()

    @pl.when(qi == 0)
    def _():
        kbf_s[...] = k_ref[...].astype(BF16)
        vbf_s[...] = v_ref[...].astype(BF16)
        kmean_s[...] = jnp.zeros_like(kmean_s)
        for n in range(nblk):
            kmean_s[n:n + 1, :] = jnp.mean(k_ref[n * blk:(n + 1) * blk, :], axis=0, keepdims=True)

    q = q_ref[...]
    blk_id = lax.broadcasted_iota(jnp.int32, (blk, LANES), 1)
    blk_idf = blk_id.astype(F32)
    past = blk_id < qi
    outs = []
    for h in range(2):
        sel_h = lo if h == 0 else jnp.logical_not(lo)
        qh = jnp.where(sel_h, q, 0.0)
        gate = lax.dot_general(qh, kmean_s[...], NT_DIMS, precision=lax.Precision.HIGHEST,
                               preferred_element_type=F32)
        gate = jnp.where(past, gate, NEG)
        chosen = jnp.zeros((blk, LANES), jnp.bool_)
        for _ in range(MOBA_TOPK):
            mx = jnp.max(gate, axis=1, keepdims=True)
            first_idx = jnp.min(jnp.where(gate == mx, blk_idf, float(LANES)), axis=1, keepdims=True)
            pick = blk_idf == first_idx
            chosen = jnp.logical_or(chosen, pick)
            gate = jnp.where(pick, -jnp.inf, gate)
        rowbias = jnp.where(jnp.logical_and(chosen, past), 0.0, NEG)

        qs = (qh * ATTN_SCALE).astype(BF16)
        m_r, l_r, acc_r = m_s.at[h], l_s.at[h], acc_s.at[h]

        rows_own = pl.ds(pl.multiple_of(qi * blk, blk), blk)
        s = lax.dot_general(qs, kbf_s[rows_own, :], NT_DIMS, preferred_element_type=F32)
        s = s + bt_ref[h, 0]
        _softmax_step(s, vbf_s[rows_own, :], m_r, l_r, acc_r, first=True)

        def body(kj, carry, h=h, qs=qs, rowbias=rowbias, m_r=m_r, l_r=l_r, acc_r=acc_r):
            rows = pl.ds(pl.multiple_of(kj * blk, blk), blk)
            s = lax.dot_general(qs, kbf_s[rows, :], NT_DIMS, preferred_element_type=F32)
            tile = jnp.minimum(qi - kj, ntile - 1)
            rb = jnp.min(jnp.where(blk_id == kj, rowbias, 0.0), axis=1, keepdims=True)
            s = s + bt_ref[h, tile] + rb
            _softmax_step(s, vbf_s[rows, :], m_r, l_r, acc_r, first=False)
            return carry

        lax.fori_loop(0, qi, body, 0)
        outs.append(acc_r[...] / l_r[...])
    o_ref[...] = jnp.where(lo, outs[0], outs[1])


def _moba_attention(qkv, bt_b, bsz, seq, col0):
    npair = H_B // 2
    nblk = seq // MOBA_BLOCK
    ntile = bt_b.shape[1]
    kern = functools.partial(_moba_kernel, nblk=nblk, ntile=ntile)
    blk = MOBA_BLOCK

    def col(off):
        return pl.BlockSpec((None, seq, LANES), lambda p, b, i, off=off: (b, 0, off + p))

    return pl.pallas_call(
        kern,
        grid=(npair, bsz, nblk),
        in_specs=[pl.BlockSpec((None, blk, LANES), lambda p, b, i: (b, i, col0 + p)),
                  col(col0 + npair), col(col0 + 2 * npair),
                  pl.BlockSpec((2, ntile, blk, blk), lambda p, b, i: (p, 0, 0, 0))],
        out_specs=pl.BlockSpec((None, blk, LANES), lambda p, b, i: (b, i, p)),
        out_shape=jax.ShapeDtypeStruct((bsz, seq, W_B), F32),
        scratch_shapes=[pltpu.VMEM((seq, LANES), BF16), pltpu.VMEM((seq, LANES), BF16),
                        pltpu.VMEM((LANES, LANES), F32),
                        pltpu.VMEM((2, blk, LANES), F32), pltpu.VMEM((2, blk, LANES), F32),
                        pltpu.VMEM((2, blk, LANES), F32)],
        compiler_params=_cparams("parallel", "parallel", "arbitrary"),
    )(qkv, qkv, qkv, bt_b)


def _moba_bias_tiles(rel_b, seq):
    blk = MOBA_BLOCK
    nblk = seq // blk
    first_const = 0
    d_all = np.arange(seq)
    nf = np.maximum(d_all, 1).astype(np.float32)
    bucket_np = np.where(d_all < 16, d_all,
                         np.minimum(16 + (np.log(nf / 16) / math.log(REL_MAX_DIST / 16) * 16).astype(np.int32), 31))
    last_change = int(np.max(np.nonzero(np.diff(bucket_np))[0])) + 1 if seq > 1 else 0
    first_const = min(nblk - 1, -(-(last_change + 2 + blk - 1) // blk))
    ntile = first_const + 1
    a = jnp.arange(blk)[:, None]
    j = jnp.arange(blk)[None, :]
    off = jnp.arange(ntile)[:, None, None]
    d = off * blk + a - j
    bias = jnp.moveaxis(rel_b[_t5_bucket(d)], -1, 0)
    causal = (d >= 0)[None]
    return jnp.where(causal, bias, NEG).astype(F32)


def _fgate_kernel(f_ref, bf_ref, o_ref, *, chunks):
    z = f_ref[...] + bf_ref[...]
    ls = jnp.minimum(z, 0.0) - jnp.log(1.0 + jnp.exp(-jnp.abs(z)))
    n = ls.shape[0]
    i = lax.broadcasted_iota(jnp.int32, (LANES, LANES), 0)
    j = lax.broadcasted_iota(jnp.int32, (LANES, LANES), 1)
    upper = (i <= j).astype(F32)
    within = jnp.dot(ls, upper, precision=lax.Precision.HIGHEST, preferred_element_type=F32)
    r = lax.broadcasted_iota(jnp.int32, (n, n), 0)
    c = lax.broadcasted_iota(jnp.int32, (n, n), 1)
    prior = jnp.logical_and(r // chunks == c // chunks, c < r).astype(F32)
    carry = jnp.dot(prior, within, precision=lax.Precision.HIGHEST, preferred_element_type=F32)
    o_ref[...] = within + carry[:, LANES - 1:LANES]


def _forget_cumsum(f_t, b_f):
    bsz, hp, seq = f_t.shape
    chunks = seq // LANES
    rows = hp * chunks
    bf = jnp.repeat(jnp.pad(b_f, (0, hp - b_f.shape[0])), chunks).reshape(rows, 1)
    out = pl.pallas_call(
        functools.partial(_fgate_kernel, chunks=chunks),
        grid=(bsz,),
        in_specs=[pl.BlockSpec((None, rows, LANES), lambda b: (b, 0, 0)),
                  pl.BlockSpec((rows, 1), lambda b: (0, 0))],
        out_specs=pl.BlockSpec((None, rows, LANES), lambda b: (b, 0, 0)),
        out_shape=jax.ShapeDtypeStruct((bsz, rows, LANES), F32),
        compiler_params=_cparams("parallel"),
    )(f_t.reshape(bsz, rows, LANES), bf)
    return out.reshape(bsz, hp, seq)


def _fox_kernel(q_ref, k_ref, v_ref, cum_ref, o_ref, kbf_s, vbf_s, m_s, l_s, acc_s, *, tq):
    p_id = pl.program_id(0)
    qi = pl.program_id(2)
    lo = _head_masks()

    @pl.when(qi == 0)
    def _():
        kbf_s[...] = k_ref[...].astype(BF16)
        vbf_s[...] = v_ref[...].astype(BF16)

    q = q_ref[...] * ATTN_SCALE
    a = lax.broadcasted_iota(jnp.int32, (tq, tq), 0)
    j = lax.broadcasted_iota(jnp.int32, (tq, tq), 1)
    causal = j <= a
    rows_own = pl.ds(pl.multiple_of(qi * tq, tq), tq)
    outs = []
    for h in range(2):
        sel_h = lo if h == 0 else jnp.logical_not(lo)
        qs = jnp.where(sel_h, q, 0.0).astype(BF16)
        head = pl.ds(2 * p_id + h, 1)
        m_r, l_r, acc_r = m_s.at[h], l_s.at[h], acc_s.at[h]
        c_own = cum_ref[head, rows_own]
        c0 = jnp.max(c_own, axis=1, keepdims=True)
        s = lax.dot_general(qs, kbf_s[rows_own, :], NT_DIMS, preferred_element_type=F32)
        s = jnp.where(causal, s - (c_own - c0), NEG)
        _softmax_step(s, vbf_s[rows_own, :], m_r, l_r, acc_r, first=True)

        def body(kj, carry, qs=qs, head=head, c0=c0, m_r=m_r, l_r=l_r, acc_r=acc_r):
            rows = pl.ds(pl.multiple_of(kj * tq, tq), tq)
            s = lax.dot_general(qs, kbf_s[rows, :], NT_DIMS, preferred_element_type=F32)
            s = s - (cum_ref[head, rows] - c0)
            _softmax_step(s, vbf_s[rows, :], m_r, l_r, acc_r, first=False)
            return carry

        lax.fori_loop(0, qi, body, 0)
        outs.append(acc_r[...] / l_r[...])
    o_ref[...] = jnp.where(lo, outs[0], outs[1])


def _fox_attention(qkv, cum, bsz, seq, col0):
    npair = H_C // 2
    tq = 256
    hp = cum.shape[1]
    kern = functools.partial(_fox_kernel, tq=tq)

    def col(off):
        return pl.BlockSpec((None, seq, LANES), lambda p, b, i, off=off: (b, 0, off + p))

    return pl.pallas_call(
        kern,
        grid=(npair, bsz, seq // tq),
        in_specs=[pl.BlockSpec((None, tq, LANES), lambda p, b, i: (b, i, col0 + p)),
                  col(col0 + npair), col(col0 + 2 * npair),
                  pl.BlockSpec((None, hp, seq), lambda p, b, i: (b, 0, 0))],
        out_specs=pl.BlockSpec((None, tq, LANES), lambda p, b, i: (b, i, p)),
        out_shape=jax.ShapeDtypeStruct((bsz, seq, W_C), F32),
        scratch_shapes=[pltpu.VMEM((seq, LANES), BF16), pltpu.VMEM((seq, LANES), BF16),
                        pltpu.VMEM((2, tq, LANES), F32), pltpu.VMEM((2, tq, LANES), F32),
                        pltpu.VMEM((2, tq, LANES), F32)],
        compiler_params=_cparams("parallel", "parallel", "arbitrary"),
    )(qkv, qkv, qkv, cum)


def _layer_norm_rows(z, g, b):
    mu = jnp.mean(z, axis=-1, keepdims=True)
    zc = z - mu
    var = jnp.mean(zc * zc, axis=-1, keepdims=True)
    return zc * lax.rsqrt(var + LN_EPS) * g + b


def _mixout_kernel(oa_ref, ob_ref, oc_ref, ga_ref, gb_ref, gc_ref, wa_ref, wb_ref, wc_ref,
                   x_ref, g1_ref, lng_ref, lnb_ref, sc_ref, sh_ref, x_out, h_out, *, alpha):
    def normed(o_ref, g_ref):
        o = o_ref[...]
        ms = jnp.mean(o * o, axis=-1, keepdims=True)
        return (o * lax.rsqrt(ms + LN_EPS) * g_ref[...]).astype(BF16)

    y = jnp.dot(normed(oa_ref, ga_ref), wa_ref[...], preferred_element_type=F32)
    y = y + jnp.dot(normed(ob_ref, gb_ref), wb_ref[...], preferred_element_type=F32)
    y = y + jnp.dot(normed(oc_ref, gc_ref), wc_ref[...], preferred_element_type=F32)
    x1 = _layer_norm_rows(alpha * x_ref[...] + (1.0 + g1_ref[...]) * y, lng_ref[...], lnb_ref[...])
    x_out[...] = x1
    h_out[...] = (x1 * (1.0 + sc_ref[...]) + sh_ref[...]).astype(BF16)


def _mix_out(o_a, o_b, o_c, g_a, g_b, g_c, w_out_bf, x, g1, ln_g, ln_b, sc2, sh2, alpha):
    bsz, seq, d = x.shape
    tm = 256
    w_a, w_b, w_c = w_out_bf[:W_A], w_out_bf[W_A:W_A + W_B], w_out_bf[W_A + W_B:]

    def act(width):
        return pl.BlockSpec((None, tm, width), lambda b, i: (b, i, 0))

    def const(shape):
        return pl.BlockSpec(shape, lambda b, i: (0,) * len(shape))

    bvec = pl.BlockSpec((None, 1, d), lambda b, i: (b, 0, 0))
    return pl.pallas_call(
        functools.partial(_mixout_kernel, alpha=alpha),
        grid=(bsz, seq // tm),
        in_specs=[act(W_A), act(W_B), act(W_C),
                  const((1, W_A)), const((1, W_B)), const((1, W_C)),
                  const((W_A, d)), const((W_B, d)), const((W_C, d)),
                  act(d), bvec, const((1, d)), const((1, d)), bvec, bvec],
        out_specs=[act(d), act(d)],
        out_shape=[jax.ShapeDtypeStruct((bsz, seq, d), F32),
                   jax.ShapeDtypeStruct((bsz, seq, d), BF16)],
        compiler_params=_cparams("parallel", "parallel"),
    )(o_a, o_b, o_c, g_a.reshape(1, -1), g_b.reshape(1, -1), g_c.reshape(1, -1),
      w_a, w_b, w_c, x, g1, ln_g.reshape(1, d), ln_b.reshape(1, d), sc2, sh2)


FFN_HALO = 8


def _ffn_up_kernel(h_ref, wa_ref, wb_ref, cwa_ref, cwb_ref, cba_ref, cbb_ref, o_ref, bufa, bufb):
    i = pl.program_id(2)
    tm = h_ref.shape[0]
    h = h_ref[...]

    def conv(w_ref, cw_ref, cb_ref, buf):
        @pl.when(i == 0)
        def _():
            buf[0:FFN_HALO, :] = jnp.zeros((FFN_HALO, buf.shape[1]), F32)

        u = jnp.dot(h, w_ref[...], preferred_element_type=F32)
        buf[FFN_HALO:FFN_HALO + tm, :] = u
        out = (cw_ref[2:3, :] * u
               + cw_ref[1:2, :] * buf[FFN_HALO - 1:FFN_HALO - 1 + tm, :]
               + cw_ref[0:1, :] * buf[FFN_HALO - 2:FFN_HALO - 2 + tm, :]
               + cb_ref[...])
        buf[0:FFN_HALO, :] = u[tm - FFN_HALO:tm, :]
        return out

    a = conv(wa_ref, cwa_ref, cba_ref, bufa)
    b = conv(wb_ref, cwb_ref, cbb_ref, bufb)
    o_ref[...] = (a * (1.0 / (1.0 + jnp.exp(-a))) * b).astype(o_ref.dtype)


def _ffn_up(h, w_up_bf, conv_w, conv_b):
    bsz, seq, d = h.shape
    dff = w_up_bf.shape[1] // 2
    tm, tn = 512, 512
    nj = dff // tn
    cb = conv_b.reshape(1, -1)
    return pl.pallas_call(
        _ffn_up_kernel,
        grid=(nj, bsz, seq // tm),
        in_specs=[pl.BlockSpec((None, tm, d), lambda j, b, i: (b, i, 0)),
                  pl.BlockSpec((d, tn), lambda j, b, i: (0, j)),
                  pl.BlockSpec((d, tn), lambda j, b, i: (0, nj + j)),
                  pl.BlockSpec((CONV_WIDTH, tn), lambda j, b, i: (0, j)),
                  pl.BlockSpec((CONV_WIDTH, tn), lambda j, b, i: (0, nj + j)),
                  pl.BlockSpec((1, tn), lambda j, b, i: (0, j)),
                  pl.BlockSpec((1, tn), lambda j, b, i: (0, nj + j))],
        out_specs=pl.BlockSpec((None, tm, tn), lambda j, b, i: (b, i, j)),
        out_shape=jax.ShapeDtypeStruct((bsz, seq, dff), BF16),
        scratch_shapes=[pltpu.VMEM((tm + FFN_HALO, tn), F32)] * 2,
        compiler_params=_cparams("parallel", "parallel", "arbitrary"),
    )(h, w_up_bf, w_up_bf, conv_w, conv_w, cb, cb)


def _ffn_down_kernel(g_ref, w_ref, x_ref, g2_ref, lng_ref, lnb_ref, sc_ref, sh_ref,
                     x_out, h_out, acc_s, *, alpha):
    k = pl.program_id(2)

    @pl.when(k == 0)
    def _():
        acc_s[...] = jnp.zeros_like(acc_s)

    acc_s[...] += jnp.dot(g_ref[...], w_ref[...], preferred_element_type=F32)

    @pl.when(k == pl.num_programs(2) - 1)
    def _():
        x2 = _layer_norm_rows(alpha * x_ref[...] + (1.0 + g2_ref[...]) * acc_s[...],
                              lng_ref[...], lnb_ref[...])
        x_out[...] = x2
        h_out[...] = (x2 * (1.0 + sc_ref[...]) + sh_ref[...]).astype(BF16)


def _ffn_down(g, w_down_bf, x, g2, ln_g, ln_b, sc_next, sh_next, alpha):
    bsz, seq, d = x.shape
    dff = g.shape[2]
    tm, tk = 512, 512
    act = pl.BlockSpec((None, tm, d), lambda b, i, k: (b, i, 0))
    bvec = pl.BlockSpec((None, 1, d), lambda b, i, k: (b, 0, 0))
    const = pl.BlockSpec((1, d), lambda b, i, k: (0, 0))
    return pl.pallas_call(
        functools.partial(_ffn_down_kernel, alpha=alpha),
        grid=(bsz, seq // tm, dff // tk),
        in_specs=[pl.BlockSpec((None, tm, tk), lambda b, i, k: (b, i, k)),
                  pl.BlockSpec((tk, d), lambda b, i, k: (k, 0)),
                  act, bvec, const, const, bvec, bvec],
        out_specs=[act, act],
        out_shape=[jax.ShapeDtypeStruct((bsz, seq, d), F32),
                   jax.ShapeDtypeStruct((bsz, seq, d), BF16)],
        scratch_shapes=[pltpu.VMEM((tm, d), F32)],
        compiler_params=_cparams("parallel", "parallel", "arbitrary"),
    )(g, w_down_bf, x, g2, ln_g.reshape(1, d), ln_b.reshape(1, d), sc_next, sh_next)


def kernel(x, c, rel_bias, w_ada, b_ada, w_in, b_f, g_mix_a, g_mix_b, g_mix_c, w_out,
           ln1_g, ln1_b, w_up, conv_w, conv_b, w_down, ln2_g, ln2_b):
    bsz, seq, d = x.shape
    depth = w_ada.shape[0]
    alpha = (2 * depth) ** 0.25
    n_qkv = 3 * (W_A + W_B + W_C)

    mod = _ada_mod(c, w_ada, b_ada)

    def mod_vec(l, idx):
        return mod[l, :, idx * d:(idx + 1) * d].reshape(bsz, 1, d)

    bm_a = _dilated_bias_tiles(rel_bias[:, :H_A])
    bt_b = _moba_bias_tiles(rel_bias[:, H_A:], seq)

    h = _modulate(x, mod_vec(0, 1), mod_vec(0, 0))
    for l in range(depth):
        w_in_bf = w_in[l].astype(BF16)
        w_f = jnp.pad(w_in_bf[:, n_qkv:], ((0, 0), (0, LANES - H_C)))
        qkv, f = _inproj(h.reshape(bsz * seq, d), w_in_bf[:, :n_qkv], w_f)
        qkv = qkv.reshape(bsz, seq, n_qkv)
        f_t = jnp.transpose(f.reshape(bsz, seq, LANES)[:, :, :16], (0, 2, 1))
        cum = _forget_cumsum(f_t, b_f[l])

        o_a = _dilated_attention(qkv, bm_a, bsz, seq)
        o_b = _moba_attention(qkv, bt_b, bsz, seq, 3 * W_A // LANES)
        o_c = _fox_attention(qkv, cum, bsz, seq, 3 * (W_A + W_B) // LANES)

        x, h = _mix_out(o_a, o_b, o_c, g_mix_a[l], g_mix_b[l], g_mix_c[l], w_out[l].astype(BF16),
                        x, mod_vec(l, 2), ln1_g[l], ln1_b[l], mod_vec(l, 4), mod_vec(l, 3), alpha)
        g = _ffn_up(h, w_up[l].astype(BF16), conv_w[l], conv_b[l])
        nxt = min(l + 1, depth - 1)
        x, h = _ffn_down(g, w_down[l].astype(BF16), x, mod_vec(l, 5), ln2_g[l], ln2_b[l],
                         mod_vec(nxt, 1), mod_vec(nxt, 0), alpha)
    return x
```

```python
import functools
import math

import numpy as np
import jax
import jax.numpy as jnp
from jax import lax
from jax.experimental import pallas as pl
from jax.experimental.pallas import tpu as pltpu

F32 = jnp.float32
BF16 = jnp.bfloat16

HEAD_DIM = 64
LANES = 128
H_A, H_B, H_C = 12, 10, 10
W_A, W_B, W_C = H_A * HEAD_DIM, H_B * HEAD_DIM, H_C * HEAD_DIM
DIL_PATTERNS = ((128, 1), (512, 4), (2048, 16))
DIL_W = 128
MOBA_BLOCK = 256
MOBA_TOPK = 3
N_BUCKETS = 32
REL_MAX_DIST = 2048
CONV_WIDTH = 3
LN_EPS = 1e-5
NEG = -1e30
ATTN_SCALE = HEAD_DIM ** -0.5
VMEM_LIMIT = 56 * 1024 * 1024


def _cparams(*sem):
    return pltpu.CompilerParams(dimension_semantics=sem, vmem_limit_bytes=VMEM_LIMIT)


def _t5_bucket(dist):
    n = jnp.maximum(dist, 0)
    max_exact = N_BUCKETS // 2
    nf = jnp.maximum(n, 1).astype(F32)
    large = max_exact + (jnp.log(nf / max_exact) / math.log(REL_MAX_DIST / max_exact)
                         * (N_BUCKETS - max_exact)).astype(jnp.int32)
    large = jnp.minimum(large, N_BUCKETS - 1)
    return jnp.where(n < max_exact, n, large)


def _ada_kernel(c_ref, w_ref, b_ref, o_ref):
    c = c_ref[...]
    cond = c * (1.0 / (1.0 + jnp.exp(-c)))
    o_ref[...] = jnp.dot(cond, w_ref[...], precision=lax.Precision.HIGHEST,
                         preferred_element_type=F32) + b_ref[...]


def _ada_mod(c, w_ada, b_ada):
    depth, d, n = w_ada.shape
    bsz = c.shape[0]
    tn = 1024
    return pl.pallas_call(
        _ada_kernel,
        grid=(depth, n // tn),
        in_specs=[pl.BlockSpec((bsz, d), lambda l, j: (0, 0)),
                  pl.BlockSpec((None, d, tn), lambda l, j: (l, 0, j)),
                  pl.BlockSpec((None, 1, tn), lambda l, j: (l, 0, j))],
        out_specs=pl.BlockSpec((None, bsz, tn), lambda l, j: (l, 0, j)),
        out_shape=jax.ShapeDtypeStruct((depth, bsz, n), F32),
        compiler_params=_cparams("parallel", "parallel"),
    )(c, w_ada, b_ada.reshape(depth, 1, n))


def _modulate_kernel(x_ref, sc_ref, sh_ref, o_ref):
    o_ref[...] = (x_ref[...] * (1.0 + sc_ref[...]) + sh_ref[...]).astype(o_ref.dtype)


def _modulate(x, sc, sh):
    bsz, s, d = x.shape
    tm = 512
    vec = pl.BlockSpec((None, 1, d), lambda b, i: (b, 0, 0))
    return pl.pallas_call(
        _modulate_kernel,
        grid=(bsz, s // tm),
        in_specs=[pl.BlockSpec((None, tm, d), lambda b, i: (b, i, 0)), vec, vec],
        out_specs=pl.BlockSpec((None, tm, d), lambda b, i: (b, i, 0)),
        out_shape=jax.ShapeDtypeStruct((bsz, s, d), BF16),
        compiler_params=_cparams("parallel", "parallel"),
    )(x, sc, sh)


def _inproj_kernel(a_ref, w_ref, wf_ref, o_ref, f_ref):
    a = a_ref[...]
    o_ref[...] = jnp.dot(a, w_ref[...], preferred_element_type=F32)

    @pl.when(pl.program_id(1) == 0)
    def _():
        f_ref[...] = jnp.dot(a, wf_ref[...], preferred_element_type=F32)


def _inproj(h, w_main, w_f):
    t, d = h.shape
    n = w_main.shape[1]
    tm, tn = 512, 1024
    return pl.pallas_call(
        _inproj_kernel,
        grid=(t // tm, n // tn),
        in_specs=[pl.BlockSpec((tm, d), lambda i, j: (i, 0)),
                  pl.BlockSpec((d, tn), lambda i, j: (0, j)),
                  pl.BlockSpec((d, LANES), lambda i, j: (0, 0))],
        out_specs=[pl.BlockSpec((tm, tn), lambda i, j: (i, j)),
                   pl.BlockSpec((tm, LANES), lambda i, j: (i, 0))],
        out_shape=[jax.ShapeDtypeStruct((t, n), F32),
                   jax.ShapeDtypeStruct((t, LANES), F32)],
        compiler_params=_cparams("parallel", "arbitrary"),
    )(h, w_main, w_f)


TOEP_LEN = 512
TOEP_ROWS = 256


def _toeplitz_kernel(rr_ref, o_ref):
    x = jnp.broadcast_to(rr_ref[...], (TOEP_ROWS, TOEP_LEN))
    y = pltpu.roll(x, TOEP_LEN - TOEP_ROWS + 1, 1, stride=1, stride_axis=0)
    o_ref[...] = y[:, :o_ref.shape[1]]


def _toeplitz(rr, cols):
    g = rr.shape[0]
    return pl.pallas_call(
        _toeplitz_kernel,
        grid=(g,),
        in_specs=[pl.BlockSpec((None, 1, TOEP_LEN), lambda i: (i, 0, 0))],
        out_specs=pl.BlockSpec((None, TOEP_ROWS, cols), lambda i: (i, 0, 0)),
        out_shape=jax.ShapeDtypeStruct((g, TOEP_ROWS, cols), F32),
        compiler_params=_cparams("parallel"),
    )(rr.reshape(g, 1, TOEP_LEN))


def _bias_of_distance(rel, dist, valid):
    bias = jnp.moveaxis(rel[_t5_bucket(dist)], -1, 0)
    return jnp.where(valid, bias, NEG).astype(F32)


def _dilated_bias_tiles(rel_a):
    c = jnp.arange(TOEP_LEN)
    rows = []
    for (_, dil) in DIL_PATTERNS:
        d0 = c - (TOEP_ROWS - 1) + DIL_W
        d1 = c - (TOEP_ROWS - 1)
        rows.append(jnp.stack([_bias_of_distance(rel_a, d0 * dil, (d0 >= 0) & (d0 <= DIL_W)),
                               _bias_of_distance(rel_a, d1 * dil, d1 >= 0)]))
    rr = jnp.stack(rows)
    out = _toeplitz(rr.reshape(-1, TOEP_LEN), DIL_W)
    return out.reshape(len(DIL_PATTERNS), 2, H_A, TOEP_ROWS, DIL_W)


def _moba_num_tiles(seq):
    blk = MOBA_BLOCK
    d_all = np.arange(max(seq, 2))
    nf = np.maximum(d_all, 1).astype(np.float32)
    half = N_BUCKETS // 2
    big = half + (np.log(nf / half) / math.log(REL_MAX_DIST / half) * (N_BUCKETS - half)).astype(np.int32)
    bucket = np.where(d_all < half, d_all, np.minimum(big, N_BUCKETS - 1))
    last_change = int(np.max(np.nonzero(np.diff(bucket))[0])) + 1
    first_const = min(seq // blk - 1, -(-(last_change + 2 + blk - 1) // blk))
    return first_const + 1


def _moba_bias_tiles(rel_b, seq):
    ntile = _moba_num_tiles(seq)
    c = jnp.arange(TOEP_LEN)[None, :]
    d = jnp.arange(ntile)[:, None] * MOBA_BLOCK + c - (TOEP_ROWS - 1)
    rr = _bias_of_distance(rel_b, d, d >= 0)
    out = _toeplitz(rr.reshape(-1, TOEP_LEN), MOBA_BLOCK)
    return out.reshape(H_B, ntile, TOEP_ROWS, MOBA_BLOCK)


def _head_rows():
    row = lax.broadcasted_iota(jnp.int32, (LANES, 1), 0)
    return row < HEAD_DIM, row >= HEAD_DIM


def _dilated_kernel(q_ref, k_ref, v_ref, bm_ref, o_ref, og_s, lse_s, *, seq):
    w = DIL_W
    own_rows = _head_rows()
    for g, (_, dil) in enumerate(DIL_PATTERNS):
        nb = (seq // dil) // w

        def rows(start, size, dil=dil):
            return pl.ds(start, size) if dil == 1 else pl.ds(start, size, stride=dil)

        def block(idx, carry, g=g, dil=dil, nb=nb, rows=rows):
            r = idx // nb
            n = idx % nb
            q_rows = rows(n * w * dil + r, w)
            kv_rows = rows(jnp.maximum(n - 1, 0) * w * dil + r, 2 * w)
            var = jnp.where(n == 0, 1, 0)
            qt = q_ref[q_rows, :].T * ATTN_SCALE
            k = k_ref[kv_rows, :].astype(BF16)
            vt = v_ref[kv_rows, :].T.astype(BF16)
            o_parts, lse_parts = [], []
            for h in range(2):
                qh = jnp.where(own_rows[h], qt, 0.0).astype(BF16)
                s = jnp.dot(k, qh, preferred_element_type=F32) + bm_ref[g, var, h]
                m = jnp.max(s, axis=0, keepdims=True)
                p = jnp.exp(s - m)
                l = jnp.sum(p, axis=0, keepdims=True)
                pv = jnp.dot(vt[h * HEAD_DIM:(h + 1) * HEAD_DIM, :], p.astype(BF16),
                             preferred_element_type=F32)
                o_parts.append(pv / l)
                lse_parts.append(jnp.broadcast_to(m + jnp.log(l), (HEAD_DIM, w)))
            og_s[g, q_rows, :] = jnp.concatenate(o_parts, axis=0).T
            lse_s[g, q_rows, :] = jnp.concatenate(lse_parts, axis=0).T
            return carry

        lax.fori_loop(0, nb * dil, block, 0)

    top = jnp.maximum(jnp.maximum(lse_s[0], lse_s[1]), lse_s[2])
    num = jnp.zeros_like(top)
    den = jnp.zeros_like(top)
    for g in range(len(DIL_PATTERNS)):
        wg = jnp.exp(lse_s[g] - top)
        num = num + wg * og_s[g]
        den = den + wg
    o_ref[...] = num / den


def _dilated_attention(qkv, bm_a, bsz, seq):
    npair = H_A // 2
    npat = len(DIL_PATTERNS)
    kern = functools.partial(_dilated_kernel, seq=seq)

    def col(off):
        return pl.BlockSpec((None, seq, LANES), lambda p, b, off=off: (b, 0, off + p))

    return pl.pallas_call(
        kern,
        grid=(npair, bsz),
        in_specs=[col(0), col(npair), col(2 * npair),
                  pl.BlockSpec((npat, 2, 2, 2 * DIL_W, DIL_W), lambda p, b: (0, 0, p, 0, 0))],
        out_specs=pl.BlockSpec((None, seq, LANES), lambda p, b: (b, 0, p)),
        out_shape=jax.ShapeDtypeStruct((bsz, seq, W_A), F32),
        scratch_shapes=[pltpu.VMEM((npat, seq, LANES), F32)] * 2,
        compiler_params=_cparams("parallel", "parallel"),
    )(qkv, qkv, qkv, bm_a)


def _transpose_to(dst_ref, src_ref, chunk=512):
    for c in range(src_ref.shape[0] // chunk):
        dst_ref[:, c * chunk:(c + 1) * chunk] = src_ref[c * chunk:(c + 1) * chunk, :].T.astype(BF16)


def _flash_first(s, vt_h):
    m = jnp.max(s, axis=0, keepdims=True)
    p = jnp.exp(s - m)
    l = jnp.sum(p, axis=0, keepdims=True)
    return m, l, jnp.dot(vt_h, p.astype(BF16), preferred_element_type=F32)


def _flash_next(s, vt_h, m, l, acc):
    m_new = jnp.maximum(m, jnp.max(s, axis=0, keepdims=True))
    alpha = jnp.exp(m - m_new)
    p = jnp.exp(s - m_new)
    l_new = alpha * l + jnp.sum(p, axis=0, keepdims=True)
    acc_new = alpha * acc + jnp.dot(vt_h, p.astype(BF16), preferred_element_type=F32)
    return m_new, l_new, acc_new


def _moba_kernel(q_ref, k_ref, v_ref, bt_ref, o_ref,
                 kbf_s, vt_s, kmean_s, selb_s, qts_s, acc_s, *, nblk, ntile):
    blk = MOBA_BLOCK
    qi = pl.program_id(2)

    @pl.when(qi == 0)
    def _():
        kbf_s[...] = k_ref[...].astype(BF16)
        _transpose_to(vt_s, v_ref)
        for n in range(nblk):
            kmean_s[n:n + 1, :] = jnp.mean(k_ref[n * blk:(n + 1) * blk, :], axis=0, keepdims=True)

    qt = q_ref[...].T
    own_rows = _head_rows()
    blk_id = lax.broadcasted_iota(jnp.int32, (nblk, 1), 0)
    blk_idf = blk_id.astype(F32)
    past = blk_id < qi
    rows_own = pl.ds(pl.multiple_of(qi * blk, blk), blk)
    stats = []
    for h in range(2):
        qh = jnp.where(own_rows[h], qt, 0.0)
        gate = jnp.dot(kmean_s[...], qh, precision=lax.Precision.HIGHEST,
                       preferred_element_type=F32)
        gate = jnp.where(past, gate, NEG)
        chosen = jnp.zeros(gate.shape, jnp.bool_)
        for _ in range(MOBA_TOPK):
            mx = jnp.max(gate, axis=0, keepdims=True)
            first_idx = jnp.min(jnp.where(gate == mx, blk_idf, float(nblk)), axis=0, keepdims=True)
            pick = blk_idf == first_idx
            chosen = jnp.logical_or(chosen, pick)
            gate = jnp.where(pick, -jnp.inf, gate)
        selb_s[h] = jnp.where(jnp.logical_and(chosen, past), 0.0, NEG)

        qs = (qh * ATTN_SCALE).astype(BF16)
        qts_s[h] = qs
        s = jnp.dot(kbf_s[rows_own, :], qs, preferred_element_type=F32) + bt_ref[h, 0]
        m, l, acc = _flash_first(s, vt_s[h * HEAD_DIM:(h + 1) * HEAD_DIM, rows_own])
        acc_s[h] = acc
        stats += [m, l]

    def body(kj, carry):
        rows = pl.ds(pl.multiple_of(kj * blk, blk), blk)
        tile = jnp.minimum(qi - kj, ntile - 1)
        k = kbf_s[rows, :]
        out = []
        for h in range(2):
            s = (jnp.dot(k, qts_s[h], preferred_element_type=F32)
                 + bt_ref[h, tile] + selb_s[h, pl.ds(kj, 1), :])
            m, l, acc = _flash_next(s, vt_s[h * HEAD_DIM:(h + 1) * HEAD_DIM, rows],
                                    carry[2 * h], carry[2 * h + 1], acc_s[h])
            acc_s[h] = acc
            out += [m, l]
        return tuple(out)

    _, l0, _, l1 = lax.fori_loop(0, qi, body, tuple(stats))
    o_ref[...] = jnp.concatenate([acc_s[0] / l0, acc_s[1] / l1], axis=0).T


def _moba_attention(qkv, bt_b, bsz, seq, col0):
    npair = H_B // 2
    nblk = seq // MOBA_BLOCK
    ntile = bt_b.shape[1]
    kern = functools.partial(_moba_kernel, nblk=nblk, ntile=ntile)
    blk = MOBA_BLOCK

    def col(off):
        return pl.BlockSpec((None, seq, LANES), lambda p, b, i, off=off: (b, 0, off + p))

    return pl.pallas_call(
        kern,
        grid=(npair, bsz, nblk),
        in_specs=[pl.BlockSpec((None, blk, LANES), lambda p, b, i: (b, i, col0 + p)),
                  col(col0 + npair), col(col0 + 2 * npair),
                  pl.BlockSpec((2, ntile, blk, blk), lambda p, b, i: (p, 0, 0, 0))],
        out_specs=pl.BlockSpec((None, blk, LANES), lambda p, b, i: (b, i, p)),
        out_shape=jax.ShapeDtypeStruct((bsz, seq, W_B), F32),
        scratch_shapes=[pltpu.VMEM((seq, LANES), BF16), pltpu.VMEM((LANES, seq), BF16),
                        pltpu.VMEM((nblk, LANES), F32), pltpu.VMEM((2, nblk, blk), F32),
                        pltpu.VMEM((2, LANES, blk), BF16), pltpu.VMEM((2, HEAD_DIM, blk), F32)],
        compiler_params=_cparams("parallel", "parallel", "arbitrary"),
    )(qkv, qkv, qkv, bt_b)


CUM_SPLIT = 3
CUM_CHUNK = 256


def _fgate_kernel(f_ref, bf_ref, pm_ref, o_ref, cum_s):
    seq = f_ref.shape[0]
    ch = CUM_CHUNK
    i = lax.broadcasted_iota(jnp.int32, (ch, ch), 0)
    j = lax.broadcasted_iota(jnp.int32, (ch, ch), 1)
    lower = (j <= i).astype(F32)
    carry = jnp.zeros((1, LANES), F32)
    for c in range(seq // ch):
        z = f_ref[c * ch:(c + 1) * ch, :] + bf_ref[...]
        ls = jnp.minimum(z, 0.0) - jnp.log(1.0 + jnp.exp(-jnp.abs(z)))
        cum = jnp.dot(lower, ls, precision=lax.Precision.HIGHEST, preferred_element_type=F32) + carry
        cum_s[c * ch:(c + 1) * ch, :] = cum
        carry = cum[ch - 1:ch, :]
    rest = -cum_s[...]
    out = None
    for t in range(CUM_SPLIT):
        term = rest.astype(BF16)
        rest = rest - term.astype(F32)
        part = jnp.dot(term, pm_ref[t], preferred_element_type=F32)
        out = part if out is None else out + part
    o_ref[...] = out.astype(BF16)


def _forget_keys(f, b_f):
    bsz, seq, _ = f.shape
    npair = H_C // 2
    pm = np.zeros((CUM_SPLIT, LANES, npair * LANES), np.float32)
    for t in range(CUM_SPLIT):
        for p in range(npair):
            pm[t, 2 * p, p * LANES + HEAD_DIM + t] = 1.0
            pm[t, 2 * p + 1, p * LANES + t] = 1.0
    bf = jnp.pad(b_f, (0, LANES - b_f.shape[0])).reshape(1, LANES)
    return pl.pallas_call(
        _fgate_kernel,
        grid=(bsz,),
        in_specs=[pl.BlockSpec((None, seq, LANES), lambda b: (b, 0, 0)),
                  pl.BlockSpec((1, LANES), lambda b: (0, 0)),
                  pl.BlockSpec((CUM_SPLIT, LANES, npair * LANES), lambda b: (0, 0, 0))],
        out_specs=pl.BlockSpec((None, seq, npair * LANES), lambda b: (b, 0, 0)),
        out_shape=jax.ShapeDtypeStruct((bsz, seq, npair * LANES), BF16),
        scratch_shapes=[pltpu.VMEM((seq, LANES), F32)],
        compiler_params=_cparams("parallel"),
    )(f, bf, jnp.asarray(pm, BF16))


def _fox_kernel(q_ref, k_ref, v_ref, ca_ref, o_ref, kaug_s, vt_s, qts_s, acc_s, *, tq):
    qi = pl.program_id(2)

    @pl.when(qi == 0)
    def _():
        lane = lax.broadcasted_iota(jnp.int32, (1, LANES), 1)
        k = k_ref[...]
        ca = ca_ref[...].astype(F32)
        kaug_s[0] = jnp.where(lane < HEAD_DIM, k, ca).astype(BF16)
        kaug_s[1] = jnp.where(lane >= HEAD_DIM, k, ca).astype(BF16)
        _transpose_to(vt_s, v_ref)

    qt = q_ref[...].T * ATTN_SCALE
    row = lax.broadcasted_iota(jnp.int32, (LANES, 1), 0)
    own_rows = _head_rows()
    one_rows = (jnp.logical_and(row >= HEAD_DIM, row < HEAD_DIM + CUM_SPLIT), row < CUM_SPLIT)
    key = lax.broadcasted_iota(jnp.int32, (tq, tq), 0)
    qry = lax.broadcasted_iota(jnp.int32, (tq, tq), 1)
    causal = key <= qry
    rows_own = pl.ds(pl.multiple_of(qi * tq, tq), tq)
    stats = []
    for h in range(2):
        qs = jnp.where(own_rows[h], qt, jnp.where(one_rows[h], 1.0, 0.0)).astype(BF16)
        qts_s[h] = qs
        s = jnp.dot(kaug_s[h, rows_own, :], qs, preferred_element_type=F32)
        s = jnp.where(causal, s, NEG)
        m, l, acc = _flash_first(s, vt_s[h * HEAD_DIM:(h + 1) * HEAD_DIM, rows_own])
        acc_s[h] = acc
        stats += [m, l]

    def body(kj, carry):
        rows = pl.ds(pl.multiple_of(kj * tq, tq), tq)
        out = []
        for h in range(2):
            s = jnp.dot(kaug_s[h, rows, :], qts_s[h], preferred_element_type=F32)
            m, l, acc = _flash_next(s, vt_s[h * HEAD_DIM:(h + 1) * HEAD_DIM, rows],
                                    carry[2 * h], carry[2 * h + 1], acc_s[h])
            acc_s[h] = acc
            out += [m, l]
        return tuple(out)

    _, l0, _, l1 = lax.fori_loop(0, qi, body, tuple(stats))
    o_ref[...] = jnp.concatenate([acc_s[0] / l0, acc_s[1] / l1], axis=0).T


def _fox_attention(qkv, caug, bsz, seq, col0):
    npair = H_C // 2
    tq = 256
    kern = functools.partial(_fox_kernel, tq=tq)

    def col(off):
        return pl.BlockSpec((None, seq, LANES), lambda p, b, i, off=off: (b, 0, off + p))

    return pl.pallas_call(
        kern,
        grid=(npair, bsz, seq // tq),
        in_specs=[pl.BlockSpec((None, tq, LANES), lambda p, b, i: (b, i, col0 + p)),
                  col(col0 + npair), col(col0 + 2 * npair),
                  pl.BlockSpec((None, seq, LANES), lambda p, b, i: (b, 0, p))],
        out_specs=pl.BlockSpec((None, tq, LANES), lambda p, b, i: (b, i, p)),
        out_shape=jax.ShapeDtypeStruct((bsz, seq, W_C), F32),
        scratch_shapes=[pltpu.VMEM((2, seq, LANES), BF16), pltpu.VMEM((LANES, seq), BF16),
                        pltpu.VMEM((2, LANES, tq), BF16), pltpu.VMEM((2, HEAD_DIM, tq), F32)],
        compiler_params=_cparams("parallel", "parallel", "arbitrary"),
    )(qkv, qkv, qkv, caug)


def _layer_norm_rows(z, g, b):
    mu = jnp.mean(z, axis=-1, keepdims=True)
    zc = z - mu
    var = jnp.mean(zc * zc, axis=-1, keepdims=True)
    return zc * lax.rsqrt(var + LN_EPS) * g + b


def _mixout_kernel(oa_ref, ob_ref, oc_ref, ga_ref, gb_ref, gc_ref, wa_ref, wb_ref, wc_ref,
                   x_ref, g1_ref, lng_ref, lnb_ref, sc_ref, sh_ref, x_out, h_out, *, alpha):
    def normed(o_ref, g_ref):
        o = o_ref[...]
        ms = jnp.mean(o * o, axis=-1, keepdims=True)
        return (o * lax.rsqrt(ms + LN_EPS) * g_ref[...]).astype(BF16)

    y = jnp.dot(normed(oa_ref, ga_ref), wa_ref[...], preferred_element_type=F32)
    y = y + jnp.dot(normed(ob_ref, gb_ref), wb_ref[...], preferred_element_type=F32)
    y = y + jnp.dot(normed(oc_ref, gc_ref), wc_ref[...], preferred_element_type=F32)
    x1 = _layer_norm_rows(alpha * x_ref[...] + (1.0 + g1_ref[...]) * y, lng_ref[...], lnb_ref[...])
    x_out[...] = x1
    h_out[...] = (x1 * (1.0 + sc_ref[...]) + sh_ref[...]).astype(BF16)


def _mix_out(o_a, o_b, o_c, g_a, g_b, g_c, w_out_bf, x, g1, ln_g, ln_b, sc2, sh2, alpha):
    bsz, seq, d = x.shape
    tm = 256
    w_a, w_b, w_c = w_out_bf[:W_A], w_out_bf[W_A:W_A + W_B], w_out_bf[W_A + W_B:]

    def act(width):
        return pl.BlockSpec((None, tm, width), lambda b, i: (b, i, 0))

    def const(shape):
        return pl.BlockSpec(shape, lambda b, i: (0,) * len(shape))

    bvec = pl.BlockSpec((None, 1, d), lambda b, i: (b, 0, 0))
    return pl.pallas_call(
        functools.partial(_mixout_kernel, alpha=alpha),
        grid=(bsz, seq // tm),
        in_specs=[act(W_A), act(W_B), act(W_C),
                  const((1, W_A)), const((1, W_B)), const((1, W_C)),
                  const((W_A, d)), const((W_B, d)), const((W_C, d)),
                  act(d), bvec, const((1, d)), const((1, d)), bvec, bvec],
        out_specs=[act(d), act(d)],
        out_shape=[jax.ShapeDtypeStruct((bsz, seq, d), F32),
                   jax.ShapeDtypeStruct((bsz, seq, d), BF16)],
        compiler_params=_cparams("parallel", "parallel"),
    )(o_a, o_b, o_c, g_a.reshape(1, -1), g_b.reshape(1, -1), g_c.reshape(1, -1),
      w_a, w_b, w_c, x, g1, ln_g.reshape(1, d), ln_b.reshape(1, d), sc2, sh2)


FFN_HALO = 8


def _ffn_up_kernel(h_ref, wa_ref, wb_ref, cwa_ref, cwb_ref, cba_ref, cbb_ref, o_ref, bufa, bufb):
    i = pl.program_id(2)
    tm = h_ref.shape[0]
    h = h_ref[...]

    def conv(w_ref, cw_ref, cb_ref, buf):
        @pl.when(i == 0)
        def _():
            buf[0:FFN_HALO, :] = jnp.zeros((FFN_HALO, buf.shape[1]), F32)

        u = jnp.dot(h, w_ref[...], preferred_element_type=F32)
        buf[FFN_HALO:FFN_HALO + tm, :] = u
        out = (cw_ref[2:3, :] * u
               + cw_ref[1:2, :] * buf[FFN_HALO - 1:FFN_HALO - 1 + tm, :]
               + cw_ref[0:1, :] * buf[FFN_HALO - 2:FFN_HALO - 2 + tm, :]
               + cb_ref[...])
        buf[0:FFN_HALO, :] = u[tm - FFN_HALO:tm, :]
        return out

    a = conv(wa_ref, cwa_ref, cba_ref, bufa)
    b = conv(wb_ref, cwb_ref, cbb_ref, bufb)
    o_ref[...] = (a * (1.0 / (1.0 + jnp.exp(-a))) * b).astype(o_ref.dtype)


def _ffn_up(h, w_up_bf, conv_w, conv_b):
    bsz, seq, d = h.shape
    dff = w_up_bf.shape[1] // 2
    tm, tn = 512, 512
    nj = dff // tn
    cb = conv_b.reshape(1, -1)
    return pl.pallas_call(
        _ffn_up_kernel,
        grid=(nj, bsz, seq // tm),
        in_specs=[pl.BlockSpec((None, tm, d), lambda j, b, i: (b, i, 0)),
                  pl.BlockSpec((d, tn), lambda j, b, i: (0, j)),
                  pl.BlockSpec((d, tn), lambda j, b, i: (0, nj + j)),
                  pl.BlockSpec((CONV_WIDTH, tn), lambda j, b, i: (0, j)),
                  pl.BlockSpec((CONV_WIDTH, tn), lambda j, b, i: (0, nj + j)),
                  pl.BlockSpec((1, tn), lambda j, b, i: (0, j)),
                  pl.BlockSpec((1, tn), lambda j, b, i: (0, nj + j))],
        out_specs=pl.BlockSpec((None, tm, tn), lambda j, b, i: (b, i, j)),
        out_shape=jax.ShapeDtypeStruct((bsz, seq, dff), BF16),
        scratch_shapes=[pltpu.VMEM((tm + FFN_HALO, tn), F32)] * 2,
        compiler_params=_cparams("parallel", "parallel", "arbitrary"),
    )(h, w_up_bf, w_up_bf, conv_w, conv_w, cb, cb)


def _ffn_down_kernel(g_ref, w_ref, x_ref, g2_ref, lng_ref, lnb_ref, sc_ref, sh_ref,
                     x_out, h_out, acc_s, *, alpha):
    k = pl.program_id(2)

    @pl.when(k == 0)
    def _():
        acc_s[...] = jnp.zeros_like(acc_s)

    acc_s[...] += jnp.dot(g_ref[...], w_ref[...], preferred_element_type=F32)

    @pl.when(k == pl.num_programs(2) - 1)
    def _():
        x2 = _layer_norm_rows(alpha * x_ref[...] + (1.0 + g2_ref[...]) * acc_s[...],
                              lng_ref[...], lnb_ref[...])
        x_out[...] = x2
        h_out[...] = (x2 * (1.0 + sc_ref[...]) + sh_ref[...]).astype(BF16)


def _ffn_down(g, w_down_bf, x, g2, ln_g, ln_b, sc_next, sh_next, alpha):
    bsz, seq, d = x.shape
    dff = g.shape[2]
    tm, tk = 512, 512
    act = pl.BlockSpec((None, tm, d), lambda b, i, k: (b, i, 0))
    bvec = pl.BlockSpec((None, 1, d), lambda b, i, k: (b, 0, 0))
    const = pl.BlockSpec((1, d), lambda b, i, k: (0, 0))
    return pl.pallas_call(
        functools.partial(_ffn_down_kernel, alpha=alpha),
        grid=(bsz, seq // tm, dff // tk),
        in_specs=[pl.BlockSpec((None, tm, tk), lambda b, i, k: (b, i, k)),
                  pl.BlockSpec((tk, d), lambda b, i, k: (k, 0)),
                  act, bvec, const, const, bvec, bvec],
        out_specs=[act, act],
        out_shape=[jax.ShapeDtypeStruct((bsz, seq, d), F32),
                   jax.ShapeDtypeStruct((bsz, seq, d), BF16)],
        scratch_shapes=[pltpu.VMEM((tm, d), F32)],
        compiler_params=_cparams("parallel", "parallel", "arbitrary"),
    )(g, w_down_bf, x, g2, ln_g.reshape(1, d), ln_b.reshape(1, d), sc_next, sh_next)


def kernel(x, c, rel_bias, w_ada, b_ada, w_in, b_f, g_mix_a, g_mix_b, g_mix_c, w_out,
           ln1_g, ln1_b, w_up, conv_w, conv_b, w_down, ln2_g, ln2_b):
    bsz, seq, d = x.shape
    depth = w_ada.shape[0]
    alpha = (2 * depth) ** 0.25
    n_qkv = 3 * (W_A + W_B + W_C)

    mod = _ada_mod(c, w_ada, b_ada)

    def mod_vec(l, idx):
        return mod[l, :, idx * d:(idx + 1) * d].reshape(bsz, 1, d)

    bm_a = _dilated_bias_tiles(rel_bias[:, :H_A])
    bt_b = _moba_bias_tiles(rel_bias[:, H_A:], seq)

    h = _modulate(x, mod_vec(0, 1), mod_vec(0, 0))
    for l in range(depth):
        w_in_bf = w_in[l].astype(BF16)
        w_f = jnp.pad(w_in_bf[:, n_qkv:], ((0, 0), (0, LANES - H_C)))
        qkv, f = _inproj(h.reshape(bsz * seq, d), w_in_bf[:, :n_qkv], w_f)
        qkv = qkv.reshape(bsz, seq, n_qkv)
        caug = _forget_keys(f.reshape(bsz, seq, LANES), b_f[l])

        o_a = _dilated_attention(qkv, bm_a, bsz, seq)
        o_b = _moba_attention(qkv, bt_b, bsz, seq, 3 * W_A // LANES)
        o_c = _fox_attention(qkv, caug, bsz, seq, 3 * (W_A + W_B) // LANES)

        x, h = _mix_out(o_a, o_b, o_c, g_mix_a[l], g_mix_b[l], g_mix_c[l], w_out[l].astype(BF16),
                        x, mod_vec(l, 2), ln1_g[l], ln1_b[l], mod_vec(l, 4), mod_vec(l, 3), alpha)
        g = _ffn_up(h, w_up[l].astype(BF16), conv_w[l], conv_b[l])
        nxt = min(l + 1, depth - 1)
        x, h = _ffn_down(g, w_down[l].astype(BF16), x, mod_vec(l, 5), ln2_g[l], ln2_b[l],
                         mod_vec(nxt, 1), mod_vec(nxt, 0), alpha)
    return x
```

```python
import functools
import math

import numpy as np
import jax
import jax.numpy as jnp
from jax import lax
from jax.experimental import pallas as pl
from jax.experimental.pallas import tpu as pltpu

F32 = jnp.float32
BF16 = jnp.bfloat16

HEAD_DIM = 64
LANES = 128
H_A, H_B, H_C = 12, 10, 10
W_A, W_B, W_C = H_A * HEAD_DIM, H_B * HEAD_DIM, H_C * HEAD_DIM
DIL_PATTERNS = ((128, 1), (512, 4), (2048, 16))
DIL_W = 128
MOBA_BLOCK = 256
MOBA_TOPK = 3
N_BUCKETS = 32
REL_MAX_DIST = 2048
CONV_WIDTH = 3
LN_EPS = 1e-5
NEG = -1e30
ATTN_SCALE = HEAD_DIM ** -0.5
LOG2E = math.log2(math.e)
VMEM_LIMIT = 56 * 1024 * 1024


def _cparams(*sem):
    return pltpu.CompilerParams(dimension_semantics=sem, vmem_limit_bytes=VMEM_LIMIT)


def _t5_bucket(dist):
    n = jnp.maximum(dist, 0)
    max_exact = N_BUCKETS // 2
    nf = jnp.maximum(n, 1).astype(F32)
    large = max_exact + (jnp.log(nf / max_exact) / math.log(REL_MAX_DIST / max_exact)
                         * (N_BUCKETS - max_exact)).astype(jnp.int32)
    large = jnp.minimum(large, N_BUCKETS - 1)
    return jnp.where(n < max_exact, n, large)


def _ada_kernel(c_ref, w_ref, b_ref, o_ref):
    c = c_ref[...]
    cond = c * (1.0 / (1.0 + jnp.exp(-c)))
    o_ref[...] = jnp.dot(cond, w_ref[...], precision=lax.Precision.HIGHEST,
                         preferred_element_type=F32) + b_ref[...]


def _ada_mod(c, w_ada, b_ada):
    depth, d, n = w_ada.shape
    bsz = c.shape[0]
    tn = 1024
    return pl.pallas_call(
        _ada_kernel,
        grid=(depth, n // tn),
        in_specs=[pl.BlockSpec((bsz, d), lambda l, j: (0, 0)),
                  pl.BlockSpec((None, d, tn), lambda l, j: (l, 0, j)),
                  pl.BlockSpec((None, 1, tn), lambda l, j: (l, 0, j))],
        out_specs=pl.BlockSpec((None, bsz, tn), lambda l, j: (l, 0, j)),
        out_shape=jax.ShapeDtypeStruct((depth, bsz, n), F32),
        compiler_params=_cparams("parallel", "parallel"),
    )(c, w_ada, b_ada.reshape(depth, 1, n))


def _modulate_kernel(x_ref, sc_ref, sh_ref, o_ref):
    o_ref[...] = (x_ref[...] * (1.0 + sc_ref[...]) + sh_ref[...]).astype(o_ref.dtype)


def _modulate(x, sc, sh):
    bsz, s, d = x.shape
    tm = 512
    vec = pl.BlockSpec((None, 1, d), lambda b, i: (b, 0, 0))
    return pl.pallas_call(
        _modulate_kernel,
        grid=(bsz, s // tm),
        in_specs=[pl.BlockSpec((None, tm, d), lambda b, i: (b, i, 0)), vec, vec],
        out_specs=pl.BlockSpec((None, tm, d), lambda b, i: (b, i, 0)),
        out_shape=jax.ShapeDtypeStruct((bsz, s, d), BF16),
        compiler_params=_cparams("parallel", "parallel"),
    )(x, sc, sh)


def _inproj_kernel(a_ref, w_ref, wf_ref, o_ref, f_ref):
    a = a_ref[...]
    o_ref[...] = jnp.dot(a, w_ref[...], preferred_element_type=F32)

    @pl.when(pl.program_id(1) == 0)
    def _():
        f_ref[...] = jnp.dot(a, wf_ref[...], preferred_element_type=F32)


def _inproj(h, w_main, w_f):
    t, d = h.shape
    n = w_main.shape[1]
    tm, tn = 512, 1024
    return pl.pallas_call(
        _inproj_kernel,
        grid=(t // tm, n // tn),
        in_specs=[pl.BlockSpec((tm, d), lambda i, j: (i, 0)),
                  pl.BlockSpec((d, tn), lambda i, j: (0, j)),
                  pl.BlockSpec((d, LANES), lambda i, j: (0, 0))],
        out_specs=[pl.BlockSpec((tm, tn), lambda i, j: (i, j)),
                   pl.BlockSpec((tm, LANES), lambda i, j: (i, 0))],
        out_shape=[jax.ShapeDtypeStruct((t, n), F32),
                   jax.ShapeDtypeStruct((t, LANES), F32)],
        compiler_params=_cparams("parallel", "arbitrary"),
    )(h, w_main, w_f)


TOEP_LEN = 512
TOEP_ROWS = 256


def _toeplitz_kernel(rr_ref, o_ref):
    x = jnp.broadcast_to(rr_ref[...], (TOEP_ROWS, TOEP_LEN))
    y = pltpu.roll(x, TOEP_LEN - TOEP_ROWS + 1, 1, stride=1, stride_axis=0)
    o_ref[...] = y[:, :o_ref.shape[1]]


def _toeplitz(rr, cols):
    g = rr.shape[0]
    return pl.pallas_call(
        _toeplitz_kernel,
        grid=(g,),
        in_specs=[pl.BlockSpec((None, 1, TOEP_LEN), lambda i: (i, 0, 0))],
        out_specs=pl.BlockSpec((None, TOEP_ROWS, cols), lambda i: (i, 0, 0)),
        out_shape=jax.ShapeDtypeStruct((g, TOEP_ROWS, cols), F32),
        compiler_params=_cparams("parallel"),
    )(rr.reshape(g, 1, TOEP_LEN))


def _bias_of_distance(rel, dist, valid):
    bias = jnp.moveaxis(rel[_t5_bucket(dist)], -1, 0)
    return jnp.where(valid, bias * LOG2E, NEG).astype(F32)


def _dilated_bias_tiles(rel_a):
    c = jnp.arange(TOEP_LEN)
    rows = []
    for (_, dil) in DIL_PATTERNS:
        d0 = c - (TOEP_ROWS - 1) + DIL_W
        d1 = c - (TOEP_ROWS - 1)
        rows.append(jnp.stack([_bias_of_distance(rel_a, d0 * dil, (d0 >= 0) & (d0 <= DIL_W)),
                               _bias_of_distance(rel_a, d1 * dil, d1 >= 0)]))
    rr = jnp.stack(rows)
    out = _toeplitz(rr.reshape(-1, TOEP_LEN), DIL_W)
    return out.reshape(len(DIL_PATTERNS), 2, H_A, TOEP_ROWS, DIL_W)


def _moba_num_tiles(seq):
    blk = MOBA_BLOCK
    d_all = np.arange(max(seq, 2))
    nf = np.maximum(d_all, 1).astype(np.float32)
    half = N_BUCKETS // 2
    big = half + (np.log(nf / half) / math.log(REL_MAX_DIST / half) * (N_BUCKETS - half)).astype(np.int32)
    bucket = np.where(d_all < half, d_all, np.minimum(big, N_BUCKETS - 1))
    last_change = int(np.max(np.nonzero(np.diff(bucket))[0])) + 1
    first_const = min(seq // blk - 1, -(-(last_change + 2 + blk - 1) // blk))
    return first_const + 1


def _moba_bias_tiles(rel_b, seq):
    ntile = _moba_num_tiles(seq)
    c = jnp.arange(TOEP_LEN)[None, :]
    d = jnp.arange(ntile)[:, None] * MOBA_BLOCK + c - (TOEP_ROWS - 1)
    rr = _bias_of_distance(rel_b, d, d >= 0)
    out = _toeplitz(rr.reshape(-1, TOEP_LEN), MOBA_BLOCK)
    return out.reshape(H_B, ntile, TOEP_ROWS, MOBA_BLOCK)


def _head_rows():
    row = lax.broadcasted_iota(jnp.int32, (LANES, 1), 0)
    return row < HEAD_DIM, row >= HEAD_DIM


DIL_UNROLL = 4


def _dilated_kernel(q_ref, k_ref, v_ref, bm_ref, o_ref, og_s, lse_s, *, seq):
    w = DIL_W
    own_rows = _head_rows()
    for g, (_, dil) in enumerate(DIL_PATTERNS):
        nb = (seq // dil) // w

        def rows(start, size, dil=dil):
            return pl.ds(start, size) if dil == 1 else pl.ds(start, size, stride=dil)

        def trip(it, carry, g=g, dil=dil, nb=nb, rows=rows):
            blocks = []
            for u in range(DIL_UNROLL):
                idx = it * DIL_UNROLL + u
                r = idx // nb
                n = idx % nb
                q_rows = rows(n * w * dil + r, w)
                kv_rows = rows(jnp.maximum(n - 1, 0) * w * dil + r, 2 * w)
                var = jnp.where(n == 0, 1, 0)
                qt = q_ref[q_rows, :].T * (ATTN_SCALE * LOG2E)
                k = k_ref[kv_rows, :].astype(BF16)
                vt = v_ref[kv_rows, :].T.astype(BF16)
                scores = [jnp.dot(k, jnp.where(own_rows[h], qt, 0.0).astype(BF16),
                                  preferred_element_type=F32) for h in range(2)]
                blocks.append((q_rows, var, vt, scores))
            results = []
            for (q_rows, var, vt, scores) in blocks:
                per_head = []
                for h in range(2):
                    s = scores[h] + bm_ref[g, var, h]
                    m = jnp.max(s, axis=0, keepdims=True)
                    p = jnp.exp2(s - m)
                    l = jnp.sum(p, axis=0, keepdims=True)
                    pv = jnp.dot(vt[h * HEAD_DIM:(h + 1) * HEAD_DIM, :], p.astype(BF16),
                                 preferred_element_type=F32)
                    per_head.append((pv, m, l))
                results.append((q_rows, per_head))
            for (q_rows, per_head) in results:
                o_t = jnp.concatenate([pv / l for (pv, _, l) in per_head], axis=0)
                lse_t = jnp.concatenate([jnp.broadcast_to(m + jnp.log(l) * LOG2E, (HEAD_DIM, w))
                                         for (_, m, l) in per_head], axis=0)
                og_s[g, q_rows, :] = o_t.T
                lse_s[g, q_rows, :] = lse_t.T
            return carry

        lax.fori_loop(0, nb * dil // DIL_UNROLL, trip, 0)

    top = jnp.maximum(jnp.maximum(lse_s[0], lse_s[1]), lse_s[2])
    num = jnp.zeros_like(top)
    den = jnp.zeros_like(top)
    for g in range(len(DIL_PATTERNS)):
        wg = jnp.exp2(lse_s[g] - top)
        num = num + wg * og_s[g]
        den = den + wg
    o_ref[...] = num / den


def _dilated_attention(qkv, bm_a, bsz, seq):
    npair = H_A // 2
    npat = len(DIL_PATTERNS)
    kern = functools.partial(_dilated_kernel, seq=seq)

    def col(off):
        return pl.BlockSpec((None, seq, LANES), lambda p, b, off=off: (b, 0, off + p))

    return pl.pallas_call(
        kern,
        grid=(npair, bsz),
        in_specs=[col(0), col(npair), col(2 * npair),
                  pl.BlockSpec((npat, 2, 2, 2 * DIL_W, DIL_W), lambda p, b: (0, 0, p, 0, 0))],
        out_specs=pl.BlockSpec((None, seq, LANES), lambda p, b: (b, 0, p)),
        out_shape=jax.ShapeDtypeStruct((bsz, seq, W_A), F32),
        scratch_shapes=[pltpu.VMEM((npat, seq, LANES), F32)] * 2,
        compiler_params=_cparams("parallel", "parallel"),
    )(qkv, qkv, qkv, bm_a)


def _transpose_to(dst_ref, src_ref, chunk=512):
    for c in range(src_ref.shape[0] // chunk):
        dst_ref[:, c * chunk:(c + 1) * chunk] = src_ref[c * chunk:(c + 1) * chunk, :].T.astype(BF16)


def _causal_flash(qi, tq, kaug_s, vt_s, qa_s, sbuf, pbuf, acc_s, diag_fn, tile_fn):
    def rows(kj):
        return pl.ds(pl.multiple_of(kj * tq, tq), tq)

    def head(h):
        return slice(h * HEAD_DIM, (h + 1) * HEAD_DIM)

    def qk(kj, h):
        return jnp.dot(kaug_s[h, rows(kj), :], qa_s[h], preferred_element_type=F32)

    def pv(kj, par, h):
        return jnp.dot(vt_s[head(h), rows(kj)], pbuf[par, h], preferred_element_type=F32)

    s_diag = [qk(qi, h) for h in range(2)]
    for h in range(2):
        sbuf[0, h] = qk(0, h)
    carry = []
    for h in range(2):
        s = diag_fn(h, s_diag[h])
        m = jnp.max(s, axis=0, keepdims=True)
        p = jnp.exp2(s - m)
        pbuf[0, h] = p.astype(BF16)
        acc_s[h] = jnp.zeros(acc_s.shape[1:], F32)
        carry += [m, jnp.sum(p, axis=0, keepdims=True), jnp.zeros_like(m)]

    def body(t, carry):
        nxt = jnp.minimum(t + 1, qi - 1)
        for h in range(2):
            sbuf[(t + 1) % 2, h] = qk(nxt, h)
        prev = jnp.where(t == 0, qi, t - 1)
        pvs = [pv(prev, t % 2, h) for h in range(2)]
        out = []
        for h in range(2):
            m, l, alpha_prev = carry[3 * h:3 * h + 3]
            s = tile_fn(h, t, sbuf[t % 2, h])
            m_new = jnp.maximum(m, jnp.max(s, axis=0, keepdims=True))
            alpha = jnp.exp2(m - m_new)
            p = jnp.exp2(s - m_new)
            pbuf[(t + 1) % 2, h] = p.astype(BF16)
            l_new = alpha * l + jnp.sum(p, axis=0, keepdims=True)
            acc_s[h] = alpha_prev * acc_s[h] + pvs[h]
            out += [m_new, l_new, alpha]
        return tuple(out)

    carry = lax.fori_loop(0, qi, body, tuple(carry))
    last = jnp.maximum(qi - 1, 0)
    outs = []
    for h in range(2):
        _, l, alpha = carry[3 * h:3 * h + 3]
        outs.append((alpha * acc_s[h] + pv(last, qi % 2, h)) / l)
    return jnp.concatenate(outs, axis=0).T


def _moba_kernel(q_ref, k_ref, v_ref, bt_ref, o_ref,
                 kaug_s, vt_s, kmean_s, qa_s, sbuf, pbuf, acc_s, *, nblk, ntile):
    blk = MOBA_BLOCK
    qi = pl.program_id(2)
    spare = LANES - HEAD_DIM - nblk

    @pl.when(qi == 0)
    def _():
        lane = lax.broadcasted_iota(jnp.int32, (1, LANES), 1)
        for n in range(nblk):
            k = k_ref[n * blk:(n + 1) * blk, :]
            kaug_s[0, n * blk:(n + 1) * blk, :] = jnp.where(
                lane < HEAD_DIM, k, (lane == HEAD_DIM + n).astype(F32)).astype(BF16)
            kaug_s[1, n * blk:(n + 1) * blk, :] = jnp.where(
                lane >= HEAD_DIM, k, (lane == n).astype(F32)).astype(BF16)
            kmean_s[n:n + 1, :] = jnp.mean(k, axis=0, keepdims=True)
        _transpose_to(vt_s, v_ref)

    qt = q_ref[...].T
    own_rows = _head_rows()
    blk_id = lax.broadcasted_iota(jnp.int32, (nblk, 1), 0)
    blk_idf = blk_id.astype(F32)
    past = blk_id < qi
    for h in range(2):
        qh = jnp.where(own_rows[h], qt, 0.0)
        gate = jnp.dot(kmean_s[...], qh, precision=lax.Precision.HIGHEST,
                       preferred_element_type=F32)
        gate = jnp.where(past, gate, NEG)
        chosen = jnp.zeros(gate.shape, jnp.bool_)
        for _ in range(MOBA_TOPK):
            mx = jnp.max(gate, axis=0, keepdims=True)
            first_idx = jnp.min(jnp.where(gate == mx, blk_idf, float(nblk)), axis=0, keepdims=True)
            pick = blk_idf == first_idx
            chosen = jnp.logical_or(chosen, pick)
            gate = jnp.where(pick, -jnp.inf, gate)
        keep = jnp.logical_or(jnp.logical_and(chosen, past), blk_id == qi)
        selb = jnp.where(keep, 0.0, NEG)
        qs = qt[h * HEAD_DIM:(h + 1) * HEAD_DIM, :] * (ATTN_SCALE * LOG2E)
        pad = jnp.zeros((spare, blk), F32)
        parts = [qs, selb, pad] if h == 0 else [selb, pad, qs]
        qa_s[h] = jnp.concatenate(parts, axis=0).astype(BF16)

    o_ref[...] = _causal_flash(
        qi, blk, kaug_s, vt_s, qa_s, sbuf, pbuf, acc_s,
        diag_fn=lambda h, s: s + bt_ref[h, 0],
        tile_fn=lambda h, t, s: s + bt_ref[h, jnp.minimum(qi - t, ntile - 1)])


def _moba_attention(qkv, bt_b, bsz, seq, col0):
    npair = H_B // 2
    nblk = seq // MOBA_BLOCK
    ntile = bt_b.shape[1]
    assert nblk % 8 == 0 and nblk <= LANES - HEAD_DIM
    kern = functools.partial(_moba_kernel, nblk=nblk, ntile=ntile)
    blk = MOBA_BLOCK

    def col(off):
        return pl.BlockSpec((None, seq, LANES), lambda p, b, i, off=off: (b, 0, off + p))

    return pl.pallas_call(
        kern,
        grid=(npair, bsz, nblk),
        in_specs=[pl.BlockSpec((None, blk, LANES), lambda p, b, i: (b, i, col0 + p)),
                  col(col0 + npair), col(col0 + 2 * npair),
                  pl.BlockSpec((2, ntile, blk, blk), lambda p, b, i: (p, 0, 0, 0))],
        out_specs=pl.BlockSpec((None, blk, LANES), lambda p, b, i: (b, i, p)),
        out_shape=jax.ShapeDtypeStruct((bsz, seq, W_B), F32),
        scratch_shapes=[pltpu.VMEM((2, seq, LANES), BF16), pltpu.VMEM((LANES, seq), BF16),
                        pltpu.VMEM((nblk, LANES), F32), pltpu.VMEM((2, LANES, blk), BF16),
                        pltpu.VMEM((2, 2, blk, blk), F32), pltpu.VMEM((2, 2, blk, blk), BF16),
                        pltpu.VMEM((2, HEAD_DIM, blk), F32)],
        compiler_params=_cparams("parallel", "parallel", "arbitrary"),
    )(qkv, qkv, qkv, bt_b)


CUM_SPLIT = 3
CUM_CHUNK = 256


def _fgate_kernel(f_ref, bf_ref, pm_ref, o_ref, cum_s):
    seq = f_ref.shape[0]
    ch = CUM_CHUNK
    i = lax.broadcasted_iota(jnp.int32, (ch, ch), 0)
    j = lax.broadcasted_iota(jnp.int32, (ch, ch), 1)
    lower = (j <= i).astype(F32)
    carry = jnp.zeros((1, LANES), F32)
    for c in range(seq // ch):
        z = f_ref[c * ch:(c + 1) * ch, :] + bf_ref[...]
        ls = jnp.minimum(z, 0.0) - jnp.log(1.0 + jnp.exp(-jnp.abs(z)))
        cum = jnp.dot(lower, ls, precision=lax.Precision.HIGHEST, preferred_element_type=F32) + carry
        cum_s[c * ch:(c + 1) * ch, :] = cum
        carry = cum[ch - 1:ch, :]
    rest = cum_s[...] * (-LOG2E)
    out = None
    for t in range(CUM_SPLIT):
        term = rest.astype(BF16)
        rest = rest - term.astype(F32)
        part = jnp.dot(term, pm_ref[t], preferred_element_type=F32)
        out = part if out is None else out + part
    o_ref[...] = out.astype(BF16)


def _forget_keys(f, b_f):
    bsz, seq, _ = f.shape
    npair = H_C // 2
    pm = np.zeros((CUM_SPLIT, LANES, npair * LANES), np.float32)
    for t in range(CUM_SPLIT):
        for p in range(npair):
            pm[t, 2 * p, p * LANES + HEAD_DIM + t] = 1.0
            pm[t, 2 * p + 1, p * LANES + t] = 1.0
    bf = jnp.pad(b_f, (0, LANES - b_f.shape[0])).reshape(1, LANES)
    return pl.pallas_call(
        _fgate_kernel,
        grid=(bsz,),
        in_specs=[pl.BlockSpec((None, seq, LANES), lambda b: (b, 0, 0)),
                  pl.BlockSpec((1, LANES), lambda b: (0, 0)),
                  pl.BlockSpec((CUM_SPLIT, LANES, npair * LANES), lambda b: (0, 0, 0))],
        out_specs=pl.BlockSpec((None, seq, npair * LANES), lambda b: (b, 0, 0)),
        out_shape=jax.ShapeDtypeStruct((bsz, seq, npair * LANES), BF16),
        scratch_shapes=[pltpu.VMEM((seq, LANES), F32)],
        compiler_params=_cparams("parallel"),
    )(f, bf, jnp.asarray(pm, BF16))


def _fox_kernel(q_ref, k_ref, v_ref, ca_ref, o_ref, kaug_s, vt_s, qa_s, sbuf, pbuf, acc_s, *, tq):
    qi = pl.program_id(2)

    @pl.when(qi == 0)
    def _():
        lane = lax.broadcasted_iota(jnp.int32, (1, LANES), 1)
        k = k_ref[...]
        ca = ca_ref[...].astype(F32)
        kaug_s[0] = jnp.where(lane < HEAD_DIM, k, ca).astype(BF16)
        kaug_s[1] = jnp.where(lane >= HEAD_DIM, k, ca).astype(BF16)
        _transpose_to(vt_s, v_ref)

    qt = q_ref[...].T * (ATTN_SCALE * LOG2E)
    row = lax.broadcasted_iota(jnp.int32, (LANES, 1), 0)
    own_rows = _head_rows()
    one_rows = (jnp.logical_and(row >= HEAD_DIM, row < HEAD_DIM + CUM_SPLIT), row < CUM_SPLIT)
    for h in range(2):
        qa_s[h] = jnp.where(own_rows[h], qt, jnp.where(one_rows[h], 1.0, 0.0)).astype(BF16)
    key = lax.broadcasted_iota(jnp.int32, (tq, tq), 0)
    qry = lax.broadcasted_iota(jnp.int32, (tq, tq), 1)
    causal = key <= qry
    o_ref[...] = _causal_flash(
        qi, tq, kaug_s, vt_s, qa_s, sbuf, pbuf, acc_s,
        diag_fn=lambda h, s: jnp.where(causal, s, NEG),
        tile_fn=lambda h, t, s: s)


def _fox_attention(qkv, caug, bsz, seq, col0):
    npair = H_C // 2
    tq = 256
    kern = functools.partial(_fox_kernel, tq=tq)

    def col(off):
        return pl.BlockSpec((None, seq, LANES), lambda p, b, i, off=off: (b, 0, off + p))

    return pl.pallas_call(
        kern,
        grid=(npair, bsz, seq // tq),
        in_specs=[pl.BlockSpec((None, tq, LANES), lambda p, b, i: (b, i, col0 + p)),
                  col(col0 + npair), col(col0 + 2 * npair),
                  pl.BlockSpec((None, seq, LANES), lambda p, b, i: (b, 0, p))],
        out_specs=pl.BlockSpec((None, tq, LANES), lambda p, b, i: (b, i, p)),
        out_shape=jax.ShapeDtypeStruct((bsz, seq, W_C), F32),
        scratch_shapes=[pltpu.VMEM((2, seq, LANES), BF16), pltpu.VMEM((LANES, seq), BF16),
                        pltpu.VMEM((2, LANES, tq), BF16),
                        pltpu.VMEM((2, 2, tq, tq), F32), pltpu.VMEM((2, 2, tq, tq), BF16),
                        pltpu.VMEM((2, HEAD_DIM, tq), F32)],
        compiler_params=_cparams("parallel", "parallel", "arbitrary"),
    )(qkv, qkv, qkv, caug)


def _layer_norm_rows(z, g, b):
    mu = jnp.mean(z, axis=-1, keepdims=True)
    zc = z - mu
    var = jnp.mean(zc * zc, axis=-1, keepdims=True)
    return zc * lax.rsqrt(var + LN_EPS) * g + b


def _mixout_kernel(oa_ref, ob_ref, oc_ref, ga_ref, gb_ref, gc_ref, wa_ref, wb_ref, wc_ref,
                   x_ref, g1_ref, lng_ref, lnb_ref, sc_ref, sh_ref, x_out, h_out, *, alpha):
    def normed(o_ref, g_ref):
        o = o_ref[...]
        ms = jnp.mean(o * o, axis=-1, keepdims=True)
        return (o * lax.rsqrt(ms + LN_EPS) * g_ref[...]).astype(BF16)

    y = jnp.dot(normed(oa_ref, ga_ref), wa_ref[...], preferred_element_type=F32)
    y = y + jnp.dot(normed(ob_ref, gb_ref), wb_ref[...], preferred_element_type=F32)
    y = y + jnp.dot(normed(oc_ref, gc_ref), wc_ref[...], preferred_element_type=F32)
    x1 = _layer_norm_rows(alpha * x_ref[...] + (1.0 + g1_ref[...]) * y, lng_ref[...], lnb_ref[...])
    x_out[...] = x1
    h_out[...] = (x1 * (1.0 + sc_ref[...]) + sh_ref[...]).astype(BF16)


def _mix_out(o_a, o_b, o_c, g_a, g_b, g_c, w_out_bf, x, g1, ln_g, ln_b, sc2, sh2, alpha):
    bsz, seq, d = x.shape
    tm = 256
    w_a, w_b, w_c = w_out_bf[:W_A], w_out_bf[W_A:W_A + W_B], w_out_bf[W_A + W_B:]

    def act(width):
        return pl.BlockSpec((None, tm, width), lambda b, i: (b, i, 0))

    def const(shape):
        return pl.BlockSpec(shape, lambda b, i: (0,) * len(shape))

    bvec = pl.BlockSpec((None, 1, d), lambda b, i: (b, 0, 0))
    return pl.pallas_call(
        functools.partial(_mixout_kernel, alpha=alpha),
        grid=(bsz, seq // tm),
        in_specs=[act(W_A), act(W_B), act(W_C),
                  const((1, W_A)), const((1, W_B)), const((1, W_C)),
                  const((W_A, d)), const((W_B, d)), const((W_C, d)),
                  act(d), bvec, const((1, d)), const((1, d)), bvec, bvec],
        out_specs=[act(d), act(d)],
        out_shape=[jax.ShapeDtypeStruct((bsz, seq, d), F32),
                   jax.ShapeDtypeStruct((bsz, seq, d), BF16)],
        compiler_params=_cparams("parallel", "parallel"),
    )(o_a, o_b, o_c, g_a.reshape(1, -1), g_b.reshape(1, -1), g_c.reshape(1, -1),
      w_a, w_b, w_c, x, g1, ln_g.reshape(1, d), ln_b.reshape(1, d), sc2, sh2)


FFN_HALO = 8


def _ffn_up_kernel(h_ref, wa_ref, wb_ref, cwa_ref, cwb_ref, cba_ref, cbb_ref, o_ref, bufa, bufb):
    i = pl.program_id(2)
    tm = h_ref.shape[0]
    h = h_ref[...]

    def conv(w_ref, cw_ref, cb_ref, buf):
        @pl.when(i == 0)
        def _():
            buf[0:FFN_HALO, :] = jnp.zeros((FFN_HALO, buf.shape[1]), F32)

        u = jnp.dot(h, w_ref[...], preferred_element_type=F32)
        buf[FFN_HALO:FFN_HALO + tm, :] = u
        out = (cw_ref[2:3, :] * u
               + cw_ref[1:2, :] * buf[FFN_HALO - 1:FFN_HALO - 1 + tm, :]
               + cw_ref[0:1, :] * buf[FFN_HALO - 2:FFN_HALO - 2 + tm, :]
               + cb_ref[...])
        buf[0:FFN_HALO, :] = u[tm - FFN_HALO:tm, :]
        return out

    a = conv(wa_ref, cwa_ref, cba_ref, bufa)
    b = conv(wb_ref, cwb_ref, cbb_ref, bufb)
    o_ref[...] = (a * (1.0 / (1.0 + jnp.exp(-a))) * b).astype(o_ref.dtype)


def _ffn_up(h, w_up_bf, conv_w, conv_b):
    bsz, seq, d = h.shape
    dff = w_up_bf.shape[1] // 2
    tm, tn = 512, 512
    nj = dff // tn
    cb = conv_b.reshape(1, -1)
    return pl.pallas_call(
        _ffn_up_kernel,
        grid=(nj, bsz, seq // tm),
        in_specs=[pl.BlockSpec((None, tm, d), lambda j, b, i: (b, i, 0)),
                  pl.BlockSpec((d, tn), lambda j, b, i: (0, j)),
                  pl.BlockSpec((d, tn), lambda j, b, i: (0, nj + j)),
                  pl.BlockSpec((CONV_WIDTH, tn), lambda j, b, i: (0, j)),
                  pl.BlockSpec((CONV_WIDTH, tn), lambda j, b, i: (0, nj + j)),
                  pl.BlockSpec((1, tn), lambda j, b, i: (0, j)),
                  pl.BlockSpec((1, tn), lambda j, b, i: (0, nj + j))],
        out_specs=pl.BlockSpec((None, tm, tn), lambda j, b, i: (b, i, j)),
        out_shape=jax.ShapeDtypeStruct((bsz, seq, dff), BF16),
        scratch_shapes=[pltpu.VMEM((tm + FFN_HALO, tn), F32)] * 2,
        compiler_params=_cparams("parallel", "parallel", "arbitrary"),
    )(h, w_up_bf, w_up_bf, conv_w, conv_w, cb, cb)


def _ffn_down_kernel(g_ref, w_ref, x_ref, g2_ref, lng_ref, lnb_ref, sc_ref, sh_ref,
                     x_out, h_out, acc_s, *, alpha):
    k = pl.program_id(2)

    @pl.when(k == 0)
    def _():
        acc_s[...] = jnp.zeros_like(acc_s)

    acc_s[...] += jnp.dot(g_ref[...], w_ref[...], preferred_element_type=F32)

    @pl.when(k == pl.num_programs(2) - 1)
    def _():
        x2 = _layer_norm_rows(alpha * x_ref[...] + (1.0 + g2_ref[...]) * acc_s[...],
                              lng_ref[...], lnb_ref[...])
        x_out[...] = x2
        h_out[...] = (x2 * (1.0 + sc_ref[...]) + sh_ref[...]).astype(BF16)


def _ffn_down(g, w_down_bf, x, g2, ln_g, ln_b, sc_next, sh_next, alpha):
    bsz, seq, d = x.shape
    dff = g.shape[2]
    tm, tk = 512, 512
    act = pl.BlockSpec((None, tm, d), lambda b, i, k: (b, i, 0))
    bvec = pl.BlockSpec((None, 1, d), lambda b, i, k: (b, 0, 0))
    const = pl.BlockSpec((1, d), lambda b, i, k: (0, 0))
    return pl.pallas_call(
        functools.partial(_ffn_down_kernel, alpha=alpha),
        grid=(bsz, seq // tm, dff // tk),
        in_specs=[pl.BlockSpec((None, tm, tk), lambda b, i, k: (b, i, k)),
                  pl.BlockSpec((tk, d), lambda b, i, k: (k, 0)),
                  act, bvec, const, const, bvec, bvec],
        out_specs=[act, act],
        out_shape=[jax.ShapeDtypeStruct((bsz, seq, d), F32),
                   jax.ShapeDtypeStruct((bsz, seq, d), BF16)],
        scratch_shapes=[pltpu.VMEM((tm, d), F32)],
        compiler_params=_cparams("parallel", "parallel", "arbitrary"),
    )(g, w_down_bf, x, g2, ln_g.reshape(1, d), ln_b.reshape(1, d), sc_next, sh_next)


def kernel(x, c, rel_bias, w_ada, b_ada, w_in, b_f, g_mix_a, g_mix_b, g_mix_c, w_out,
           ln1_g, ln1_b, w_up, conv_w, conv_b, w_down, ln2_g, ln2_b):
    bsz, seq, d = x.shape
    depth = w_ada.shape[0]
    alpha = (2 * depth) ** 0.25
    n_qkv = 3 * (W_A + W_B + W_C)

    mod = _ada_mod(c, w_ada, b_ada)

    def mod_vec(l, idx):
        return mod[l, :, idx * d:(idx + 1) * d].reshape(bsz, 1, d)

    bm_a = _dilated_bias_tiles(rel_bias[:, :H_A])
    bt_b = _moba_bias_tiles(rel_bias[:, H_A:], seq)

    h = _modulate(x, mod_vec(0, 1), mod_vec(0, 0))
    for l in range(depth):
        w_in_bf = w_in[l].astype(BF16)
        w_f = jnp.pad(w_in_bf[:, n_qkv:], ((0, 0), (0, LANES - H_C)))
        qkv, f = _inproj(h.reshape(bsz * seq, d), w_in_bf[:, :n_qkv], w_f)
        qkv = qkv.reshape(bsz, seq, n_qkv)
        caug = _forget_keys(f.reshape(bsz, seq, LANES), b_f[l])

        o_a = _dilated_attention(qkv, bm_a, bsz, seq)
        o_b = _moba_attention(qkv, bt_b, bsz, seq, 3 * W_A // LANES)
        o_c = _fox_attention(qkv, caug, bsz, seq, 3 * (W_A + W_B) // LANES)

        x, h = _mix_out(o_a, o_b, o_c, g_mix_a[l], g_mix_b[l], g_mix_c[l], w_out[l].astype(BF16),
                        x, mod_vec(l, 2), ln1_g[l], ln1_b[l], mod_vec(l, 4), mod_vec(l, 3), alpha)
        g = _ffn_up(h, w_up[l].astype(BF16), conv_w[l], conv_b[l])
        nxt = min(l + 1, depth - 1)
        x, h = _ffn_down(g, w_down[l].astype(BF16), x, mod_vec(l, 5), ln2_g[l], ln2_b[l],
                         mod_vec(nxt, 1), mod_vec(nxt, 0), alpha)
    return x
```

```python
import functools
import math

import numpy as np
import jax
import jax.numpy as jnp
from jax import lax
from jax.experimental import pallas as pl
from jax.experimental.pallas import tpu as pltpu

F32 = jnp.float32
BF16 = jnp.bfloat16

HEAD_DIM = 64
LANES = 128
H_A, H_B, H_C = 12, 10, 10
W_A, W_B, W_C = H_A * HEAD_DIM, H_B * HEAD_DIM, H_C * HEAD_DIM
DIL_PATTERNS = ((128, 1), (512, 4), (2048, 16))
DIL_W = 128
MOBA_BLOCK = 256
MOBA_TOPK = 3
N_BUCKETS = 32
REL_MAX_DIST = 2048
CONV_WIDTH = 3
LN_EPS = 1e-5
NEG = -1e30
ATTN_SCALE = HEAD_DIM ** -0.5
LOG2E = math.log2(math.e)
VMEM_LIMIT = 56 * 1024 * 1024


def _cparams(*sem):
    return pltpu.CompilerParams(dimension_semantics=sem, vmem_limit_bytes=VMEM_LIMIT)


def _t5_bucket(dist):
    n = jnp.maximum(dist, 0)
    max_exact = N_BUCKETS // 2
    nf = jnp.maximum(n, 1).astype(F32)
    large = max_exact + (jnp.log(nf / max_exact) / math.log(REL_MAX_DIST / max_exact)
                         * (N_BUCKETS - max_exact)).astype(jnp.int32)
    large = jnp.minimum(large, N_BUCKETS - 1)
    return jnp.where(n < max_exact, n, large)


def _ada_kernel(c_ref, w_ref, b_ref, o_ref):
    c = c_ref[...]
    cond = c * (1.0 / (1.0 + jnp.exp(-c)))
    o_ref[...] = jnp.dot(cond, w_ref[...], precision=lax.Precision.HIGHEST,
                         preferred_element_type=F32) + b_ref[...]


def _ada_mod(c, w_ada, b_ada):
    depth, d, n = w_ada.shape
    bsz = c.shape[0]
    tn = 1024
    return pl.pallas_call(
        _ada_kernel,
        grid=(depth, n // tn),
        in_specs=[pl.BlockSpec((bsz, d), lambda l, j: (0, 0)),
                  pl.BlockSpec((None, d, tn), lambda l, j: (l, 0, j)),
                  pl.BlockSpec((None, 1, tn), lambda l, j: (l, 0, j))],
        out_specs=pl.BlockSpec((None, bsz, tn), lambda l, j: (l, 0, j)),
        out_shape=jax.ShapeDtypeStruct((depth, bsz, n), F32),
        compiler_params=_cparams("parallel", "parallel"),
    )(c, w_ada, b_ada.reshape(depth, 1, n))


def _modulate_kernel(x_ref, sc_ref, sh_ref, o_ref):
    o_ref[...] = (x_ref[...] * (1.0 + sc_ref[...]) + sh_ref[...]).astype(o_ref.dtype)


def _modulate(x, sc, sh):
    bsz, s, d = x.shape
    tm = 512
    vec = pl.BlockSpec((None, 1, d), lambda b, i: (b, 0, 0))
    return pl.pallas_call(
        _modulate_kernel,
        grid=(bsz, s // tm),
        in_specs=[pl.BlockSpec((None, tm, d), lambda b, i: (b, i, 0)), vec, vec],
        out_specs=pl.BlockSpec((None, tm, d), lambda b, i: (b, i, 0)),
        out_shape=jax.ShapeDtypeStruct((bsz, s, d), BF16),
        compiler_params=_cparams("parallel", "parallel"),
    )(x, sc, sh)


def _inproj_kernel(a_ref, w_ref, wf_ref, o_ref, f_ref):
    a = a_ref[...]
    o_ref[...] = jnp.dot(a, w_ref[...], preferred_element_type=F32)

    @pl.when(pl.program_id(1) == 0)
    def _():
        f_ref[...] = jnp.dot(a, wf_ref[...], preferred_element_type=F32)


def _inproj(h, w_main, w_f):
    t, d = h.shape
    n = w_main.shape[1]
    tm, tn = 512, 1024
    return pl.pallas_call(
        _inproj_kernel,
        grid=(t // tm, n // tn),
        in_specs=[pl.BlockSpec((tm, d), lambda i, j: (i, 0)),
                  pl.BlockSpec((d, tn), lambda i, j: (0, j)),
                  pl.BlockSpec((d, LANES), lambda i, j: (0, 0))],
        out_specs=[pl.BlockSpec((tm, tn), lambda i, j: (i, j)),
                   pl.BlockSpec((tm, LANES), lambda i, j: (i, 0))],
        out_shape=[jax.ShapeDtypeStruct((t, n), F32),
                   jax.ShapeDtypeStruct((t, LANES), F32)],
        compiler_params=_cparams("parallel", "arbitrary"),
    )(h, w_main, w_f)


TOEP_LEN = 512
TOEP_ROWS = 256


def _toeplitz_kernel(rr_ref, o_ref):
    x = jnp.broadcast_to(rr_ref[...], (TOEP_ROWS, TOEP_LEN))
    y = pltpu.roll(x, TOEP_LEN - TOEP_ROWS + 1, 1, stride=1, stride_axis=0)
    o_ref[...] = y[:, :o_ref.shape[1]]


def _toeplitz(rr, cols):
    g = rr.shape[0]
    return pl.pallas_call(
        _toeplitz_kernel,
        grid=(g,),
        in_specs=[pl.BlockSpec((None, 1, TOEP_LEN), lambda i: (i, 0, 0))],
        out_specs=pl.BlockSpec((None, TOEP_ROWS, cols), lambda i: (i, 0, 0)),
        out_shape=jax.ShapeDtypeStruct((g, TOEP_ROWS, cols), F32),
        compiler_params=_cparams("parallel"),
    )(rr.reshape(g, 1, TOEP_LEN))


def _bias_of_distance(rel, dist, valid):
    bias = jnp.moveaxis(rel[_t5_bucket(dist)], -1, 0)
    return jnp.where(valid, bias * LOG2E, NEG).astype(F32)


def _dilated_bias_tiles(rel_a):
    c = jnp.arange(TOEP_LEN)
    rows = []
    for (_, dil) in DIL_PATTERNS:
        d0 = c - (TOEP_ROWS - 1) + DIL_W
        d1 = c - (TOEP_ROWS - 1)
        rows.append(jnp.stack([_bias_of_distance(rel_a, d0 * dil, (d0 >= 0) & (d0 <= DIL_W)),
                               _bias_of_distance(rel_a, d1 * dil, d1 >= 0)]))
    rr = jnp.stack(rows)
    out = _toeplitz(rr.reshape(-1, TOEP_LEN), DIL_W)
    return out.reshape(len(DIL_PATTERNS), 2, H_A, TOEP_ROWS, DIL_W)


def _moba_num_tiles(seq):
    blk = MOBA_BLOCK
    d_all = np.arange(max(seq, 2))
    nf = np.maximum(d_all, 1).astype(np.float32)
    half = N_BUCKETS // 2
    big = half + (np.log(nf / half) / math.log(REL_MAX_DIST / half) * (N_BUCKETS - half)).astype(np.int32)
    bucket = np.where(d_all < half, d_all, np.minimum(big, N_BUCKETS - 1))
    last_change = int(np.max(np.nonzero(np.diff(bucket))[0])) + 1
    first_const = min(seq // blk - 1, -(-(last_change + 2 + blk - 1) // blk))
    return first_const + 1


def _moba_bias_tiles(rel_b, seq):
    ntile = _moba_num_tiles(seq)
    c = jnp.arange(TOEP_LEN)[None, :]
    d = jnp.arange(ntile)[:, None] * MOBA_BLOCK + c - (TOEP_ROWS - 1)
    rr = _bias_of_distance(rel_b, d, d >= 0)
    out = _toeplitz(rr.reshape(-1, TOEP_LEN), MOBA_BLOCK)
    return out.reshape(H_B, ntile, TOEP_ROWS, MOBA_BLOCK)


def _head_rows():
    row = lax.broadcasted_iota(jnp.int32, (LANES, 1), 0)
    return row < HEAD_DIM, row >= HEAD_DIM


DIL_UNROLL = 4


def _dilated_kernel(q_ref, k_ref, v_ref, bm_ref, o_ref, og_s, lse_s, *, seq):
    w = DIL_W
    own_rows = _head_rows()
    for g, (_, dil) in enumerate(DIL_PATTERNS):
        nb = (seq // dil) // w

        def rows(start, size, dil=dil):
            return pl.ds(start, size) if dil == 1 else pl.ds(start, size, stride=dil)

        def trip(it, carry, g=g, dil=dil, nb=nb, rows=rows):
            blocks = []
            for u in range(DIL_UNROLL):
                idx = it * DIL_UNROLL + u
                r = idx // nb
                n = idx % nb
                q_rows = rows(n * w * dil + r, w)
                kv_rows = rows(jnp.maximum(n - 1, 0) * w * dil + r, 2 * w)
                var = jnp.where(n == 0, 1, 0)
                qt = q_ref[q_rows, :].T * (ATTN_SCALE * LOG2E)
                k = k_ref[kv_rows, :].astype(BF16)
                vt = v_ref[kv_rows, :].T.astype(BF16)
                scores = [jnp.dot(k, jnp.where(own_rows[h], qt, 0.0).astype(BF16),
                                  preferred_element_type=F32) for h in range(2)]
                blocks.append((q_rows, var, vt, scores))
            results = []
            for (q_rows, var, vt, scores) in blocks:
                per_head = []
                for h in range(2):
                    s = scores[h] + bm_ref[g, var, h]
                    m = jnp.max(s, axis=0, keepdims=True)
                    p = jnp.exp2(s - m)
                    l = jnp.sum(p, axis=0, keepdims=True)
                    pv = jnp.dot(vt[h * HEAD_DIM:(h + 1) * HEAD_DIM, :], p.astype(BF16),
                                 preferred_element_type=F32)
                    per_head.append((pv, m, l))
                results.append((q_rows, per_head))
            for (q_rows, per_head) in results:
                o_t = jnp.concatenate([pv / l for (pv, _, l) in per_head], axis=0)
                lse_t = jnp.concatenate([jnp.broadcast_to(m + jnp.log(l) * LOG2E, (HEAD_DIM, w))
                                         for (_, m, l) in per_head], axis=0)
                og_s[g, q_rows, :] = o_t.T
                lse_s[g, q_rows, :] = lse_t.T
            return carry

        lax.fori_loop(0, nb * dil // DIL_UNROLL, trip, 0)

    top = jnp.maximum(jnp.maximum(lse_s[0], lse_s[1]), lse_s[2])
    num = jnp.zeros_like(top)
    den = jnp.zeros_like(top)
    for g in range(len(DIL_PATTERNS)):
        wg = jnp.exp2(lse_s[g] - top)
        num = num + wg * og_s[g]
        den = den + wg
    o_ref[...] = num / den


def _dilated_attention(qkv, bm_a, bsz, seq):
    npair = H_A // 2
    npat = len(DIL_PATTERNS)
    kern = functools.partial(_dilated_kernel, seq=seq)

    def col(off):
        return pl.BlockSpec((None, seq, LANES), lambda p, b, off=off: (b, 0, off + p))

    return pl.pallas_call(
        kern,
        grid=(npair, bsz),
        in_specs=[col(0), col(npair), col(2 * npair),
                  pl.BlockSpec((npat, 2, 2, 2 * DIL_W, DIL_W), lambda p, b: (0, 0, p, 0, 0))],
        out_specs=pl.BlockSpec((None, seq, LANES), lambda p, b: (b, 0, p)),
        out_shape=jax.ShapeDtypeStruct((bsz, seq, W_A), F32),
        scratch_shapes=[pltpu.VMEM((npat, seq, LANES), F32)] * 2,
        compiler_params=_cparams("parallel", "parallel"),
    )(qkv, qkv, qkv, bm_a)


VT_ROWS = HEAD_DIM + 16


def _ones_rows(width):
    row = lax.broadcasted_iota(jnp.int32, (VT_ROWS - HEAD_DIM, width), 0)
    return jnp.where(row == 0, 1.0, 0.0)


def _transpose_to(dst_ref, src_ref, chunk=512):
    for c in range(src_ref.shape[0] // chunk):
        cols = slice(c * chunk, (c + 1) * chunk)
        t = src_ref[cols, :].T
        for h in range(2):
            dst_ref[h, :, cols] = jnp.concatenate(
                [t[h * HEAD_DIM:(h + 1) * HEAD_DIM, :], _ones_rows(chunk)], axis=0).astype(BF16)


FLASH_CHUNK = 4


def _causal_rows(qi, tq, nblk, o_ref, kaug_s, vt_s, qa_s, ssc, psc, bias_fn):
    ck = FLASH_CHUNK * tq

    def tile(kj):
        return slice(kj * tq, (kj + 1) * tq)

    def scores(c, h):
        top = None
        for kj in range(c * FLASH_CHUNK, (c + 1) * FLASH_CHUNK):
            s = jnp.dot(kaug_s[h, tile(kj), :], qa_s[h], preferred_element_type=F32)
            s = bias_fn(h, kj, s)
            ssc[h, tile(kj), :] = s
            cmax = jnp.max(s, axis=0, keepdims=True)
            top = cmax if top is None else jnp.maximum(top, cmax)
        return top

    def values(c, h, top, state):
        m, acc = state
        m_new = jnp.maximum(m, top)
        for kj in range(c * FLASH_CHUNK, (c + 1) * FLASH_CHUNK):
            psc[h, tile(kj), :] = jnp.exp2(ssc[h, tile(kj), :] - m_new).astype(BF16)
        pv = jnp.dot(vt_s[h, :, c * ck:(c + 1) * ck], psc[h, c * ck:(c + 1) * ck, :],
                     preferred_element_type=F32)
        return m_new, jnp.exp2(m - m_new) * acc + pv

    def row_pass(nchunk):
        tops = {(0, h): scores(0, h) for h in range(2)}
        state = [(jnp.full((1, tq), NEG, F32), jnp.zeros((VT_ROWS, tq), F32)) for _ in range(2)]
        for c in range(nchunk):
            if c + 1 < nchunk:
                for h in range(2):
                    tops[(c + 1, h)] = scores(c + 1, h)
            for h in range(2):
                state[h] = values(c, h, tops[(c, h)], state[h])
        return jnp.concatenate([acc[:HEAD_DIM] / acc[HEAD_DIM:HEAD_DIM + 1]
                                for (_, acc) in state], axis=0).T

    for nchunk in range(1, nblk // FLASH_CHUNK + 1):
        @pl.when(qi // FLASH_CHUNK == nchunk - 1)
        def _(nchunk=nchunk):
            o_ref[...] = row_pass(nchunk)


def _moba_kernel(q_ref, k_ref, v_ref, bt_ref, o_ref,
                 kaug_s, vt_s, kmean_s, qa_s, ssc, psc, *, nblk, ntile):
    blk = MOBA_BLOCK
    qi = pl.program_id(2)
    spare = LANES - HEAD_DIM - nblk

    @pl.when(qi == 0)
    def _():
        lane = lax.broadcasted_iota(jnp.int32, (1, LANES), 1)
        for n in range(nblk):
            k = k_ref[n * blk:(n + 1) * blk, :]
            kaug_s[0, n * blk:(n + 1) * blk, :] = jnp.where(
                lane < HEAD_DIM, k, (lane == HEAD_DIM + n).astype(F32)).astype(BF16)
            kaug_s[1, n * blk:(n + 1) * blk, :] = jnp.where(
                lane >= HEAD_DIM, k, (lane == n).astype(F32)).astype(BF16)
            kmean_s[n:n + 1, :] = jnp.mean(k, axis=0, keepdims=True)
        _transpose_to(vt_s, v_ref)

    qt = q_ref[...].T
    own_rows = _head_rows()
    blk_id = lax.broadcasted_iota(jnp.int32, (nblk, 1), 0)
    blk_idf = blk_id.astype(F32)
    past = blk_id < qi
    for h in range(2):
        qh = jnp.where(own_rows[h], qt, 0.0)
        gate = jnp.dot(kmean_s[...], qh, precision=lax.Precision.HIGHEST,
                       preferred_element_type=F32)
        gate = jnp.where(past, gate, NEG)
        chosen = jnp.zeros(gate.shape, jnp.bool_)
        for _ in range(MOBA_TOPK):
            mx = jnp.max(gate, axis=0, keepdims=True)
            first_idx = jnp.min(jnp.where(gate == mx, blk_idf, float(nblk)), axis=0, keepdims=True)
            pick = blk_idf == first_idx
            chosen = jnp.logical_or(chosen, pick)
            gate = jnp.where(pick, -jnp.inf, gate)
        keep = jnp.logical_or(jnp.logical_and(chosen, past), blk_id == qi)
        selb = jnp.where(keep, 0.0, NEG)
        qs = qt[h * HEAD_DIM:(h + 1) * HEAD_DIM, :] * (ATTN_SCALE * LOG2E)
        pad = jnp.zeros((spare, blk), F32)
        parts = [qs, selb, pad] if h == 0 else [selb, pad, qs]
        qa_s[h] = jnp.concatenate(parts, axis=0).astype(BF16)

    _causal_rows(qi, blk, nblk, o_ref, kaug_s, vt_s, qa_s, ssc, psc,
                 bias_fn=lambda h, kj, s: s + bt_ref[h, jnp.clip(qi - kj, 0, ntile - 1)])


def _moba_attention(qkv, bt_b, bsz, seq, col0):
    npair = H_B // 2
    nblk = seq // MOBA_BLOCK
    ntile = bt_b.shape[1]
    assert nblk % 8 == 0 and nblk <= LANES - HEAD_DIM
    kern = functools.partial(_moba_kernel, nblk=nblk, ntile=ntile)
    blk = MOBA_BLOCK

    def col(off):
        return pl.BlockSpec((None, seq, LANES), lambda p, b, i, off=off: (b, 0, off + p))

    return pl.pallas_call(
        kern,
        grid=(npair, bsz, nblk),
        in_specs=[pl.BlockSpec((None, blk, LANES), lambda p, b, i: (b, i, col0 + p)),
                  col(col0 + npair), col(col0 + 2 * npair),
                  pl.BlockSpec((2, ntile, blk, blk), lambda p, b, i: (p, 0, 0, 0))],
        out_specs=pl.BlockSpec((None, blk, LANES), lambda p, b, i: (b, i, p)),
        out_shape=jax.ShapeDtypeStruct((bsz, seq, W_B), F32),
        scratch_shapes=[pltpu.VMEM((2, seq, LANES), BF16), pltpu.VMEM((2, VT_ROWS, seq), BF16),
                        pltpu.VMEM((nblk, LANES), F32), pltpu.VMEM((2, LANES, blk), BF16),
                        pltpu.VMEM((2, seq, blk), F32), pltpu.VMEM((2, seq, blk), BF16)],
        compiler_params=_cparams("parallel", "parallel", "arbitrary"),
    )(qkv, qkv, qkv, bt_b)


CUM_SPLIT = 3
CUM_CHUNK = 256
FOX_TILE = 256


def _fgate_kernel(f_ref, bf_ref, pm_ref, o_ref, cum_s):
    seq = f_ref.shape[0]
    ch = CUM_CHUNK
    i = lax.broadcasted_iota(jnp.int32, (ch, ch), 0)
    j = lax.broadcasted_iota(jnp.int32, (ch, ch), 1)
    lower = (j <= i).astype(F32)
    carry = jnp.zeros((1, LANES), F32)
    for c in range(seq // ch):
        z = f_ref[c * ch:(c + 1) * ch, :] + bf_ref[...]
        ls = jnp.minimum(z, 0.0) - jnp.log(1.0 + jnp.exp(-jnp.abs(z)))
        cum = jnp.dot(lower, ls, precision=lax.Precision.HIGHEST, preferred_element_type=F32) + carry
        cum_s[c * ch:(c + 1) * ch, :] = cum
        carry = cum[ch - 1:ch, :]
    rest = cum_s[...] * (-LOG2E)
    out = None
    for t in range(CUM_SPLIT):
        term = rest.astype(BF16)
        rest = rest - term.astype(F32)
        part = jnp.dot(term, pm_ref[t], preferred_element_type=F32)
        out = part if out is None else out + part
    o_ref[...] = out.astype(BF16)


def _forget_keys(f, b_f):
    bsz, seq, _ = f.shape
    npair = H_C // 2
    nblk = seq // FOX_TILE
    pm = np.zeros((CUM_SPLIT, LANES, npair * LANES), np.float32)
    for t in range(CUM_SPLIT):
        for p in range(npair):
            pm[t, 2 * p, p * LANES + HEAD_DIM + nblk + t] = 1.0
            pm[t, 2 * p + 1, p * LANES + nblk + t] = 1.0
    bf = jnp.pad(b_f, (0, LANES - b_f.shape[0])).reshape(1, LANES)
    return pl.pallas_call(
        _fgate_kernel,
        grid=(bsz,),
        in_specs=[pl.BlockSpec((None, seq, LANES), lambda b: (b, 0, 0)),
                  pl.BlockSpec((1, LANES), lambda b: (0, 0)),
                  pl.BlockSpec((CUM_SPLIT, LANES, npair * LANES), lambda b: (0, 0, 0))],
        out_specs=pl.BlockSpec((None, seq, npair * LANES), lambda b: (b, 0, 0)),
        out_shape=jax.ShapeDtypeStruct((bsz, seq, npair * LANES), BF16),
        scratch_shapes=[pltpu.VMEM((seq, LANES), F32)],
        compiler_params=_cparams("parallel"),
    )(f, bf, jnp.asarray(pm, BF16))


def _fox_kernel(q_ref, k_ref, v_ref, ca_ref, o_ref,
                kaug_s, vt_s, qa_s, cm_s, ssc, psc, *, tq, nblk):
    qi = pl.program_id(2)
    spare = LANES - HEAD_DIM - nblk - 8

    @pl.when(qi == 0)
    def _():
        lane = lax.broadcasted_iota(jnp.int32, (1, LANES), 1)
        for n in range(nblk):
            k = k_ref[n * tq:(n + 1) * tq, :]
            ca = ca_ref[n * tq:(n + 1) * tq, :].astype(F32)
            kaug_s[0, n * tq:(n + 1) * tq, :] = jnp.where(
                lane < HEAD_DIM, k, ca + (lane == HEAD_DIM + n).astype(F32)).astype(BF16)
            kaug_s[1, n * tq:(n + 1) * tq, :] = jnp.where(
                lane >= HEAD_DIM, k, ca + (lane == n).astype(F32)).astype(BF16)
        _transpose_to(vt_s, v_ref)
        key = lax.broadcasted_iota(jnp.int32, (tq, tq), 0)
        qry = lax.broadcasted_iota(jnp.int32, (tq, tq), 1)
        cm_s[0] = jnp.zeros((tq, tq), F32)
        cm_s[1] = jnp.where(key <= qry, 0.0, NEG)

    qt = q_ref[...].T * (ATTN_SCALE * LOG2E)
    blk_id = lax.broadcasted_iota(jnp.int32, (nblk, 1), 0)
    future = jnp.broadcast_to(jnp.where(blk_id <= qi, 0.0, NEG), (nblk, tq))
    sub = lax.broadcasted_iota(jnp.int32, (8, 1), 0)
    ones = jnp.broadcast_to(jnp.where(sub < CUM_SPLIT, 1.0, 0.0), (8, tq))
    pad = jnp.zeros((spare, tq), F32)
    for h in range(2):
        qs = qt[h * HEAD_DIM:(h + 1) * HEAD_DIM, :]
        parts = [qs, future, ones, pad] if h == 0 else [future, ones, pad, qs]
        qa_s[h] = jnp.concatenate(parts, axis=0).astype(BF16)

    _causal_rows(qi, tq, nblk, o_ref, kaug_s, vt_s, qa_s, ssc, psc,
                 bias_fn=lambda h, kj, s: s + cm_s[jnp.where(kj == qi, 1, 0)])


def _fox_attention(qkv, caug, bsz, seq, col0):
    npair = H_C // 2
    tq = FOX_TILE
    nblk = seq // tq
    assert nblk % 8 == 0 and nblk % FLASH_CHUNK == 0 and nblk + 8 <= LANES - HEAD_DIM
    kern = functools.partial(_fox_kernel, tq=tq, nblk=nblk)

    def col(off):
        return pl.BlockSpec((None, seq, LANES), lambda p, b, i, off=off: (b, 0, off + p))

    return pl.pallas_call(
        kern,
        grid=(npair, bsz, seq // tq),
        in_specs=[pl.BlockSpec((None, tq, LANES), lambda p, b, i: (b, i, col0 + p)),
                  col(col0 + npair), col(col0 + 2 * npair),
                  pl.BlockSpec((None, seq, LANES), lambda p, b, i: (b, 0, p))],
        out_specs=pl.BlockSpec((None, tq, LANES), lambda p, b, i: (b, i, p)),
        out_shape=jax.ShapeDtypeStruct((bsz, seq, W_C), F32),
        scratch_shapes=[pltpu.VMEM((2, seq, LANES), BF16), pltpu.VMEM((2, VT_ROWS, seq), BF16),
                        pltpu.VMEM((2, LANES, tq), BF16), pltpu.VMEM((2, tq, tq), F32),
                        pltpu.VMEM((2, seq, tq), F32), pltpu.VMEM((2, seq, tq), BF16)],
        compiler_params=_cparams("parallel", "parallel", "arbitrary"),
    )(qkv, qkv, qkv, caug)


def _layer_norm_rows(z, g, b):
    mu = jnp.mean(z, axis=-1, keepdims=True)
    zc = z - mu
    var = jnp.mean(zc * zc, axis=-1, keepdims=True)
    return zc * lax.rsqrt(var + LN_EPS) * g + b


def _mixout_kernel(oa_ref, ob_ref, oc_ref, ga_ref, gb_ref, gc_ref, wa_ref, wb_ref, wc_ref,
                   x_ref, g1_ref, lng_ref, lnb_ref, sc_ref, sh_ref, x_out, h_out, *, alpha):
    def normed(o_ref, g_ref):
        o = o_ref[...]
        ms = jnp.mean(o * o, axis=-1, keepdims=True)
        return (o * lax.rsqrt(ms + LN_EPS) * g_ref[...]).astype(BF16)

    y = jnp.dot(normed(oa_ref, ga_ref), wa_ref[...], preferred_element_type=F32)
    y = y + jnp.dot(normed(ob_ref, gb_ref), wb_ref[...], preferred_element_type=F32)
    y = y + jnp.dot(normed(oc_ref, gc_ref), wc_ref[...], preferred_element_type=F32)
    x1 = _layer_norm_rows(alpha * x_ref[...] + (1.0 + g1_ref[...]) * y, lng_ref[...], lnb_ref[...])
    x_out[...] = x1
    h_out[...] = (x1 * (1.0 + sc_ref[...]) + sh_ref[...]).astype(BF16)


def _mix_out(o_a, o_b, o_c, g_a, g_b, g_c, w_out_bf, x, g1, ln_g, ln_b, sc2, sh2, alpha):
    bsz, seq, d = x.shape
    tm = 256
    w_a, w_b, w_c = w_out_bf[:W_A], w_out_bf[W_A:W_A + W_B], w_out_bf[W_A + W_B:]

    def act(width):
        return pl.BlockSpec((None, tm, width), lambda b, i: (b, i, 0))

    def const(shape):
        return pl.BlockSpec(shape, lambda b, i: (0,) * len(shape))

    bvec = pl.BlockSpec((None, 1, d), lambda b, i: (b, 0, 0))
    return pl.pallas_call(
        functools.partial(_mixout_kernel, alpha=alpha),
        grid=(bsz, seq // tm),
        in_specs=[act(W_A), act(W_B), act(W_C),
                  const((1, W_A)), const((1, W_B)), const((1, W_C)),
                  const((W_A, d)), const((W_B, d)), const((W_C, d)),
                  act(d), bvec, const((1, d)), const((1, d)), bvec, bvec],
        out_specs=[act(d), act(d)],
        out_shape=[jax.ShapeDtypeStruct((bsz, seq, d), F32),
                   jax.ShapeDtypeStruct((bsz, seq, d), BF16)],
        compiler_params=_cparams("parallel", "parallel"),
    )(o_a, o_b, o_c, g_a.reshape(1, -1), g_b.reshape(1, -1), g_c.reshape(1, -1),
      w_a, w_b, w_c, x, g1, ln_g.reshape(1, d), ln_b.reshape(1, d), sc2, sh2)


FFN_HALO = 8


def _ffn_up_kernel(h_ref, wa_ref, wb_ref, cwa_ref, cwb_ref, cba_ref, cbb_ref, o_ref, bufa, bufb):
    i = pl.program_id(2)
    tm = h_ref.shape[0]
    h = h_ref[...]

    def conv(w_ref, cw_ref, cb_ref, buf):
        @pl.when(i == 0)
        def _():
            buf[0:FFN_HALO, :] = jnp.zeros((FFN_HALO, buf.shape[1]), F32)

        u = jnp.dot(h, w_ref[...], preferred_element_type=F32)
        buf[FFN_HALO:FFN_HALO + tm, :] = u
        out = (cw_ref[2:3, :] * u
               + cw_ref[1:2, :] * buf[FFN_HALO - 1:FFN_HALO - 1 + tm, :]
               + cw_ref[0:1, :] * buf[FFN_HALO - 2:FFN_HALO - 2 + tm, :]
               + cb_ref[...])
        buf[0:FFN_HALO, :] = u[tm - FFN_HALO:tm, :]
        return out

    a = conv(wa_ref, cwa_ref, cba_ref, bufa)
    b = conv(wb_ref, cwb_ref, cbb_ref, bufb)
    o_ref[...] = (a * (1.0 / (1.0 + jnp.exp(-a))) * b).astype(o_ref.dtype)


def _ffn_up(h, w_up_bf, conv_w, conv_b):
    bsz, seq, d = h.shape
    dff = w_up_bf.shape[1] // 2
    tm, tn = 512, 512
    nj = dff // tn
    cb = conv_b.reshape(1, -1)
    return pl.pallas_call(
        _ffn_up_kernel,
        grid=(nj, bsz, seq // tm),
        in_specs=[pl.BlockSpec((None, tm, d), lambda j, b, i: (b, i, 0)),
                  pl.BlockSpec((d, tn), lambda j, b, i: (0, j)),
                  pl.BlockSpec((d, tn), lambda j, b, i: (0, nj + j)),
                  pl.BlockSpec((CONV_WIDTH, tn), lambda j, b, i: (0, j)),
                  pl.BlockSpec((CONV_WIDTH, tn), lambda j, b, i: (0, nj + j)),
                  pl.BlockSpec((1, tn), lambda j, b, i: (0, j)),
                  pl.BlockSpec((1, tn), lambda j, b, i: (0, nj + j))],
        out_specs=pl.BlockSpec((None, tm, tn), lambda j, b, i: (b, i, j)),
        out_shape=jax.ShapeDtypeStruct((bsz, seq, dff), BF16),
        scratch_shapes=[pltpu.VMEM((tm + FFN_HALO, tn), F32)] * 2,
        compiler_params=_cparams("parallel", "parallel", "arbitrary"),
    )(h, w_up_bf, w_up_bf, conv_w, conv_w, cb, cb)


def _ffn_down_kernel(g_ref, w_ref, x_ref, g2_ref, lng_ref, lnb_ref, sc_ref, sh_ref,
                     x_out, h_out, acc_s, *, alpha):
    k = pl.program_id(2)

    @pl.when(k == 0)
    def _():
        acc_s[...] = jnp.zeros_like(acc_s)

    acc_s[...] += jnp.dot(g_ref[...], w_ref[...], preferred_element_type=F32)

    @pl.when(k == pl.num_programs(2) - 1)
    def _():
        x2 = _layer_norm_rows(alpha * x_ref[...] + (1.0 + g2_ref[...]) * acc_s[...],
                              lng_ref[...], lnb_ref[...])
        x_out[...] = x2
        h_out[...] = (x2 * (1.0 + sc_ref[...]) + sh_ref[...]).astype(BF16)


def _ffn_down(g, w_down_bf, x, g2, ln_g, ln_b, sc_next, sh_next, alpha):
    bsz, seq, d = x.shape
    dff = g.shape[2]
    tm, tk = 512, 512
    act = pl.BlockSpec((None, tm, d), lambda b, i, k: (b, i, 0))
    bvec = pl.BlockSpec((None, 1, d), lambda b, i, k: (b, 0, 0))
    const = pl.BlockSpec((1, d), lambda b, i, k: (0, 0))
    return pl.pallas_call(
        functools.partial(_ffn_down_kernel, alpha=alpha),
        grid=(bsz, seq // tm, dff // tk),
        in_specs=[pl.BlockSpec((None, tm, tk), lambda b, i, k: (b, i, k)),
                  pl.BlockSpec((tk, d), lambda b, i, k: (k, 0)),
                  act, bvec, const, const, bvec, bvec],
        out_specs=[act, act],
        out_shape=[jax.ShapeDtypeStruct((bsz, seq, d), F32),
                   jax.ShapeDtypeStruct((bsz, seq, d), BF16)],
        scratch_shapes=[pltpu.VMEM((tm, d), F32)],
        compiler_params=_cparams("parallel", "parallel", "arbitrary"),
    )(g, w_down_bf, x, g2, ln_g.reshape(1, d), ln_b.reshape(1, d), sc_next, sh_next)


def kernel(x, c, rel_bias, w_ada, b_ada, w_in, b_f, g_mix_a, g_mix_b, g_mix_c, w_out,
           ln1_g, ln1_b, w_up, conv_w, conv_b, w_down, ln2_g, ln2_b):
    bsz, seq, d = x.shape
    depth = w_ada.shape[0]
    alpha = (2 * depth) ** 0.25
    n_qkv = 3 * (W_A + W_B + W_C)

    mod = _ada_mod(c, w_ada, b_ada)

    def mod_vec(l, idx):
        return mod[l, :, idx * d:(idx + 1) * d].reshape(bsz, 1, d)

    bm_a = _dilated_bias_tiles(rel_bias[:, :H_A])
    bt_b = _moba_bias_tiles(rel_bias[:, H_A:], seq)

    h = _modulate(x, mod_vec(0, 1), mod_vec(0, 0))
    for l in range(depth):
        w_in_bf = w_in[l].astype(BF16)
        w_f = jnp.pad(w_in_bf[:, n_qkv:], ((0, 0), (0, LANES - H_C)))
        qkv, f = _inproj(h.reshape(bsz * seq, d), w_in_bf[:, :n_qkv], w_f)
        qkv = qkv.reshape(bsz, seq, n_qkv)
        caug = _forget_keys(f.reshape(bsz, seq, LANES), b_f[l])

        o_a = _dilated_attention(qkv, bm_a, bsz, seq)
        o_b = _moba_attention(qkv, bt_b, bsz, seq, 3 * W_A // LANES)
        o_c = _fox_attention(qkv, caug, bsz, seq, 3 * (W_A + W_B) // LANES)

        x, h = _mix_out(o_a, o_b, o_c, g_mix_a[l], g_mix_b[l], g_mix_c[l], w_out[l].astype(BF16),
                        x, mod_vec(l, 2), ln1_g[l], ln1_b[l], mod_vec(l, 4), mod_vec(l, 3), alpha)
        g = _ffn_up(h, w_up[l].astype(BF16), conv_w[l], conv_b[l])
        nxt = min(l + 1, depth - 1)
        x, h = _ffn_down(g, w_down[l].astype(BF16), x, mod_vec(l, 5), ln2_g[l], ln2_b[l],
                         mod_vec(nxt, 1), mod_vec(nxt, 0), alpha)
    return x
```

```python
import functools
import math

import numpy as np
import jax
import jax.numpy as jnp
from jax import lax
from jax.experimental import pallas as pl
from jax.experimental.pallas import tpu as pltpu

F32 = jnp.float32
BF16 = jnp.bfloat16

HEAD_DIM = 64
LANES = 128
H_A, H_B, H_C = 12, 10, 10
W_A, W_B, W_C = H_A * HEAD_DIM, H_B * HEAD_DIM, H_C * HEAD_DIM
DIL_PATTERNS = ((128, 1), (512, 4), (2048, 16))
DIL_W = 128
MOBA_BLOCK = 256
MOBA_TOPK = 3
N_BUCKETS = 32
REL_MAX_DIST = 2048
CONV_WIDTH = 3
LN_EPS = 1e-5
NEG = -1e30
ATTN_SCALE = HEAD_DIM ** -0.5
LOG2E = math.log2(math.e)
VMEM_LIMIT = 56 * 1024 * 1024


def _cparams(*sem):
    return pltpu.CompilerParams(dimension_semantics=sem, vmem_limit_bytes=VMEM_LIMIT)


def _t5_bucket(dist):
    n = jnp.maximum(dist, 0)
    max_exact = N_BUCKETS // 2
    nf = jnp.maximum(n, 1).astype(F32)
    large = max_exact + (jnp.log(nf / max_exact) / math.log(REL_MAX_DIST / max_exact)
                         * (N_BUCKETS - max_exact)).astype(jnp.int32)
    large = jnp.minimum(large, N_BUCKETS - 1)
    return jnp.where(n < max_exact, n, large)


def _ada_kernel(c_ref, w_ref, b_ref, o_ref):
    c = c_ref[...]
    cond = c * (1.0 / (1.0 + jnp.exp(-c)))
    o_ref[...] = jnp.dot(cond, w_ref[...], precision=lax.Precision.HIGHEST,
                         preferred_element_type=F32) + b_ref[...]


def _ada_mod(c, w_ada, b_ada):
    depth, d, n = w_ada.shape
    bsz = c.shape[0]
    tn = 1024
    return pl.pallas_call(
        _ada_kernel,
        grid=(depth, n // tn),
        in_specs=[pl.BlockSpec((bsz, d), lambda l, j: (0, 0)),
                  pl.BlockSpec((None, d, tn), lambda l, j: (l, 0, j)),
                  pl.BlockSpec((None, 1, tn), lambda l, j: (l, 0, j))],
        out_specs=pl.BlockSpec((None, bsz, tn), lambda l, j: (l, 0, j)),
        out_shape=jax.ShapeDtypeStruct((depth, bsz, n), F32),
        compiler_params=_cparams("parallel", "parallel"),
    )(c, w_ada, b_ada.reshape(depth, 1, n))


def _modulate_kernel(x_ref, sc_ref, sh_ref, o_ref):
    o_ref[...] = (x_ref[...] * (1.0 + sc_ref[...]) + sh_ref[...]).astype(o_ref.dtype)


def _modulate(x, sc, sh):
    bsz, s, d = x.shape
    tm = 512
    vec = pl.BlockSpec((None, 1, d), lambda b, i: (b, 0, 0))
    return pl.pallas_call(
        _modulate_kernel,
        grid=(bsz, s // tm),
        in_specs=[pl.BlockSpec((None, tm, d), lambda b, i: (b, i, 0)), vec, vec],
        out_specs=pl.BlockSpec((None, tm, d), lambda b, i: (b, i, 0)),
        out_shape=jax.ShapeDtypeStruct((bsz, s, d), BF16),
        compiler_params=_cparams("parallel", "parallel"),
    )(x, sc, sh)


def _inproj_kernel(a_ref, w_ref, wf_ref, o_ref, f_ref):
    a = a_ref[...]
    o_ref[...] = jnp.dot(a, w_ref[...], preferred_element_type=F32)

    @pl.when(pl.program_id(1) == 0)
    def _():
        f_ref[...] = jnp.dot(a, wf_ref[...], preferred_element_type=F32)


def _inproj(h, w_main, w_f):
    t, d = h.shape
    n = w_main.shape[1]
    tm, tn = 1024, 1024
    return pl.pallas_call(
        _inproj_kernel,
        grid=(t // tm, n // tn),
        in_specs=[pl.BlockSpec((tm, d), lambda i, j: (i, 0)),
                  pl.BlockSpec((d, tn), lambda i, j: (0, j)),
                  pl.BlockSpec((d, LANES), lambda i, j: (0, 0))],
        out_specs=[pl.BlockSpec((tm, tn), lambda i, j: (i, j)),
                   pl.BlockSpec((tm, LANES), lambda i, j: (i, 0))],
        out_shape=[jax.ShapeDtypeStruct((t, n), F32),
                   jax.ShapeDtypeStruct((t, LANES), F32)],
        compiler_params=_cparams("parallel", "arbitrary"),
    )(h, w_main, w_f)


TOEP_LEN = 512
TOEP_ROWS = 256


def _toeplitz_kernel(rr_ref, o_ref):
    x = jnp.broadcast_to(rr_ref[...], (TOEP_ROWS, TOEP_LEN))
    y = pltpu.roll(x, TOEP_LEN - TOEP_ROWS + 1, 1, stride=1, stride_axis=0)
    o_ref[...] = y[:, :o_ref.shape[1]]


def _toeplitz(rr, cols):
    g = rr.shape[0]
    return pl.pallas_call(
        _toeplitz_kernel,
        grid=(g,),
        in_specs=[pl.BlockSpec((None, 1, TOEP_LEN), lambda i: (i, 0, 0))],
        out_specs=pl.BlockSpec((None, TOEP_ROWS, cols), lambda i: (i, 0, 0)),
        out_shape=jax.ShapeDtypeStruct((g, TOEP_ROWS, cols), F32),
        compiler_params=_cparams("parallel"),
    )(rr.reshape(g, 1, TOEP_LEN))


def _bias_of_distance(rel, dist, valid):
    bias = jnp.moveaxis(rel[_t5_bucket(dist)], -1, 0)
    return jnp.where(valid, bias * LOG2E, NEG).astype(F32)


def _dilated_bias_tiles(rel_a):
    c = jnp.arange(TOEP_LEN)
    rows = []
    for (_, dil) in DIL_PATTERNS:
        d0 = c - (TOEP_ROWS - 1) + DIL_W
        d1 = c - (TOEP_ROWS - 1)
        rows.append(jnp.stack([_bias_of_distance(rel_a, d0 * dil, (d0 >= 0) & (d0 <= DIL_W)),
                               _bias_of_distance(rel_a, d1 * dil, d1 >= 0)]))
    rr = jnp.stack(rows)
    out = _toeplitz(rr.reshape(-1, TOEP_LEN), DIL_W)
    return out.reshape(len(DIL_PATTERNS), 2, H_A, TOEP_ROWS, DIL_W)


def _moba_num_tiles(seq):
    blk = MOBA_BLOCK
    d_all = np.arange(max(seq, 2))
    nf = np.maximum(d_all, 1).astype(np.float32)
    half = N_BUCKETS // 2
    big = half + (np.log(nf / half) / math.log(REL_MAX_DIST / half) * (N_BUCKETS - half)).astype(np.int32)
    bucket = np.where(d_all < half, d_all, np.minimum(big, N_BUCKETS - 1))
    last_change = int(np.max(np.nonzero(np.diff(bucket))[0])) + 1
    first_const = min(seq // blk - 1, -(-(last_change + 2 + blk - 1) // blk))
    return first_const + 1


def _moba_bias_tiles(rel_b, seq):
    ntile = _moba_num_tiles(seq)
    c = jnp.arange(TOEP_LEN)[None, :]
    d = jnp.arange(ntile)[:, None] * MOBA_BLOCK + c - (TOEP_ROWS - 1)
    rr = _bias_of_distance(rel_b, d, d >= 0)
    out = _toeplitz(rr.reshape(-1, TOEP_LEN), MOBA_BLOCK)
    return out.reshape(H_B, ntile, TOEP_ROWS, MOBA_BLOCK)


def _head_rows():
    row = lax.broadcasted_iota(jnp.int32, (LANES, 1), 0)
    return row < HEAD_DIM, row >= HEAD_DIM


DIL_UNROLL = 4


def _dilated_kernel(q_ref, k_ref, v_ref, bm_ref, o_ref, og_s, lse_s, sblk_s, pblk_s, vts_s, *, seq):
    w = DIL_W
    own_rows = _head_rows()
    for g, (_, dil) in enumerate(DIL_PATTERNS):
        nb = (seq // dil) // w

        def rows(start, size, dil=dil):
            return pl.ds(start, size) if dil == 1 else pl.ds(start, size, stride=dil)

        def trip(it, carry, g=g, dil=dil, nb=nb, rows=rows):
            metas = []
            for u in range(DIL_UNROLL):
                idx = it * DIL_UNROLL + u
                r = idx // nb
                n = idx % nb
                q_rows = rows(n * w * dil + r, w)
                kv_rows = rows(jnp.maximum(n - 1, 0) * w * dil + r, 2 * w)
                var = jnp.where(n == 0, 1, 0)
                qt = q_ref[q_rows, :].T * (ATTN_SCALE * LOG2E)
                k = k_ref[kv_rows, :].astype(BF16)
                vt = v_ref[kv_rows, :].T
                tops = []
                for h in range(2):
                    vts_s[u, h] = jnp.concatenate(
                        [vt[h * HEAD_DIM:(h + 1) * HEAD_DIM, :], _ones_rows(2 * w)], axis=0).astype(BF16)
                    s = jnp.dot(k, jnp.where(own_rows[h], qt, 0.0).astype(BF16),
                                preferred_element_type=F32) + bm_ref[g, var, h]
                    sblk_s[u, h] = s
                    tops.append(jnp.max(s, axis=0, keepdims=True))
                metas.append((q_rows, tops))
            pvs = []
            for u, (_, tops) in enumerate(metas):
                for h in range(2):
                    pblk_s[u, h] = jnp.exp2(sblk_s[u, h] - tops[h]).astype(BF16)
                    pvs.append(jnp.dot(vts_s[u, h], pblk_s[u, h], preferred_element_type=F32))
            for u, (q_rows, tops) in enumerate(metas):
                pair = pvs[2 * u:2 * u + 2]
                o_t = jnp.concatenate([pv[:HEAD_DIM] / pv[HEAD_DIM:HEAD_DIM + 1] for pv in pair], axis=0)
                lse_t = jnp.concatenate(
                    [jnp.broadcast_to(m + jnp.log(pv[HEAD_DIM:HEAD_DIM + 1]) * LOG2E, (HEAD_DIM, w))
                     for m, pv in zip(tops, pair)], axis=0)
                og_s[g, q_rows, :] = o_t.T
                lse_s[g, q_rows, :] = lse_t.T
            return carry

        lax.fori_loop(0, nb * dil // DIL_UNROLL, trip, 0)

    top = jnp.maximum(jnp.maximum(lse_s[0], lse_s[1]), lse_s[2])
    num = jnp.zeros_like(top)
    den = jnp.zeros_like(top)
    for g in range(len(DIL_PATTERNS)):
        wg = jnp.exp2(lse_s[g] - top)
        num = num + wg * og_s[g]
        den = den + wg
    o_ref[...] = num / den


def _dilated_attention(qkv, bm_a, bsz, seq):
    npair = H_A // 2
    npat = len(DIL_PATTERNS)
    kern = functools.partial(_dilated_kernel, seq=seq)

    def col(off):
        return pl.BlockSpec((None, seq, LANES), lambda p, b, off=off: (b, 0, off + p))

    return pl.pallas_call(
        kern,
        grid=(npair, bsz),
        in_specs=[col(0), col(npair), col(2 * npair),
                  pl.BlockSpec((npat, 2, 2, 2 * DIL_W, DIL_W), lambda p, b: (0, 0, p, 0, 0))],
        out_specs=pl.BlockSpec((None, seq, LANES), lambda p, b: (b, 0, p)),
        out_shape=jax.ShapeDtypeStruct((bsz, seq, W_A), F32),
        scratch_shapes=[pltpu.VMEM((npat, seq, LANES), F32)] * 2 + [
            pltpu.VMEM((DIL_UNROLL, 2, 2 * DIL_W, DIL_W), F32),
            pltpu.VMEM((DIL_UNROLL, 2, 2 * DIL_W, DIL_W), BF16),
            pltpu.VMEM((DIL_UNROLL, 2, VT_ROWS, 2 * DIL_W), BF16)],
        compiler_params=_cparams("parallel", "parallel"),
    )(qkv, qkv, qkv, bm_a)


VT_ROWS = HEAD_DIM + 16


def _ones_rows(width):
    row = lax.broadcasted_iota(jnp.int32, (VT_ROWS - HEAD_DIM, width), 0)
    return jnp.where(row == 0, 1.0, 0.0)


def _transpose_to(dst_ref, src_ref, chunk=512):
    for c in range(src_ref.shape[0] // chunk):
        cols = slice(c * chunk, (c + 1) * chunk)
        t = src_ref[cols, :].T
        for h in range(2):
            dst_ref[h, :, cols] = jnp.concatenate(
                [t[h * HEAD_DIM:(h + 1) * HEAD_DIM, :], _ones_rows(chunk)], axis=0).astype(BF16)


FLASH_CHUNK = 4


def _causal_rows(qi, tq, nblk, o_ref, kaug_s, vt_s, qa_s, ssc, psc, bias_fn):
    ck = FLASH_CHUNK * tq

    def tile(kj):
        return slice(kj * tq, (kj + 1) * tq)

    def scores(c, h):
        top = None
        for kj in range(c * FLASH_CHUNK, (c + 1) * FLASH_CHUNK):
            s = jnp.dot(kaug_s[h, tile(kj), :], qa_s[h], preferred_element_type=F32)
            s = bias_fn(h, kj, s)
            ssc[h, tile(kj), :] = s
            cmax = jnp.max(s, axis=0, keepdims=True)
            top = cmax if top is None else jnp.maximum(top, cmax)
        return top

    def values(c, h, top, state):
        m, acc = state
        m_new = jnp.maximum(m, top)
        for kj in range(c * FLASH_CHUNK, (c + 1) * FLASH_CHUNK):
            psc[h, tile(kj), :] = jnp.exp2(ssc[h, tile(kj), :] - m_new).astype(BF16)
        pv = jnp.dot(vt_s[h, :, c * ck:(c + 1) * ck], psc[h, c * ck:(c + 1) * ck, :],
                     preferred_element_type=F32)
        return m_new, jnp.exp2(m - m_new) * acc + pv

    def row_pass(nchunk):
        tops = {(0, h): scores(0, h) for h in range(2)}
        state = [(jnp.full((1, tq), NEG, F32), jnp.zeros((VT_ROWS, tq), F32)) for _ in range(2)]
        for c in range(nchunk):
            if c + 1 < nchunk:
                for h in range(2):
                    tops[(c + 1, h)] = scores(c + 1, h)
            for h in range(2):
                state[h] = values(c, h, tops[(c, h)], state[h])
        return jnp.concatenate([acc[:HEAD_DIM] / acc[HEAD_DIM:HEAD_DIM + 1]
                                for (_, acc) in state], axis=0).T

    for nchunk in range(1, nblk // FLASH_CHUNK + 1):
        @pl.when(qi // FLASH_CHUNK == nchunk - 1)
        def _(nchunk=nchunk):
            o_ref[...] = row_pass(nchunk)


def _moba_kernel(q_ref, k_ref, v_ref, bt_ref, o_ref,
                 kaug_s, vt_s, kmean_s, qa_s, ssc, psc, *, nblk, ntile):
    blk = MOBA_BLOCK
    qi = pl.program_id(2)
    spare = LANES - HEAD_DIM - nblk

    @pl.when(qi == 0)
    def _():
        lane = lax.broadcasted_iota(jnp.int32, (1, LANES), 1)
        for n in range(nblk):
            k = k_ref[n * blk:(n + 1) * blk, :]
            kaug_s[0, n * blk:(n + 1) * blk, :] = jnp.where(
                lane < HEAD_DIM, k, (lane == HEAD_DIM + n).astype(F32)).astype(BF16)
            kaug_s[1, n * blk:(n + 1) * blk, :] = jnp.where(
                lane >= HEAD_DIM, k, (lane == n).astype(F32)).astype(BF16)
            kmean_s[n:n + 1, :] = jnp.mean(k, axis=0, keepdims=True)
        _transpose_to(vt_s, v_ref)

    qt = q_ref[...].T
    own_rows = _head_rows()
    blk_id = lax.broadcasted_iota(jnp.int32, (nblk, 1), 0)
    blk_idf = blk_id.astype(F32)
    past = blk_id < qi
    for h in range(2):
        qh = jnp.where(own_rows[h], qt, 0.0)
        gate = jnp.dot(kmean_s[...], qh, precision=lax.Precision.HIGHEST,
                       preferred_element_type=F32)
        gate = jnp.where(past, gate, NEG)
        chosen = jnp.zeros(gate.shape, jnp.bool_)
        for _ in range(MOBA_TOPK):
            mx = jnp.max(gate, axis=0, keepdims=True)
            first_idx = jnp.min(jnp.where(gate == mx, blk_idf, float(nblk)), axis=0, keepdims=True)
            pick = blk_idf == first_idx
            chosen = jnp.logical_or(chosen, pick)
            gate = jnp.where(pick, -jnp.inf, gate)
        keep = jnp.logical_or(jnp.logical_and(chosen, past), blk_id == qi)
        selb = jnp.where(keep, 0.0, NEG)
        qs = qt[h * HEAD_DIM:(h + 1) * HEAD_DIM, :] * (ATTN_SCALE * LOG2E)
        pad = jnp.zeros((spare, blk), F32)
        parts = [qs, selb, pad] if h == 0 else [selb, pad, qs]
        qa_s[h] = jnp.concatenate(parts, axis=0).astype(BF16)

    _causal_rows(qi, blk, nblk, o_ref, kaug_s, vt_s, qa_s, ssc, psc,
                 bias_fn=lambda h, kj, s: s + bt_ref[h, jnp.clip(qi - kj, 0, ntile - 1)])


def _moba_attention(qkv, bt_b, bsz, seq, col0):
    npair = H_B // 2
    nblk = seq // MOBA_BLOCK
    ntile = bt_b.shape[1]
    assert nblk % 8 == 0 and nblk <= LANES - HEAD_DIM
    kern = functools.partial(_moba_kernel, nblk=nblk, ntile=ntile)
    blk = MOBA_BLOCK

    def col(off):
        return pl.BlockSpec((None, seq, LANES), lambda p, b, i, off=off: (b, 0, off + p))

    return pl.pallas_call(
        kern,
        grid=(npair, bsz, nblk),
        in_specs=[pl.BlockSpec((None, blk, LANES), lambda p, b, i: (b, i, col0 + p)),
                  col(col0 + npair), col(col0 + 2 * npair),
                  pl.BlockSpec((2, ntile, blk, blk), lambda p, b, i: (p, 0, 0, 0))],
        out_specs=pl.BlockSpec((None, blk, LANES), lambda p, b, i: (b, i, p)),
        out_shape=jax.ShapeDtypeStruct((bsz, seq, W_B), F32),
        scratch_shapes=[pltpu.VMEM((2, seq, LANES), BF16), pltpu.VMEM((2, VT_ROWS, seq), BF16),
                        pltpu.VMEM((nblk, LANES), F32), pltpu.VMEM((2, LANES, blk), BF16),
                        pltpu.VMEM((2, seq, blk), F32), pltpu.VMEM((2, seq, blk), BF16)],
        compiler_params=_cparams("parallel", "parallel", "arbitrary"),
    )(qkv, qkv, qkv, bt_b)


CUM_SPLIT = 3
CUM_CHUNK = 256
FOX_TILE = 256


def _fgate_kernel(f_ref, bf_ref, pm_ref, o_ref, cum_s):
    seq = f_ref.shape[0]
    ch = CUM_CHUNK
    i = lax.broadcasted_iota(jnp.int32, (ch, ch), 0)
    j = lax.broadcasted_iota(jnp.int32, (ch, ch), 1)
    lower = (j <= i).astype(F32)
    carry = jnp.zeros((1, LANES), F32)
    for c in range(seq // ch):
        z = f_ref[c * ch:(c + 1) * ch, :] + bf_ref[...]
        ls = jnp.minimum(z, 0.0) - jnp.log(1.0 + jnp.exp(-jnp.abs(z)))
        cum = jnp.dot(lower, ls, precision=lax.Precision.HIGHEST, preferred_element_type=F32) + carry
        cum_s[c * ch:(c + 1) * ch, :] = cum
        carry = cum[ch - 1:ch, :]
    rest = cum_s[...] * (-LOG2E)
    out = None
    for t in range(CUM_SPLIT):
        term = rest.astype(BF16)
        rest = rest - term.astype(F32)
        part = jnp.dot(term, pm_ref[t], preferred_element_type=F32)
        out = part if out is None else out + part
    o_ref[...] = out.astype(BF16)


def _forget_keys(f, b_f):
    bsz, seq, _ = f.shape
    npair = H_C // 2
    nblk = seq // FOX_TILE
    pm = np.zeros((CUM_SPLIT, LANES, npair * LANES), np.float32)
    for t in range(CUM_SPLIT):
        for p in range(npair):
            pm[t, 2 * p, p * LANES + HEAD_DIM + nblk + t] = 1.0
            pm[t, 2 * p + 1, p * LANES + nblk + t] = 1.0
    bf = jnp.pad(b_f, (0, LANES - b_f.shape[0])).reshape(1, LANES)
    return pl.pallas_call(
        _fgate_kernel,
        grid=(bsz,),
        in_specs=[pl.BlockSpec((None, seq, LANES), lambda b: (b, 0, 0)),
                  pl.BlockSpec((1, LANES), lambda b: (0, 0)),
                  pl.BlockSpec((CUM_SPLIT, LANES, npair * LANES), lambda b: (0, 0, 0))],
        out_specs=pl.BlockSpec((None, seq, npair * LANES), lambda b: (b, 0, 0)),
        out_shape=jax.ShapeDtypeStruct((bsz, seq, npair * LANES), BF16),
        scratch_shapes=[pltpu.VMEM((seq, LANES), F32)],
        compiler_params=_cparams("parallel"),
    )(f, bf, jnp.asarray(pm, BF16))


def _fox_kernel(q_ref, k_ref, v_ref, ca_ref, o_ref,
                kaug_s, vt_s, qa_s, cm_s, ssc, psc, *, tq, nblk):
    qi = pl.program_id(2)
    spare = LANES - HEAD_DIM - nblk - 8

    @pl.when(qi == 0)
    def _():
        lane = lax.broadcasted_iota(jnp.int32, (1, LANES), 1)
        for n in range(nblk):
            k = k_ref[n * tq:(n + 1) * tq, :]
            ca = ca_ref[n * tq:(n + 1) * tq, :].astype(F32)
            kaug_s[0, n * tq:(n + 1) * tq, :] = jnp.where(
                lane < HEAD_DIM, k, ca + (lane == HEAD_DIM + n).astype(F32)).astype(BF16)
            kaug_s[1, n * tq:(n + 1) * tq, :] = jnp.where(
                lane >= HEAD_DIM, k, ca + (lane == n).astype(F32)).astype(BF16)
        _transpose_to(vt_s, v_ref)
        key = lax.broadcasted_iota(jnp.int32, (tq, tq), 0)
        qry = lax.broadcasted_iota(jnp.int32, (tq, tq), 1)
        cm_s[0] = jnp.zeros((tq, tq), F32)
        cm_s[1] = jnp.where(key <= qry, 0.0, NEG)

    qt = q_ref[...].T * (ATTN_SCALE * LOG2E)
    blk_id = lax.broadcasted_iota(jnp.int32, (nblk, 1), 0)
    future = jnp.broadcast_to(jnp.where(blk_id <= qi, 0.0, NEG), (nblk, tq))
    sub = lax.broadcasted_iota(jnp.int32, (8, 1), 0)
    ones = jnp.broadcast_to(jnp.where(sub < CUM_SPLIT, 1.0, 0.0), (8, tq))
    pad = jnp.zeros((spare, tq), F32)
    for h in range(2):
        qs = qt[h * HEAD_DIM:(h + 1) * HEAD_DIM, :]
        parts = [qs, future, ones, pad] if h == 0 else [future, ones, pad, qs]
        qa_s[h] = jnp.concatenate(parts, axis=0).astype(BF16)

    _causal_rows(qi, tq, nblk, o_ref, kaug_s, vt_s, qa_s, ssc, psc,
                 bias_fn=lambda h, kj, s: s + cm_s[jnp.where(kj == qi, 1, 0)])


def _fox_attention(qkv, caug, bsz, seq, col0):
    npair = H_C // 2
    tq = FOX_TILE
    nblk = seq // tq
    assert nblk % 8 == 0 and nblk % FLASH_CHUNK == 0 and nblk + 8 <= LANES - HEAD_DIM
    kern = functools.partial(_fox_kernel, tq=tq, nblk=nblk)

    def col(off):
        return pl.BlockSpec((None, seq, LANES), lambda p, b, i, off=off: (b, 0, off + p))

    return pl.pallas_call(
        kern,
        grid=(npair, bsz, seq // tq),
        in_specs=[pl.BlockSpec((None, tq, LANES), lambda p, b, i: (b, i, col0 + p)),
                  col(col0 + npair), col(col0 + 2 * npair),
                  pl.BlockSpec((None, seq, LANES), lambda p, b, i: (b, 0, p))],
        out_specs=pl.BlockSpec((None, tq, LANES), lambda p, b, i: (b, i, p)),
        out_shape=jax.ShapeDtypeStruct((bsz, seq, W_C), F32),
        scratch_shapes=[pltpu.VMEM((2, seq, LANES), BF16), pltpu.VMEM((2, VT_ROWS, seq), BF16),
                        pltpu.VMEM((2, LANES, tq), BF16), pltpu.VMEM((2, tq, tq), F32),
                        pltpu.VMEM((2, seq, tq), F32), pltpu.VMEM((2, seq, tq), BF16)],
        compiler_params=_cparams("parallel", "parallel", "arbitrary"),
    )(qkv, qkv, qkv, caug)


def _layer_norm_rows(z, g, b):
    mu = jnp.mean(z, axis=-1, keepdims=True)
    zc = z - mu
    var = jnp.mean(zc * zc, axis=-1, keepdims=True)
    return zc * lax.rsqrt(var + LN_EPS) * g + b


def _mixout_kernel(oa_ref, ob_ref, oc_ref, ga_ref, gb_ref, gc_ref, wa_ref, wb_ref, wc_ref,
                   x_ref, g1_ref, lng_ref, lnb_ref, sc_ref, sh_ref, x_out, h_out, *, alpha):
    def normed(o_ref, g_ref):
        o = o_ref[...]
        ms = jnp.mean(o * o, axis=-1, keepdims=True)
        return (o * lax.rsqrt(ms + LN_EPS) * g_ref[...]).astype(BF16)

    y = jnp.dot(normed(oa_ref, ga_ref), wa_ref[...], preferred_element_type=F32)
    y = y + jnp.dot(normed(ob_ref, gb_ref), wb_ref[...], preferred_element_type=F32)
    y = y + jnp.dot(normed(oc_ref, gc_ref), wc_ref[...], preferred_element_type=F32)
    x1 = _layer_norm_rows(alpha * x_ref[...] + (1.0 + g1_ref[...]) * y, lng_ref[...], lnb_ref[...])
    x_out[...] = x1
    h_out[...] = (x1 * (1.0 + sc_ref[...]) + sh_ref[...]).astype(BF16)


def _mix_out(o_a, o_b, o_c, g_a, g_b, g_c, w_out_bf, x, g1, ln_g, ln_b, sc2, sh2, alpha):
    bsz, seq, d = x.shape
    tm = 256
    w_a, w_b, w_c = w_out_bf[:W_A], w_out_bf[W_A:W_A + W_B], w_out_bf[W_A + W_B:]

    def act(width):
        return pl.BlockSpec((None, tm, width), lambda b, i: (b, i, 0))

    def const(shape):
        return pl.BlockSpec(shape, lambda b, i: (0,) * len(shape))

    bvec = pl.BlockSpec((None, 1, d), lambda b, i: (b, 0, 0))
    return pl.pallas_call(
        functools.partial(_mixout_kernel, alpha=alpha),
        grid=(bsz, seq // tm),
        in_specs=[act(W_A), act(W_B), act(W_C),
                  const((1, W_A)), const((1, W_B)), const((1, W_C)),
                  const((W_A, d)), const((W_B, d)), const((W_C, d)),
                  act(d), bvec, const((1, d)), const((1, d)), bvec, bvec],
        out_specs=[act(d), act(d)],
        out_shape=[jax.ShapeDtypeStruct((bsz, seq, d), F32),
                   jax.ShapeDtypeStruct((bsz, seq, d), BF16)],
        compiler_params=_cparams("parallel", "parallel"),
    )(o_a, o_b, o_c, g_a.reshape(1, -1), g_b.reshape(1, -1), g_c.reshape(1, -1),
      w_a, w_b, w_c, x, g1, ln_g.reshape(1, d), ln_b.reshape(1, d), sc2, sh2)


FFN_HALO = 8
FFN_SUB = 256


def _ffn_up_kernel(h_ref, wa_ref, wb_ref, cwa_ref, cwb_ref, cba_ref, cbb_ref, o_ref, bufa, bufb):
    i = pl.program_id(2)
    tm, tn = o_ref.shape

    @pl.when(i == 0)
    def _():
        bufa[0:FFN_HALO, :] = jnp.zeros((FFN_HALO, tn), F32)
        bufb[0:FFN_HALO, :] = jnp.zeros((FFN_HALO, tn), F32)

    subs = [slice(c * FFN_SUB, (c + 1) * FFN_SUB) for c in range(tn // FFN_SUB)]
    h = h_ref[...]
    prods = [(jnp.dot(h, wa_ref[:, cs], preferred_element_type=F32),
              jnp.dot(h, wb_ref[:, cs], preferred_element_type=F32)) for cs in subs]

    def conv(u, cw_ref, cb_ref, buf, cs):
        buf[FFN_HALO:FFN_HALO + tm, cs] = u
        out = (cw_ref[2:3, cs] * u
               + cw_ref[1:2, cs] * buf[FFN_HALO - 1:FFN_HALO - 1 + tm, cs]
               + cw_ref[0:1, cs] * buf[FFN_HALO - 2:FFN_HALO - 2 + tm, cs]
               + cb_ref[:, cs])
        buf[0:FFN_HALO, cs] = u[tm - FFN_HALO:tm, :]
        return out

    for cs, (ua, ub) in zip(subs, prods):
        a = conv(ua, cwa_ref, cba_ref, bufa, cs)
        b = conv(ub, cwb_ref, cbb_ref, bufb, cs)
        o_ref[:, cs] = (a * (1.0 / (1.0 + jnp.exp(-a))) * b).astype(o_ref.dtype)


def _ffn_up(h, w_up_bf, conv_w, conv_b):
    bsz, seq, d = h.shape
    dff = w_up_bf.shape[1] // 2
    tm, tn = 512, 512
    nj = dff // tn
    cb = conv_b.reshape(1, -1)
    return pl.pallas_call(
        _ffn_up_kernel,
        grid=(nj, bsz, seq // tm),
        in_specs=[pl.BlockSpec((None, tm, d), lambda j, b, i: (b, i, 0)),
                  pl.BlockSpec((d, tn), lambda j, b, i: (0, j)),
                  pl.BlockSpec((d, tn), lambda j, b, i: (0, nj + j)),
                  pl.BlockSpec((CONV_WIDTH, tn), lambda j, b, i: (0, j)),
                  pl.BlockSpec((CONV_WIDTH, tn), lambda j, b, i: (0, nj + j)),
                  pl.BlockSpec((1, tn), lambda j, b, i: (0, j)),
                  pl.BlockSpec((1, tn), lambda j, b, i: (0, nj + j))],
        out_specs=pl.BlockSpec((None, tm, tn), lambda j, b, i: (b, i, j)),
        out_shape=jax.ShapeDtypeStruct((bsz, seq, dff), BF16),
        scratch_shapes=[pltpu.VMEM((tm + FFN_HALO, tn), F32)] * 2,
        compiler_params=_cparams("parallel", "parallel", "arbitrary"),
    )(h, w_up_bf, w_up_bf, conv_w, conv_w, cb, cb)


def _ffn_down_kernel(g_ref, w_ref, x_ref, g2_ref, lng_ref, lnb_ref, sc_ref, sh_ref,
                     x_out, *maybe_h_out, alpha):
    y = jnp.dot(g_ref[...], w_ref[...], preferred_element_type=F32)
    x2 = _layer_norm_rows(alpha * x_ref[...] + (1.0 + g2_ref[...]) * y, lng_ref[...], lnb_ref[...])
    x_out[...] = x2
    for h_out in maybe_h_out:
        h_out[...] = (x2 * (1.0 + sc_ref[...]) + sh_ref[...]).astype(BF16)


def _ffn_down(g, w_down_bf, x, g2, ln_g, ln_b, sc_next, sh_next, alpha, with_h):
    bsz, seq, d = x.shape
    dff = g.shape[2]
    tm = 256
    act = pl.BlockSpec((None, tm, d), lambda b, i: (b, i, 0))
    bvec = pl.BlockSpec((None, 1, d), lambda b, i: (b, 0, 0))
    const = pl.BlockSpec((1, d), lambda b, i: (0, 0))
    out_specs = [act, act] if with_h else [act]
    out_shape = [jax.ShapeDtypeStruct((bsz, seq, d), F32)]
    if with_h:
        out_shape.append(jax.ShapeDtypeStruct((bsz, seq, d), BF16))
    return pl.pallas_call(
        functools.partial(_ffn_down_kernel, alpha=alpha),
        grid=(bsz, seq // tm),
        in_specs=[pl.BlockSpec((None, tm, dff), lambda b, i: (b, i, 0)),
                  pl.BlockSpec((dff, d), lambda b, i: (0, 0), pipeline_mode=pl.Buffered(1)),
                  act, bvec, const, const, bvec, bvec],
        out_specs=out_specs,
        out_shape=out_shape,
        compiler_params=_cparams("parallel", "parallel"),
    )(g, w_down_bf, x, g2, ln_g.reshape(1, d), ln_b.reshape(1, d), sc_next, sh_next)


def kernel(x, c, rel_bias, w_ada, b_ada, w_in, b_f, g_mix_a, g_mix_b, g_mix_c, w_out,
           ln1_g, ln1_b, w_up, conv_w, conv_b, w_down, ln2_g, ln2_b):
    bsz, seq, d = x.shape
    depth = w_ada.shape[0]
    alpha = (2 * depth) ** 0.25
    n_qkv = 3 * (W_A + W_B + W_C)

    mod = _ada_mod(c, w_ada, b_ada)

    def mod_vec(l, idx):
        return mod[l, :, idx * d:(idx + 1) * d].reshape(bsz, 1, d)

    bm_a = _dilated_bias_tiles(rel_bias[:, :H_A])
    bt_b = _moba_bias_tiles(rel_bias[:, H_A:], seq)

    h = _modulate(x, mod_vec(0, 1), mod_vec(0, 0))
    for l in range(depth):
        w_in_bf = w_in[l].astype(BF16)
        w_f = jnp.pad(w_in_bf[:, n_qkv:], ((0, 0), (0, LANES - H_C)))
        qkv, f = _inproj(h.reshape(bsz * seq, d), w_in_bf[:, :n_qkv], w_f)
        qkv = qkv.reshape(bsz, seq, n_qkv)
        caug = _forget_keys(f.reshape(bsz, seq, LANES), b_f[l])

        o_a = _dilated_attention(qkv, bm_a, bsz, seq)
        o_b = _moba_attention(qkv, bt_b, bsz, seq, 3 * W_A // LANES)
        o_c = _fox_attention(qkv, caug, bsz, seq, 3 * (W_A + W_B) // LANES)

        x, h = _mix_out(o_a, o_b, o_c, g_mix_a[l], g_mix_b[l], g_mix_c[l], w_out[l].astype(BF16),
                        x, mod_vec(l, 2), ln1_g[l], ln1_b[l], mod_vec(l, 4), mod_vec(l, 3), alpha)
        g = _ffn_up(h, w_up[l].astype(BF16), conv_w[l], conv_b[l])
        last = l == depth - 1
        nxt = min(l + 1, depth - 1)
        outs = _ffn_down(g, w_down[l].astype(BF16), x, mod_vec(l, 5), ln2_g[l], ln2_b[l],
                         mod_vec(nxt, 1), mod_vec(nxt, 0), alpha, with_h=not last)
        x = outs[0]
        if not last:
            h = outs[1]
    return x
```

```python
import functools
import math

import numpy as np
import jax
import jax.numpy as jnp
from jax import lax
from jax.experimental import pallas as pl
from jax.experimental.pallas import tpu as pltpu

F32 = jnp.float32
BF16 = jnp.bfloat16

HEAD_DIM = 64
LANES = 128
H_A, H_B, H_C = 12, 10, 10
W_A, W_B, W_C = H_A * HEAD_DIM, H_B * HEAD_DIM, H_C * HEAD_DIM
DIL_PATTERNS = ((128, 1), (512, 4), (2048, 16))
DIL_W = 128
MOBA_BLOCK = 256
MOBA_TOPK = 3
N_BUCKETS = 32
REL_MAX_DIST = 2048
CONV_WIDTH = 3
LN_EPS = 1e-5
NEG = -1e30
ATTN_SCALE = HEAD_DIM ** -0.5
LOG2E = math.log2(math.e)
VMEM_LIMIT = 56 * 1024 * 1024


def _cparams(*sem):
    return pltpu.CompilerParams(dimension_semantics=sem, vmem_limit_bytes=VMEM_LIMIT)


def _t5_bucket(dist):
    n = jnp.maximum(dist, 0)
    max_exact = N_BUCKETS // 2
    nf = jnp.maximum(n, 1).astype(F32)
    large = max_exact + (jnp.log(nf / max_exact) / math.log(REL_MAX_DIST / max_exact)
                         * (N_BUCKETS - max_exact)).astype(jnp.int32)
    large = jnp.minimum(large, N_BUCKETS - 1)
    return jnp.where(n < max_exact, n, large)


def _ada_kernel(c_ref, w_ref, b_ref, o_ref):
    c = c_ref[...]
    cond = c * (1.0 / (1.0 + jnp.exp(-c)))
    o_ref[...] = jnp.dot(cond, w_ref[...], precision=lax.Precision.HIGHEST,
                         preferred_element_type=F32) + b_ref[...]


def _ada_mod(c, w_ada, b_ada):
    depth, d, n = w_ada.shape
    bsz = c.shape[0]
    tn = 1024
    return pl.pallas_call(
        _ada_kernel,
        grid=(depth, n // tn),
        in_specs=[pl.BlockSpec((bsz, d), lambda l, j: (0, 0)),
                  pl.BlockSpec((None, d, tn), lambda l, j: (l, 0, j)),
                  pl.BlockSpec((None, 1, tn), lambda l, j: (l, 0, j))],
        out_specs=pl.BlockSpec((None, bsz, tn), lambda l, j: (l, 0, j)),
        out_shape=jax.ShapeDtypeStruct((depth, bsz, n), F32),
        compiler_params=_cparams("parallel", "parallel"),
    )(c, w_ada, b_ada.reshape(depth, 1, n))


def _modulate_kernel(x_ref, sc_ref, sh_ref, o_ref):
    o_ref[...] = (x_ref[...] * (1.0 + sc_ref[...]) + sh_ref[...]).astype(o_ref.dtype)


def _modulate(x, sc, sh):
    bsz, s, d = x.shape
    tm = 512
    vec = pl.BlockSpec((None, 1, d), lambda b, i: (b, 0, 0))
    return pl.pallas_call(
        _modulate_kernel,
        grid=(bsz, s // tm),
        in_specs=[pl.BlockSpec((None, tm, d), lambda b, i: (b, i, 0)), vec, vec],
        out_specs=pl.BlockSpec((None, tm, d), lambda b, i: (b, i, 0)),
        out_shape=jax.ShapeDtypeStruct((bsz, s, d), BF16),
        compiler_params=_cparams("parallel", "parallel"),
    )(x, sc, sh)


def _inproj_kernel(a_ref, w_ref, wf_ref, o_ref, f_ref):
    a = a_ref[...]
    o_ref[...] = jnp.dot(a, w_ref[...], preferred_element_type=F32)

    @pl.when(pl.program_id(1) == 0)
    def _():
        f_ref[...] = jnp.dot(a, wf_ref[...], preferred_element_type=F32)


def _inproj(h, w_in_bf, layer, n, w_f):
    t, d = h.shape
    tm, tn = 1024, 1024
    return pl.pallas_call(
        _inproj_kernel,
        grid=(t // tm, n // tn),
        in_specs=[pl.BlockSpec((tm, d), lambda i, j: (i, 0)),
                  pl.BlockSpec((None, d, tn), lambda i, j: (layer, 0, j)),
                  pl.BlockSpec((d, LANES), lambda i, j: (0, 0))],
        out_specs=[pl.BlockSpec((tm, tn), lambda i, j: (i, j)),
                   pl.BlockSpec((tm, LANES), lambda i, j: (i, 0))],
        out_shape=[jax.ShapeDtypeStruct((t, n), F32),
                   jax.ShapeDtypeStruct((t, LANES), F32)],
        compiler_params=_cparams("parallel", "arbitrary"),
    )(h, w_in_bf, w_f)


TOEP_LEN = 512
TOEP_ROWS = 256


def _toeplitz_kernel(rr_ref, o_ref):
    x = jnp.broadcast_to(rr_ref[...], (TOEP_ROWS, TOEP_LEN))
    y = pltpu.roll(x, TOEP_LEN - TOEP_ROWS + 1, 1, stride=1, stride_axis=0)
    o_ref[...] = y[:, :o_ref.shape[1]]


def _toeplitz(rr, cols):
    g = rr.shape[0]
    return pl.pallas_call(
        _toeplitz_kernel,
        grid=(g,),
        in_specs=[pl.BlockSpec((None, 1, TOEP_LEN), lambda i: (i, 0, 0))],
        out_specs=pl.BlockSpec((None, TOEP_ROWS, cols), lambda i: (i, 0, 0)),
        out_shape=jax.ShapeDtypeStruct((g, TOEP_ROWS, cols), F32),
        compiler_params=_cparams("parallel"),
    )(rr.reshape(g, 1, TOEP_LEN))


def _bias_of_distance(rel, dist, valid):
    bias = jnp.moveaxis(rel[_t5_bucket(dist)], -1, 0)
    return jnp.where(valid, bias * LOG2E, NEG).astype(F32)


def _dilated_bias_tiles(rel_a):
    c = jnp.arange(TOEP_LEN)
    rows = []
    for (_, dil) in DIL_PATTERNS:
        d0 = c - (TOEP_ROWS - 1) + DIL_W
        d1 = c - (TOEP_ROWS - 1)
        rows.append(jnp.stack([_bias_of_distance(rel_a, d0 * dil, (d0 >= 0) & (d0 <= DIL_W)),
                               _bias_of_distance(rel_a, d1 * dil, d1 >= 0)]))
    rr = jnp.stack(rows)
    out = _toeplitz(rr.reshape(-1, TOEP_LEN), DIL_W)
    return out.reshape(len(DIL_PATTERNS), 2, H_A, TOEP_ROWS, DIL_W)


def _moba_num_tiles(seq):
    blk = MOBA_BLOCK
    d_all = np.arange(max(seq, 2))
    nf = np.maximum(d_all, 1).astype(np.float32)
    half = N_BUCKETS // 2
    big = half + (np.log(nf / half) / math.log(REL_MAX_DIST / half) * (N_BUCKETS - half)).astype(np.int32)
    bucket = np.where(d_all < half, d_all, np.minimum(big, N_BUCKETS - 1))
    last_change = int(np.max(np.nonzero(np.diff(bucket))[0])) + 1
    first_const = min(seq // blk - 1, -(-(last_change + 2 + blk - 1) // blk))
    return first_const + 1


def _moba_bias_tiles(rel_b, seq):
    ntile = _moba_num_tiles(seq)
    c = jnp.arange(TOEP_LEN)[None, :]
    d = jnp.arange(ntile)[:, None] * MOBA_BLOCK + c - (TOEP_ROWS - 1)
    rr = _bias_of_distance(rel_b, d, d >= 0)
    out = _toeplitz(rr.reshape(-1, TOEP_LEN), MOBA_BLOCK)
    return out.reshape(H_B, ntile, TOEP_ROWS, MOBA_BLOCK)


def _head_rows():
    row = lax.broadcasted_iota(jnp.int32, (LANES, 1), 0)
    return row < HEAD_DIM, row >= HEAD_DIM


DIL_UNROLL = 4


def _dilated_kernel(q_ref, k_ref, v_ref, bm_ref, o_ref, og_s, lse_s, sblk_s, pblk_s, vts_s, *, seq):
    w = DIL_W
    own_rows = _head_rows()
    for g, (_, dil) in enumerate(DIL_PATTERNS):
        nb = (seq // dil) // w

        def rows(start, size, dil=dil):
            return pl.ds(start, size) if dil == 1 else pl.ds(start, size, stride=dil)

        def trip(it, carry, g=g, dil=dil, nb=nb, rows=rows):
            metas = []
            for u in range(DIL_UNROLL):
                idx = it * DIL_UNROLL + u
                r = idx // nb
                n = idx % nb
                q_rows = rows(n * w * dil + r, w)
                kv_rows = rows(jnp.maximum(n - 1, 0) * w * dil + r, 2 * w)
                var = jnp.where(n == 0, 1, 0)
                qt = q_ref[q_rows, :].T * (ATTN_SCALE * LOG2E)
                k = k_ref[kv_rows, :].astype(BF16)
                vt = v_ref[kv_rows, :].T
                tops = []
                for h in range(2):
                    vts_s[u, h] = jnp.concatenate(
                        [vt[h * HEAD_DIM:(h + 1) * HEAD_DIM, :], _ones_rows(2 * w)], axis=0).astype(BF16)
                    s = jnp.dot(k, jnp.where(own_rows[h], qt, 0.0).astype(BF16),
                                preferred_element_type=F32) + bm_ref[g, var, h]
                    sblk_s[u, h] = s
                    tops.append(jnp.max(s, axis=0, keepdims=True))
                metas.append((q_rows, tops))
            pvs = []
            for u, (_, tops) in enumerate(metas):
                for h in range(2):
                    pblk_s[u, h] = jnp.exp2(sblk_s[u, h] - tops[h]).astype(BF16)
                    pvs.append(jnp.dot(vts_s[u, h], pblk_s[u, h], preferred_element_type=F32))
            for u, (q_rows, tops) in enumerate(metas):
                pair = pvs[2 * u:2 * u + 2]
                o_t = jnp.concatenate([pv[:HEAD_DIM] / pv[HEAD_DIM:HEAD_DIM + 1] for pv in pair], axis=0)
                lse_t = jnp.concatenate(
                    [jnp.broadcast_to(m + jnp.log(pv[HEAD_DIM:HEAD_DIM + 1]) * LOG2E, (HEAD_DIM, w))
                     for m, pv in zip(tops, pair)], axis=0)
                og_s[g, q_rows, :] = o_t.T
                lse_s[g, q_rows, :] = lse_t.T
            return carry

        lax.fori_loop(0, nb * dil // DIL_UNROLL, trip, 0)

    top = jnp.maximum(jnp.maximum(lse_s[0], lse_s[1]), lse_s[2])
    num = jnp.zeros_like(top)
    den = jnp.zeros_like(top)
    for g in range(len(DIL_PATTERNS)):
        wg = jnp.exp2(lse_s[g] - top)
        num = num + wg * og_s[g]
        den = den + wg
    o_ref[...] = num / den


def _dilated_attention(qkv, bm_a, bsz, seq):
    npair = H_A // 2
    npat = len(DIL_PATTERNS)
    kern = functools.partial(_dilated_kernel, seq=seq)

    def col(off):
        return pl.BlockSpec((None, seq, LANES), lambda p, b, off=off: (b, 0, off + p))

    return pl.pallas_call(
        kern,
        grid=(npair, bsz),
        in_specs=[col(0), col(npair), col(2 * npair),
                  pl.BlockSpec((npat, 2, 2, 2 * DIL_W, DIL_W), lambda p, b: (0, 0, p, 0, 0))],
        out_specs=pl.BlockSpec((None, seq, LANES), lambda p, b: (b, 0, p)),
        out_shape=jax.ShapeDtypeStruct((bsz, seq, W_A), F32),
        scratch_shapes=[pltpu.VMEM((npat, seq, LANES), F32)] * 2 + [
            pltpu.VMEM((DIL_UNROLL, 2, 2 * DIL_W, DIL_W), F32),
            pltpu.VMEM((DIL_UNROLL, 2, 2 * DIL_W, DIL_W), BF16),
            pltpu.VMEM((DIL_UNROLL, 2, VT_ROWS, 2 * DIL_W), BF16)],
        compiler_params=_cparams("parallel", "parallel"),
    )(qkv, qkv, qkv, bm_a)


VT_ROWS = HEAD_DIM + 16


def _ones_rows(width):
    row = lax.broadcasted_iota(jnp.int32, (VT_ROWS - HEAD_DIM, width), 0)
    return jnp.where(row == 0, 1.0, 0.0)


def _transpose_to(dst_ref, src_ref, chunk=512):
    for c in range(src_ref.shape[0] // chunk):
        cols = slice(c * chunk, (c + 1) * chunk)
        t = src_ref[cols, :].T
        for h in range(2):
            dst_ref[h, :, cols] = jnp.concatenate(
                [t[h * HEAD_DIM:(h + 1) * HEAD_DIM, :], _ones_rows(chunk)], axis=0).astype(BF16)


FLASH_CHUNK = 2


def _causal_rows(qi, tq, nblk, o_ref, kaug_s, vt_s, qa_s, ssc, psc, bias_fn):
    ck = FLASH_CHUNK * tq

    def tile(kj):
        return slice(kj * tq, (kj + 1) * tq)

    def scores(c, h, maybe_diag):
        top = None
        for kj in range(c * FLASH_CHUNK, (c + 1) * FLASH_CHUNK):
            s = jnp.dot(kaug_s[h, tile(kj), :], qa_s[h], preferred_element_type=F32)
            s = bias_fn(h, kj, s, maybe_diag)
            ssc[h, tile(kj), :] = s
            cmax = jnp.max(s, axis=0, keepdims=True)
            top = cmax if top is None else jnp.maximum(top, cmax)
        return top

    def values(c, h, top, state):
        m, acc = state
        m_new = jnp.maximum(m, top)
        for kj in range(c * FLASH_CHUNK, (c + 1) * FLASH_CHUNK):
            psc[h, tile(kj), :] = jnp.exp2(ssc[h, tile(kj), :] - m_new).astype(BF16)
        pv = jnp.dot(vt_s[h, :, c * ck:(c + 1) * ck], psc[h, c * ck:(c + 1) * ck, :],
                     preferred_element_type=F32)
        return m_new, jnp.exp2(m - m_new) * acc + pv

    def row_pass(nchunk):
        tops = {(0, h): scores(0, h, nchunk == 1) for h in range(2)}
        state = [(jnp.full((1, tq), NEG, F32), jnp.zeros((VT_ROWS, tq), F32)) for _ in range(2)]
        for c in range(nchunk):
            if c + 1 < nchunk:
                for h in range(2):
                    tops[(c + 1, h)] = scores(c + 1, h, c + 2 == nchunk)
            for h in range(2):
                state[h] = values(c, h, tops[(c, h)], state[h])
        return jnp.concatenate([acc[:HEAD_DIM] / acc[HEAD_DIM:HEAD_DIM + 1]
                                for (_, acc) in state], axis=0).T

    for nchunk in range(1, nblk // FLASH_CHUNK + 1):
        @pl.when(qi // FLASH_CHUNK == nchunk - 1)
        def _(nchunk=nchunk):
            o_ref[...] = row_pass(nchunk)


def _moba_kernel(q_ref, k_ref, v_ref, bt_ref, o_ref,
                 kaug_s, vt_s, kmean_s, qa_s, ssc, psc, *, nblk, ntile):
    blk = MOBA_BLOCK
    qi = pl.program_id(2)
    spare = LANES - HEAD_DIM - nblk

    @pl.when(qi == 0)
    def _():
        lane = lax.broadcasted_iota(jnp.int32, (1, LANES), 1)
        for n in range(nblk):
            k = k_ref[n * blk:(n + 1) * blk, :]
            kaug_s[0, n * blk:(n + 1) * blk, :] = jnp.where(
                lane < HEAD_DIM, k, (lane == HEAD_DIM + n).astype(F32)).astype(BF16)
            kaug_s[1, n * blk:(n + 1) * blk, :] = jnp.where(
                lane >= HEAD_DIM, k, (lane == n).astype(F32)).astype(BF16)
            kmean_s[n:n + 1, :] = jnp.mean(k, axis=0, keepdims=True)
        _transpose_to(vt_s, v_ref)

    qt = q_ref[...].T
    own_rows = _head_rows()
    blk_id = lax.broadcasted_iota(jnp.int32, (nblk, 1), 0)
    blk_idf = blk_id.astype(F32)
    past = blk_id < qi
    for h in range(2):
        qh = jnp.where(own_rows[h], qt, 0.0)
        gate = jnp.dot(kmean_s[...], qh, precision=lax.Precision.HIGHEST,
                       preferred_element_type=F32)
        gate = jnp.where(past, gate, NEG)
        chosen = jnp.zeros(gate.shape, jnp.bool_)
        for _ in range(MOBA_TOPK):
            mx = jnp.max(gate, axis=0, keepdims=True)
            first_idx = jnp.min(jnp.where(gate == mx, blk_idf, float(nblk)), axis=0, keepdims=True)
            pick = blk_idf == first_idx
            chosen = jnp.logical_or(chosen, pick)
            gate = jnp.where(pick, -jnp.inf, gate)
        keep = jnp.logical_or(jnp.logical_and(chosen, past), blk_id == qi)
        selb = jnp.where(keep, 0.0, NEG)
        qs = qt[h * HEAD_DIM:(h + 1) * HEAD_DIM, :] * (ATTN_SCALE * LOG2E)
        pad = jnp.zeros((spare, blk), F32)
        parts = [qs, selb, pad] if h == 0 else [selb, pad, qs]
        qa_s[h] = jnp.concatenate(parts, axis=0).astype(BF16)

    _causal_rows(qi, blk, nblk, o_ref, kaug_s, vt_s, qa_s, ssc, psc,
                 bias_fn=lambda h, kj, s, _: s + bt_ref[h, jnp.clip(qi - kj, 0, ntile - 1)])


def _moba_attention(qkv, bt_b, bsz, seq, col0):
    npair = H_B // 2
    nblk = seq // MOBA_BLOCK
    ntile = bt_b.shape[1]
    assert nblk % 8 == 0 and nblk <= LANES - HEAD_DIM
    kern = functools.partial(_moba_kernel, nblk=nblk, ntile=ntile)
    blk = MOBA_BLOCK

    def col(off):
        return pl.BlockSpec((None, seq, LANES), lambda p, b, i, off=off: (b, 0, off + p))

    return pl.pallas_call(
        kern,
        grid=(npair, bsz, nblk),
        in_specs=[pl.BlockSpec((None, blk, LANES), lambda p, b, i: (b, i, col0 + p)),
                  col(col0 + npair), col(col0 + 2 * npair),
                  pl.BlockSpec((2, ntile, blk, blk), lambda p, b, i: (p, 0, 0, 0))],
        out_specs=pl.BlockSpec((None, blk, LANES), lambda p, b, i: (b, i, p)),
        out_shape=jax.ShapeDtypeStruct((bsz, seq, W_B), F32),
        scratch_shapes=[pltpu.VMEM((2, seq, LANES), BF16), pltpu.VMEM((2, VT_ROWS, seq), BF16),
                        pltpu.VMEM((nblk, LANES), F32), pltpu.VMEM((2, LANES, blk), BF16),
                        pltpu.VMEM((2, seq, blk), F32), pltpu.VMEM((2, seq, blk), BF16)],
        compiler_params=_cparams("parallel", "parallel", "arbitrary"),
    )(qkv, qkv, qkv, bt_b)


CUM_SPLIT = 3
CUM_CHUNK = 256
FOX_TILE = 256


def _fgate_kernel(f_ref, bf_ref, pm_ref, o_ref, cum_s):
    seq = f_ref.shape[0]
    ch = CUM_CHUNK
    i = lax.broadcasted_iota(jnp.int32, (ch, ch), 0)
    j = lax.broadcasted_iota(jnp.int32, (ch, ch), 1)
    lower = (j <= i).astype(F32)
    carry = jnp.zeros((1, LANES), F32)
    for c in range(seq // ch):
        z = f_ref[c * ch:(c + 1) * ch, :] + bf_ref[...]
        ls = jnp.minimum(z, 0.0) - jnp.log(1.0 + jnp.exp(-jnp.abs(z)))
        cum = jnp.dot(lower, ls, precision=lax.Precision.HIGHEST, preferred_element_type=F32) + carry
        cum_s[c * ch:(c + 1) * ch, :] = cum
        carry = cum[ch - 1:ch, :]
    rest = cum_s[...] * (-LOG2E)
    out = None
    for t in range(CUM_SPLIT):
        term = rest.astype(BF16)
        rest = rest - term.astype(F32)
        part = jnp.dot(term, pm_ref[t], preferred_element_type=F32)
        out = part if out is None else out + part
    o_ref[...] = out.astype(BF16)


def _forget_keys(f, b_f):
    bsz, seq, _ = f.shape
    npair = H_C // 2
    nblk = seq // FOX_TILE
    pm = np.zeros((CUM_SPLIT, LANES, npair * LANES), np.float32)
    for t in range(CUM_SPLIT):
        for p in range(npair):
            pm[t, 2 * p, p * LANES + HEAD_DIM + nblk + t] = 1.0
            pm[t, 2 * p + 1, p * LANES + nblk + t] = 1.0
    bf = jnp.pad(b_f, (0, LANES - b_f.shape[0])).reshape(1, LANES)
    return pl.pallas_call(
        _fgate_kernel,
        grid=(bsz,),
        in_specs=[pl.BlockSpec((None, seq, LANES), lambda b: (b, 0, 0)),
                  pl.BlockSpec((1, LANES), lambda b: (0, 0)),
                  pl.BlockSpec((CUM_SPLIT, LANES, npair * LANES), lambda b: (0, 0, 0))],
        out_specs=pl.BlockSpec((None, seq, npair * LANES), lambda b: (b, 0, 0)),
        out_shape=jax.ShapeDtypeStruct((bsz, seq, npair * LANES), BF16),
        scratch_shapes=[pltpu.VMEM((seq, LANES), F32)],
        compiler_params=_cparams("parallel"),
    )(f, bf, jnp.asarray(pm, BF16))


def _fox_kernel(q_ref, k_ref, v_ref, ca_ref, o_ref,
                kaug_s, vt_s, qa_s, cm_s, ssc, psc, *, tq, nblk):
    qi = pl.program_id(2)
    spare = LANES - HEAD_DIM - nblk - 8

    @pl.when(qi == 0)
    def _():
        lane = lax.broadcasted_iota(jnp.int32, (1, LANES), 1)
        for n in range(nblk):
            k = k_ref[n * tq:(n + 1) * tq, :]
            ca = ca_ref[n * tq:(n + 1) * tq, :].astype(F32)
            kaug_s[0, n * tq:(n + 1) * tq, :] = jnp.where(
                lane < HEAD_DIM, k, ca + (lane == HEAD_DIM + n).astype(F32)).astype(BF16)
            kaug_s[1, n * tq:(n + 1) * tq, :] = jnp.where(
                lane >= HEAD_DIM, k, ca + (lane == n).astype(F32)).astype(BF16)
        _transpose_to(vt_s, v_ref)
        key = lax.broadcasted_iota(jnp.int32, (tq, tq), 0)
        qry = lax.broadcasted_iota(jnp.int32, (tq, tq), 1)
        cm_s[0] = jnp.zeros((tq, tq), F32)
        cm_s[1] = jnp.where(key <= qry, 0.0, NEG)

    qt = q_ref[...].T * (ATTN_SCALE * LOG2E)
    blk_id = lax.broadcasted_iota(jnp.int32, (nblk, 1), 0)
    future = jnp.broadcast_to(jnp.where(blk_id <= qi, 0.0, NEG), (nblk, tq))
    sub = lax.broadcasted_iota(jnp.int32, (8, 1), 0)
    ones = jnp.broadcast_to(jnp.where(sub < CUM_SPLIT, 1.0, 0.0), (8, tq))
    pad = jnp.zeros((spare, tq), F32)
    for h in range(2):
        qs = qt[h * HEAD_DIM:(h + 1) * HEAD_DIM, :]
        parts = [qs, future, ones, pad] if h == 0 else [future, ones, pad, qs]
        qa_s[h] = jnp.concatenate(parts, axis=0).astype(BF16)

    _causal_rows(qi, tq, nblk, o_ref, kaug_s, vt_s, qa_s, ssc, psc,
                 bias_fn=lambda h, kj, s, maybe_diag: (
                     s + cm_s[jnp.where(kj == qi, 1, 0)] if maybe_diag else s))


def _fox_attention(qkv, caug, bsz, seq, col0):
    npair = H_C // 2
    tq = FOX_TILE
    nblk = seq // tq
    assert nblk % 8 == 0 and nblk % FLASH_CHUNK == 0 and nblk + 8 <= LANES - HEAD_DIM
    kern = functools.partial(_fox_kernel, tq=tq, nblk=nblk)

    def col(off):
        return pl.BlockSpec((None, seq, LANES), lambda p, b, i, off=off: (b, 0, off + p))

    return pl.pallas_call(
        kern,
        grid=(npair, bsz, seq // tq),
        in_specs=[pl.BlockSpec((None, tq, LANES), lambda p, b, i: (b, i, col0 + p)),
                  col(col0 + npair), col(col0 + 2 * npair),
                  pl.BlockSpec((None, seq, LANES), lambda p, b, i: (b, 0, p))],
        out_specs=pl.BlockSpec((None, tq, LANES), lambda p, b, i: (b, i, p)),
        out_shape=jax.ShapeDtypeStruct((bsz, seq, W_C), F32),
        scratch_shapes=[pltpu.VMEM((2, seq, LANES), BF16), pltpu.VMEM((2, VT_ROWS, seq), BF16),
                        pltpu.VMEM((2, LANES, tq), BF16), pltpu.VMEM((2, tq, tq), F32),
                        pltpu.VMEM((2, seq, tq), F32), pltpu.VMEM((2, seq, tq), BF16)],
        compiler_params=_cparams("parallel", "parallel", "arbitrary"),
    )(qkv, qkv, qkv, caug)


def _layer_norm_rows(z, g, b):
    mu = jnp.mean(z, axis=-1, keepdims=True)
    zc = z - mu
    var = jnp.mean(zc * zc, axis=-1, keepdims=True)
    return zc * lax.rsqrt(var + LN_EPS) * g + b


def _mixout_kernel(oa_ref, ob_ref, oc_ref, ga_ref, gb_ref, gc_ref, wa_ref, wb_ref, wc_ref,
                   x_ref, g1_ref, lng_ref, lnb_ref, sc_ref, sh_ref, x_out, h_out, *, alpha):
    def normed(o_ref, g_ref):
        o = o_ref[...]
        ms = jnp.mean(o * o, axis=-1, keepdims=True)
        return (o * lax.rsqrt(ms + LN_EPS) * g_ref[...]).astype(BF16)

    y = jnp.dot(normed(oa_ref, ga_ref), wa_ref[...], preferred_element_type=F32)
    y = y + jnp.dot(normed(ob_ref, gb_ref), wb_ref[...], preferred_element_type=F32)
    y = y + jnp.dot(normed(oc_ref, gc_ref), wc_ref[...], preferred_element_type=F32)
    x1 = _layer_norm_rows(alpha * x_ref[...] + (1.0 + g1_ref[...]) * y, lng_ref[...], lnb_ref[...])
    x_out[...] = x1
    h_out[...] = (x1 * (1.0 + sc_ref[...]) + sh_ref[...]).astype(BF16)


def _mix_out(o_a, o_b, o_c, g_a, g_b, g_c, w_out_bf, x, g1, ln_g, ln_b, sc2, sh2, alpha):
    bsz, seq, d = x.shape
    tm = 256
    w_a, w_b, w_c = w_out_bf[:W_A], w_out_bf[W_A:W_A + W_B], w_out_bf[W_A + W_B:]

    def act(width):
        return pl.BlockSpec((None, tm, width), lambda b, i: (b, i, 0))

    def const(shape):
        return pl.BlockSpec(shape, lambda b, i: (0,) * len(shape))

    bvec = pl.BlockSpec((None, 1, d), lambda b, i: (b, 0, 0))
    return pl.pallas_call(
        functools.partial(_mixout_kernel, alpha=alpha),
        grid=(bsz, seq // tm),
        in_specs=[act(W_A), act(W_B), act(W_C),
                  const((1, W_A)), const((1, W_B)), const((1, W_C)),
                  const((W_A, d)), const((W_B, d)), const((W_C, d)),
                  act(d), bvec, const((1, d)), const((1, d)), bvec, bvec],
        out_specs=[act(d), act(d)],
        out_shape=[jax.ShapeDtypeStruct((bsz, seq, d), F32),
                   jax.ShapeDtypeStruct((bsz, seq, d), BF16)],
        compiler_params=_cparams("parallel", "parallel"),
    )(o_a, o_b, o_c, g_a.reshape(1, -1), g_b.reshape(1, -1), g_c.reshape(1, -1),
      w_a, w_b, w_c, x, g1, ln_g.reshape(1, d), ln_b.reshape(1, d), sc2, sh2)


FFN_HALO = 8
FFN_SUB = 256
FFN_ROWS = 512


def _ffn_up_kernel(h_ref, wa_ref, wb_ref, cwa_ref, cwb_ref, cba_ref, cbb_ref, o_ref, bufa, bufb):
    i = pl.program_id(2)
    tm, tn = o_ref.shape

    @pl.when(i == 0)
    def _():
        bufa[0:FFN_HALO, :] = jnp.zeros((FFN_HALO, tn), F32)
        bufb[0:FFN_HALO, :] = jnp.zeros((FFN_HALO, tn), F32)

    subs = [(r * FFN_ROWS, slice(c * FFN_SUB, (c + 1) * FFN_SUB))
            for r in range(tm // FFN_ROWS) for c in range(tn // FFN_SUB)]
    prods = []
    for r0, cs in subs:
        h = h_ref[r0:r0 + FFN_ROWS, :]
        prods.append((jnp.dot(h, wa_ref[:, cs], preferred_element_type=F32),
                      jnp.dot(h, wb_ref[:, cs], preferred_element_type=F32)))

    def conv(u, cw_ref, cb_ref, buf, r0, cs):
        base = FFN_HALO + r0
        buf[base:base + FFN_ROWS, cs] = u
        return (cw_ref[2:3, cs] * u
                + cw_ref[1:2, cs] * buf[base - 1:base - 1 + FFN_ROWS, cs]
                + cw_ref[0:1, cs] * buf[base - 2:base - 2 + FFN_ROWS, cs]
                + cb_ref[:, cs])

    for (r0, cs), (ua, ub) in zip(subs, prods):
        a = conv(ua, cwa_ref, cba_ref, bufa, r0, cs)
        b = conv(ub, cwb_ref, cbb_ref, bufb, r0, cs)
        o_ref[r0:r0 + FFN_ROWS, cs] = (a * (1.0 / (1.0 + jnp.exp(-a))) * b).astype(o_ref.dtype)
    for buf in (bufa, bufb):
        buf[0:FFN_HALO, :] = buf[tm:tm + FFN_HALO, :]


def _ffn_up(h, w_up_bf, layer, conv_w, conv_b):
    bsz, seq, d = h.shape
    dff = w_up_bf.shape[2] // 2
    tm, tn = 1024, 512
    nj = dff // tn
    cb = conv_b.reshape(1, -1)
    return pl.pallas_call(
        _ffn_up_kernel,
        grid=(nj, bsz, seq // tm),
        in_specs=[pl.BlockSpec((None, tm, d), lambda j, b, i: (b, i, 0)),
                  pl.BlockSpec((None, d, tn), lambda j, b, i: (layer, 0, j)),
                  pl.BlockSpec((None, d, tn), lambda j, b, i: (layer, 0, nj + j)),
                  pl.BlockSpec((CONV_WIDTH, tn), lambda j, b, i: (0, j)),
                  pl.BlockSpec((CONV_WIDTH, tn), lambda j, b, i: (0, nj + j)),
                  pl.BlockSpec((1, tn), lambda j, b, i: (0, j)),
                  pl.BlockSpec((1, tn), lambda j, b, i: (0, nj + j))],
        out_specs=pl.BlockSpec((None, tm, tn), lambda j, b, i: (b, i, j)),
        out_shape=jax.ShapeDtypeStruct((bsz, seq, dff), BF16),
        scratch_shapes=[pltpu.VMEM((tm + FFN_HALO, tn), F32)] * 2,
        compiler_params=_cparams("parallel", "parallel", "arbitrary"),
    )(h, w_up_bf, w_up_bf, conv_w, conv_w, cb, cb)


def _ffn_down_kernel(g_ref, w_ref, x_ref, g2_ref, lng_ref, lnb_ref, sc_ref, sh_ref,
                     x_out, *maybe_h_out, alpha):
    y = jnp.dot(g_ref[...], w_ref[...], preferred_element_type=F32)
    x2 = _layer_norm_rows(alpha * x_ref[...] + (1.0 + g2_ref[...]) * y, lng_ref[...], lnb_ref[...])
    x_out[...] = x2
    for h_out in maybe_h_out:
        h_out[...] = (x2 * (1.0 + sc_ref[...]) + sh_ref[...]).astype(BF16)


def _ffn_down(g, w_down_bf, layer, x, g2, ln_g, ln_b, sc_next, sh_next, alpha, with_h):
    bsz, seq, d = x.shape
    dff = g.shape[2]
    tm = 256
    act = pl.BlockSpec((None, tm, d), lambda b, i: (b, i, 0))
    bvec = pl.BlockSpec((None, 1, d), lambda b, i: (b, 0, 0))
    const = pl.BlockSpec((1, d), lambda b, i: (0, 0))
    out_specs = [act, act] if with_h else [act]
    out_shape = [jax.ShapeDtypeStruct((bsz, seq, d), F32)]
    if with_h:
        out_shape.append(jax.ShapeDtypeStruct((bsz, seq, d), BF16))
    return pl.pallas_call(
        functools.partial(_ffn_down_kernel, alpha=alpha),
        grid=(bsz, seq // tm),
        in_specs=[pl.BlockSpec((None, tm, dff), lambda b, i: (b, i, 0)),
                  pl.BlockSpec((None, dff, d), lambda b, i: (layer, 0, 0),
                               pipeline_mode=pl.Buffered(1)),
                  act, bvec, const, const, bvec, bvec],
        out_specs=out_specs,
        out_shape=out_shape,
        compiler_params=_cparams("parallel", "parallel"),
    )(g, w_down_bf, x, g2, ln_g.reshape(1, d), ln_b.reshape(1, d), sc_next, sh_next)


def kernel(x, c, rel_bias, w_ada, b_ada, w_in, b_f, g_mix_a, g_mix_b, g_mix_c, w_out,
           ln1_g, ln1_b, w_up, conv_w, conv_b, w_down, ln2_g, ln2_b):
    bsz, seq, d = x.shape
    depth = w_ada.shape[0]
    alpha = (2 * depth) ** 0.25
    n_qkv = 3 * (W_A + W_B + W_C)

    mod = _ada_mod(c, w_ada, b_ada)

    def mod_vec(l, idx):
        return mod[l, :, idx * d:(idx + 1) * d].reshape(bsz, 1, d)

    bm_a = _dilated_bias_tiles(rel_bias[:, :H_A])
    bt_b = _moba_bias_tiles(rel_bias[:, H_A:], seq)

    w_in_bf, w_up_bf, w_down_bf = w_in.astype(BF16), w_up.astype(BF16), w_down.astype(BF16)

    h = _modulate(x, mod_vec(0, 1), mod_vec(0, 0))
    for l in range(depth):
        w_f = jnp.pad(w_in_bf[l, :, n_qkv:], ((0, 0), (0, LANES - H_C)))
        qkv, f = _inproj(h.reshape(bsz * seq, d), w_in_bf, l, n_qkv, w_f)
        qkv = qkv.reshape(bsz, seq, n_qkv)
        caug = _forget_keys(f.reshape(bsz, seq, LANES), b_f[l])

        o_a = _dilated_attention(qkv, bm_a, bsz, seq)
        o_b = _moba_attention(qkv, bt_b, bsz, seq, 3 * W_A // LANES)
        o_c = _fox_attention(qkv, caug, bsz, seq, 3 * (W_A + W_B) // LANES)

        x, h = _mix_out(o_a, o_b, o_c, g_mix_a[l], g_mix_b[l], g_mix_c[l], w_out[l].astype(BF16),
                        x, mod_vec(l, 2), ln1_g[l], ln1_b[l], mod_vec(l, 4), mod_vec(l, 3), alpha)
        g = _ffn_up(h, w_up_bf, l, conv_w[l], conv_b[l])
        last = l == depth - 1
        nxt = min(l + 1, depth - 1)
        outs = _ffn_down(g, w_down_bf, l, x, mod_vec(l, 5), ln2_g[l], ln2_b[l],
                         mod_vec(nxt, 1), mod_vec(nxt, 0), alpha, with_h=not last)
        x = outs[0]
        if not last:
            h = outs[1]
    return x
```

```python
import functools
import math

import numpy as np
import jax
import jax.numpy as jnp
from jax import lax
from jax.experimental import pallas as pl
from jax.experimental.pallas import tpu as pltpu

F32 = jnp.float32
BF16 = jnp.bfloat16

HEAD_DIM = 64
LANES = 128
H_A, H_B, H_C = 12, 10, 10
W_A, W_B, W_C = H_A * HEAD_DIM, H_B * HEAD_DIM, H_C * HEAD_DIM
DIL_PATTERNS = ((128, 1), (512, 4), (2048, 16))
DIL_W = 128
MOBA_BLOCK = 256
MOBA_TOPK = 3
N_BUCKETS = 32
REL_MAX_DIST = 2048
CONV_WIDTH = 3
LN_EPS = 1e-5
NEG = -1e30
ATTN_SCALE = HEAD_DIM ** -0.5
LOG2E = math.log2(math.e)
VMEM_LIMIT = 56 * 1024 * 1024


def _cparams(*sem):
    return pltpu.CompilerParams(dimension_semantics=sem, vmem_limit_bytes=VMEM_LIMIT)


def _t5_bucket(dist):
    n = jnp.maximum(dist, 0)
    max_exact = N_BUCKETS // 2
    nf = jnp.maximum(n, 1).astype(F32)
    large = max_exact + (jnp.log(nf / max_exact) / math.log(REL_MAX_DIST / max_exact)
                         * (N_BUCKETS - max_exact)).astype(jnp.int32)
    large = jnp.minimum(large, N_BUCKETS - 1)
    return jnp.where(n < max_exact, n, large)


def _ada_kernel(c_ref, w_ref, b_ref, o_ref):
    c = c_ref[...]
    cond = c * (1.0 / (1.0 + jnp.exp(-c)))
    o_ref[...] = jnp.dot(cond, w_ref[...], precision=lax.Precision.HIGHEST,
                         preferred_element_type=F32) + b_ref[...]


def _ada_mod(c, w_ada, b_ada):
    depth, d, n = w_ada.shape
    bsz = c.shape[0]
    tn = 1024
    return pl.pallas_call(
        _ada_kernel,
        grid=(depth, n // tn),
        in_specs=[pl.BlockSpec((bsz, d), lambda l, j: (0, 0)),
                  pl.BlockSpec((None, d, tn), lambda l, j: (l, 0, j)),
                  pl.BlockSpec((None, 1, tn), lambda l, j: (l, 0, j))],
        out_specs=pl.BlockSpec((None, bsz, tn), lambda l, j: (l, 0, j)),
        out_shape=jax.ShapeDtypeStruct((depth, bsz, n), F32),
        compiler_params=_cparams("parallel", "parallel"),
    )(c, w_ada, b_ada.reshape(depth, 1, n))


def _modulate_kernel(x_ref, sc_ref, sh_ref, o_ref):
    o_ref[...] = (x_ref[...] * (1.0 + sc_ref[...]) + sh_ref[...]).astype(o_ref.dtype)


def _modulate(x, sc, sh):
    bsz, s, d = x.shape
    tm = 512
    vec = pl.BlockSpec((None, 1, d), lambda b, i: (b, 0, 0))
    return pl.pallas_call(
        _modulate_kernel,
        grid=(bsz, s // tm),
        in_specs=[pl.BlockSpec((None, tm, d), lambda b, i: (b, i, 0)), vec, vec],
        out_specs=pl.BlockSpec((None, tm, d), lambda b, i: (b, i, 0)),
        out_shape=jax.ShapeDtypeStruct((bsz, s, d), BF16),
        compiler_params=_cparams("parallel", "parallel"),
    )(x, sc, sh)


def _inproj_kernel(a_ref, w_ref, wf_ref, o_ref, f_ref):
    a = a_ref[...]
    o_ref[...] = jnp.dot(a, w_ref[...], preferred_element_type=F32)

    @pl.when(pl.program_id(1) == 0)
    def _():
        f_ref[...] = jnp.dot(a, wf_ref[...], preferred_element_type=F32)


def _inproj(h, w_in_bf, layer, n, w_f):
    t, d = h.shape
    tm, tn = 1024, 1024
    return pl.pallas_call(
        _inproj_kernel,
        grid=(t // tm, n // tn),
        in_specs=[pl.BlockSpec((tm, d), lambda i, j: (i, 0)),
                  pl.BlockSpec((None, d, tn), lambda i, j: (layer, 0, j)),
                  pl.BlockSpec((d, LANES), lambda i, j: (0, 0))],
        out_specs=[pl.BlockSpec((tm, tn), lambda i, j: (i, j)),
                   pl.BlockSpec((tm, LANES), lambda i, j: (i, 0))],
        out_shape=[jax.ShapeDtypeStruct((t, n), F32),
                   jax.ShapeDtypeStruct((t, LANES), F32)],
        compiler_params=_cparams("parallel", "arbitrary"),
    )(h, w_in_bf, w_f)


TOEP_LEN = 512
TOEP_ROWS = 256


def _toeplitz_kernel(rr_ref, o_ref):
    x = jnp.broadcast_to(rr_ref[...], (TOEP_ROWS, TOEP_LEN))
    y = pltpu.roll(x, TOEP_LEN - TOEP_ROWS + 1, 1, stride=1, stride_axis=0)
    o_ref[...] = y[:, :o_ref.shape[1]]


def _toeplitz(rr, cols):
    g = rr.shape[0]
    return pl.pallas_call(
        _toeplitz_kernel,
        grid=(g,),
        in_specs=[pl.BlockSpec((None, 1, TOEP_LEN), lambda i: (i, 0, 0))],
        out_specs=pl.BlockSpec((None, TOEP_ROWS, cols), lambda i: (i, 0, 0)),
        out_shape=jax.ShapeDtypeStruct((g, TOEP_ROWS, cols), F32),
        compiler_params=_cparams("parallel"),
    )(rr.reshape(g, 1, TOEP_LEN))


def _bias_of_distance(rel, dist, valid):
    bias = jnp.moveaxis(rel[_t5_bucket(dist)], -1, 0)
    return jnp.where(valid, bias * LOG2E, NEG).astype(F32)


def _dilated_bias_tiles(rel_a):
    c = jnp.arange(TOEP_LEN)
    rows = []
    for (_, dil) in DIL_PATTERNS:
        d0 = c - (TOEP_ROWS - 1) + DIL_W
        d1 = c - (TOEP_ROWS - 1)
        rows.append(jnp.stack([_bias_of_distance(rel_a, d0 * dil, (d0 >= 0) & (d0 <= DIL_W)),
                               _bias_of_distance(rel_a, d1 * dil, d1 >= 0)]))
    rr = jnp.stack(rows)
    out = _toeplitz(rr.reshape(-1, TOEP_LEN), DIL_W)
    return out.reshape(len(DIL_PATTERNS), 2, H_A, TOEP_ROWS, DIL_W)


def _moba_num_tiles(seq):
    blk = MOBA_BLOCK
    d_all = np.arange(max(seq, 2))
    nf = np.maximum(d_all, 1).astype(np.float32)
    half = N_BUCKETS // 2
    big = half + (np.log(nf / half) / math.log(REL_MAX_DIST / half) * (N_BUCKETS - half)).astype(np.int32)
    bucket = np.where(d_all < half, d_all, np.minimum(big, N_BUCKETS - 1))
    last_change = int(np.max(np.nonzero(np.diff(bucket))[0])) + 1
    first_const = min(seq // blk - 1, -(-(last_change + 2 + blk - 1) // blk))
    return first_const + 1


def _moba_bias_tiles(rel_b, seq):
    ntile = _moba_num_tiles(seq)
    c = jnp.arange(TOEP_LEN)[None, :]
    d = jnp.arange(ntile)[:, None] * MOBA_BLOCK + c - (TOEP_ROWS - 1)
    rr = _bias_of_distance(rel_b, d, d >= 0)
    out = _toeplitz(rr.reshape(-1, TOEP_LEN), MOBA_BLOCK)
    return out.reshape(H_B, ntile, TOEP_ROWS, MOBA_BLOCK)


def _head_rows():
    row = lax.broadcasted_iota(jnp.int32, (LANES, 1), 0)
    return row < HEAD_DIM, row >= HEAD_DIM


DIL_UNROLL = 4


def _dilated_kernel(q_ref, k_ref, v_ref, bm_ref, o_ref, og_s, lse_s, sblk_s, pblk_s, vts_s, *, seq):
    w = DIL_W
    own_rows = _head_rows()
    for g, (_, dil) in enumerate(DIL_PATTERNS):
        nb = (seq // dil) // w

        def rows(start, size, dil=dil):
            return pl.ds(start, size) if dil == 1 else pl.ds(start, size, stride=dil)

        def trip(it, carry, g=g, dil=dil, nb=nb, rows=rows):
            metas = []
            for u in range(DIL_UNROLL):
                idx = it * DIL_UNROLL + u
                r = idx // nb
                n = idx % nb
                q_rows = rows(n * w * dil + r, w)
                kv_rows = rows(jnp.maximum(n - 1, 0) * w * dil + r, 2 * w)
                var = jnp.where(n == 0, 1, 0)
                qt = q_ref[q_rows, :].T * (ATTN_SCALE * LOG2E)
                k = k_ref[kv_rows, :].astype(BF16)
                vt = v_ref[kv_rows, :].T
                tops = []
                for h in range(2):
                    vts_s[u, h] = jnp.concatenate(
                        [vt[h * HEAD_DIM:(h + 1) * HEAD_DIM, :], _ones_rows(2 * w)], axis=0).astype(BF16)
                    s = jnp.dot(k, jnp.where(own_rows[h], qt, 0.0).astype(BF16),
                                preferred_element_type=F32) + bm_ref[g, var, h]
                    sblk_s[u, h] = s
                    tops.append(jnp.max(s, axis=0, keepdims=True))
                metas.append((q_rows, tops))
            pvs = []
            for u, (_, tops) in enumerate(metas):
                for h in range(2):
                    pblk_s[u, h] = jnp.exp2(sblk_s[u, h] - tops[h]).astype(BF16)
                    pvs.append(jnp.dot(vts_s[u, h], pblk_s[u, h], preferred_element_type=F32))
            for u, (q_rows, tops) in enumerate(metas):
                pair = pvs[2 * u:2 * u + 2]
                o_t = jnp.concatenate([pv[:HEAD_DIM] / pv[HEAD_DIM:HEAD_DIM + 1] for pv in pair], axis=0)
                lse_t = jnp.concatenate(
                    [jnp.broadcast_to(m + jnp.log(pv[HEAD_DIM:HEAD_DIM + 1]) * LOG2E, (HEAD_DIM, w))
                     for m, pv in zip(tops, pair)], axis=0)
                og_s[g, q_rows, :] = o_t.T
                lse_s[g, q_rows, :] = lse_t.T
            return carry

        lax.fori_loop(0, nb * dil // DIL_UNROLL, trip, 0)

    top = jnp.maximum(jnp.maximum(lse_s[0], lse_s[1]), lse_s[2])
    num = jnp.zeros_like(top)
    den = jnp.zeros_like(top)
    for g in range(len(DIL_PATTERNS)):
        wg = jnp.exp2(lse_s[g] - top)
        num = num + wg * og_s[g]
        den = den + wg
    o_ref[...] = num / den


def _dilated_attention(qkv, bm_a, bsz, seq):
    npair = H_A // 2
    npat = len(DIL_PATTERNS)
    kern = functools.partial(_dilated_kernel, seq=seq)

    def col(off):
        return pl.BlockSpec((None, seq, LANES), lambda p, b, off=off: (b, 0, off + p))

    return pl.pallas_call(
        kern,
        grid=(npair, bsz),
        in_specs=[col(0), col(npair), col(2 * npair),
                  pl.BlockSpec((npat, 2, 2, 2 * DIL_W, DIL_W), lambda p, b: (0, 0, p, 0, 0))],
        out_specs=pl.BlockSpec((None, seq, LANES), lambda p, b: (b, 0, p)),
        out_shape=jax.ShapeDtypeStruct((bsz, seq, W_A), F32),
        scratch_shapes=[pltpu.VMEM((npat, seq, LANES), F32)] * 2 + [
            pltpu.VMEM((DIL_UNROLL, 2, 2 * DIL_W, DIL_W), F32),
            pltpu.VMEM((DIL_UNROLL, 2, 2 * DIL_W, DIL_W), BF16),
            pltpu.VMEM((DIL_UNROLL, 2, VT_ROWS, 2 * DIL_W), BF16)],
        compiler_params=_cparams("parallel", "parallel"),
    )(qkv, qkv, qkv, bm_a)


VT_ROWS = HEAD_DIM + 16


def _ones_rows(width):
    row = lax.broadcasted_iota(jnp.int32, (VT_ROWS - HEAD_DIM, width), 0)
    return jnp.where(row == 0, 1.0, 0.0)


def _transpose_to(dst_ref, src_ref, chunk=512):
    for c in range(src_ref.shape[0] // chunk):
        cols = slice(c * chunk, (c + 1) * chunk)
        t = src_ref[cols, :].T
        for h in range(2):
            dst_ref[h, :, cols] = jnp.concatenate(
                [t[h * HEAD_DIM:(h + 1) * HEAD_DIM, :], _ones_rows(chunk)], axis=0).astype(BF16)


FLASH_CHUNK = 2


def _own_tile(qi, tk, tq):
    col = lax.broadcasted_iota(jnp.int32, (1, tq), 1)
    return qi * (tq // tk) + col // tk


def _causal_rows(qi, tk, nblk, o_ref, kaug_s, vt_s, qa_s, ssc, psc, bias_fn):
    ck = FLASH_CHUNK * tk
    tq = ck

    def stage(c, u):
        r0 = (c % 2) * ck + u * tk
        return slice(r0, r0 + tk)

    def scores(c, h, maybe_diag):
        top = None
        for u in range(FLASH_CHUNK):
            kj = c * FLASH_CHUNK + u
            s = jnp.dot(kaug_s[h, kj * tk:(kj + 1) * tk, :], qa_s[h], preferred_element_type=F32)
            s = bias_fn(h, kj, s, maybe_diag)
            ssc[h, stage(c, u), :] = s
            cmax = jnp.max(s, axis=0, keepdims=True)
            top = cmax if top is None else jnp.maximum(top, cmax)
        return top

    def values(c, h, top, state):
        m, acc = state
        m_new = jnp.maximum(m, top)
        for u in range(FLASH_CHUNK):
            psc[h, stage(c, u), :] = jnp.exp2(ssc[h, stage(c, u), :] - m_new).astype(BF16)
        slot = slice((c % 2) * ck, (c % 2 + 1) * ck)
        pv = jnp.dot(vt_s[h, :, c * ck:(c + 1) * ck], psc[h, slot, :], preferred_element_type=F32)
        return m_new, jnp.exp2(m - m_new) * acc + pv

    def row_pass(nchunk):
        tops = {(0, h): scores(0, h, nchunk == 1) for h in range(2)}
        state = [(jnp.full((1, tq), NEG, F32), jnp.zeros((VT_ROWS, tq), F32)) for _ in range(2)]
        for c in range(nchunk):
            if c + 1 < nchunk:
                for h in range(2):
                    tops[(c + 1, h)] = scores(c + 1, h, c + 2 == nchunk)
            for h in range(2):
                state[h] = values(c, h, tops[(c, h)], state[h])
        return jnp.concatenate([acc[:HEAD_DIM] / acc[HEAD_DIM:HEAD_DIM + 1]
                                for (_, acc) in state], axis=0).T

    for nchunk in range(1, nblk // FLASH_CHUNK + 1):
        @pl.when(qi == nchunk - 1)
        def _(nchunk=nchunk):
            o_ref[...] = row_pass(nchunk)


def _moba_kernel(q_ref, k_ref, v_ref, bt_ref, o_ref,
                 kaug_s, vt_s, kmean_s, qa_s, ssc, psc, *, nblk, ntile):
    blk = MOBA_BLOCK
    qi = pl.program_id(2)
    spare = LANES - HEAD_DIM - nblk

    @pl.when(qi == 0)
    def _():
        lane = lax.broadcasted_iota(jnp.int32, (1, LANES), 1)
        for n in range(nblk):
            k = k_ref[n * blk:(n + 1) * blk, :]
            kaug_s[0, n * blk:(n + 1) * blk, :] = jnp.where(
                lane < HEAD_DIM, k, (lane == HEAD_DIM + n).astype(F32)).astype(BF16)
            kaug_s[1, n * blk:(n + 1) * blk, :] = jnp.where(
                lane >= HEAD_DIM, k, (lane == n).astype(F32)).astype(BF16)
            kmean_s[n:n + 1, :] = jnp.mean(k, axis=0, keepdims=True)
        _transpose_to(vt_s, v_ref)

    tq = FLASH_CHUNK * blk
    qt = q_ref[...].T
    own_rows = _head_rows()
    blk_id = lax.broadcasted_iota(jnp.int32, (nblk, 1), 0)
    blk_idf = blk_id.astype(F32)
    own_blk = _own_tile(qi, blk, tq)
    past = blk_id < own_blk
    for h in range(2):
        qh = jnp.where(own_rows[h], qt, 0.0)
        gate = jnp.dot(kmean_s[...], qh, precision=lax.Precision.HIGHEST,
                       preferred_element_type=F32)
        gate = jnp.where(past, gate, NEG)
        chosen = jnp.zeros(gate.shape, jnp.bool_)
        for _ in range(MOBA_TOPK):
            mx = jnp.max(gate, axis=0, keepdims=True)
            first_idx = jnp.min(jnp.where(gate == mx, blk_idf, float(nblk)), axis=0, keepdims=True)
            pick = blk_idf == first_idx
            chosen = jnp.logical_or(chosen, pick)
            gate = jnp.where(pick, -jnp.inf, gate)
        keep = jnp.logical_or(jnp.logical_and(chosen, past), blk_id == own_blk)
        selb = jnp.where(keep, 0.0, NEG)
        qs = qt[h * HEAD_DIM:(h + 1) * HEAD_DIM, :] * (ATTN_SCALE * LOG2E)
        pad = jnp.zeros((spare, tq), F32)
        parts = [qs, selb, pad] if h == 0 else [selb, pad, qs]
        qa_s[h] = jnp.concatenate(parts, axis=0).astype(BF16)

    def bias_fn(h, kj, s, _):
        tiles = [bt_ref[h, jnp.clip(qi * FLASH_CHUNK + a - kj, 0, ntile - 1)]
                 for a in range(FLASH_CHUNK)]
        return s + jnp.concatenate(tiles, axis=1)

    _causal_rows(qi, blk, nblk, o_ref, kaug_s, vt_s, qa_s, ssc, psc, bias_fn)


def _moba_attention(qkv, bt_b, bsz, seq, col0):
    npair = H_B // 2
    nblk = seq // MOBA_BLOCK
    ntile = bt_b.shape[1]
    assert nblk % 8 == 0 and nblk <= LANES - HEAD_DIM
    assert nblk % FLASH_CHUNK == 0
    kern = functools.partial(_moba_kernel, nblk=nblk, ntile=ntile)
    blk = MOBA_BLOCK
    tq = FLASH_CHUNK * blk

    def col(off):
        return pl.BlockSpec((None, seq, LANES), lambda p, b, i, off=off: (b, 0, off + p))

    return pl.pallas_call(
        kern,
        grid=(npair, bsz, seq // tq),
        in_specs=[pl.BlockSpec((None, tq, LANES), lambda p, b, i: (b, i, col0 + p)),
                  col(col0 + npair), col(col0 + 2 * npair),
                  pl.BlockSpec((2, ntile, blk, blk), lambda p, b, i: (p, 0, 0, 0))],
        out_specs=pl.BlockSpec((None, tq, LANES), lambda p, b, i: (b, i, p)),
        out_shape=jax.ShapeDtypeStruct((bsz, seq, W_B), F32),
        scratch_shapes=[pltpu.VMEM((2, seq, LANES), BF16), pltpu.VMEM((2, VT_ROWS, seq), BF16),
                        pltpu.VMEM((nblk, LANES), F32), pltpu.VMEM((2, LANES, tq), BF16),
                        pltpu.VMEM((2, 2 * tq, tq), F32), pltpu.VMEM((2, 2 * tq, tq), BF16)],
        compiler_params=_cparams("parallel", "parallel", "arbitrary"),
    )(qkv, qkv, qkv, bt_b)


CUM_SPLIT = 3
CUM_CHUNK = 256
FOX_TILE = 256


def _fgate_kernel(f_ref, bf_ref, pm_ref, o_ref, cum_s):
    seq = f_ref.shape[0]
    ch = CUM_CHUNK
    i = lax.broadcasted_iota(jnp.int32, (ch, ch), 0)
    j = lax.broadcasted_iota(jnp.int32, (ch, ch), 1)
    lower = (j <= i).astype(F32)
    carry = jnp.zeros((1, LANES), F32)
    for c in range(seq // ch):
        z = f_ref[c * ch:(c + 1) * ch, :] + bf_ref[...]
        ls = jnp.minimum(z, 0.0) - jnp.log(1.0 + jnp.exp(-jnp.abs(z)))
        cum = jnp.dot(lower, ls, precision=lax.Precision.HIGHEST, preferred_element_type=F32) + carry
        cum_s[c * ch:(c + 1) * ch, :] = cum
        carry = cum[ch - 1:ch, :]
    rest = cum_s[...] * (-LOG2E)
    out = None
    for t in range(CUM_SPLIT):
        term = rest.astype(BF16)
        rest = rest - term.astype(F32)
        part = jnp.dot(term, pm_ref[t], preferred_element_type=F32)
        out = part if out is None else out + part
    o_ref[...] = out.astype(BF16)


def _forget_keys(f, b_f):
    bsz, seq, _ = f.shape
    npair = H_C // 2
    nblk = seq // FOX_TILE
    pm = np.zeros((CUM_SPLIT, LANES, npair * LANES), np.float32)
    for t in range(CUM_SPLIT):
        for p in range(npair):
            pm[t, 2 * p, p * LANES + HEAD_DIM + nblk + t] = 1.0
            pm[t, 2 * p + 1, p * LANES + nblk + t] = 1.0
    bf = jnp.pad(b_f, (0, LANES - b_f.shape[0])).reshape(1, LANES)
    return pl.pallas_call(
        _fgate_kernel,
        grid=(bsz,),
        in_specs=[pl.BlockSpec((None, seq, LANES), lambda b: (b, 0, 0)),
                  pl.BlockSpec((1, LANES), lambda b: (0, 0)),
                  pl.BlockSpec((CUM_SPLIT, LANES, npair * LANES), lambda b: (0, 0, 0))],
        out_specs=pl.BlockSpec((None, seq, npair * LANES), lambda b: (b, 0, 0)),
        out_shape=jax.ShapeDtypeStruct((bsz, seq, npair * LANES), BF16),
        scratch_shapes=[pltpu.VMEM((seq, LANES), F32)],
        compiler_params=_cparams("parallel"),
    )(f, bf, jnp.asarray(pm, BF16))


def _fox_kernel(q_ref, k_ref, v_ref, ca_ref, o_ref,
                kaug_s, vt_s, qa_s, cm_s, ssc, psc, *, tk, nblk):
    qi = pl.program_id(2)
    spare = LANES - HEAD_DIM - nblk - 8

    @pl.when(qi == 0)
    def _():
        lane = lax.broadcasted_iota(jnp.int32, (1, LANES), 1)
        for n in range(nblk):
            k = k_ref[n * tk:(n + 1) * tk, :]
            ca = ca_ref[n * tk:(n + 1) * tk, :].astype(F32)
            kaug_s[0, n * tk:(n + 1) * tk, :] = jnp.where(
                lane < HEAD_DIM, k, ca + (lane == HEAD_DIM + n).astype(F32)).astype(BF16)
            kaug_s[1, n * tk:(n + 1) * tk, :] = jnp.where(
                lane >= HEAD_DIM, k, ca + (lane == n).astype(F32)).astype(BF16)
        _transpose_to(vt_s, v_ref)
        key = lax.broadcasted_iota(jnp.int32, (tk, tk), 0)
        qry = lax.broadcasted_iota(jnp.int32, (tk, tk), 1)
        cm_s[0] = jnp.zeros((tk, tk), F32)
        cm_s[1] = jnp.where(key <= qry, 0.0, NEG)

    tq = FLASH_CHUNK * tk
    qt = q_ref[...].T * (ATTN_SCALE * LOG2E)
    blk_id = lax.broadcasted_iota(jnp.int32, (nblk, 1), 0)
    future = jnp.where(blk_id <= _own_tile(qi, tk, tq), 0.0, NEG)
    sub = lax.broadcasted_iota(jnp.int32, (8, 1), 0)
    ones = jnp.broadcast_to(jnp.where(sub < CUM_SPLIT, 1.0, 0.0), (8, tq))
    pad = jnp.zeros((spare, tq), F32)
    for h in range(2):
        qs = qt[h * HEAD_DIM:(h + 1) * HEAD_DIM, :]
        parts = [qs, future, ones, pad] if h == 0 else [future, ones, pad, qs]
        qa_s[h] = jnp.concatenate(parts, axis=0).astype(BF16)

    def bias_fn(h, kj, s, maybe_diag):
        if not maybe_diag:
            return s
        tiles = [cm_s[jnp.where(kj == qi * FLASH_CHUNK + a, 1, 0)] for a in range(FLASH_CHUNK)]
        return s + jnp.concatenate(tiles, axis=1)

    _causal_rows(qi, tk, nblk, o_ref, kaug_s, vt_s, qa_s, ssc, psc, bias_fn)


def _fox_attention(qkv, caug, bsz, seq, col0):
    npair = H_C // 2
    tk = FOX_TILE
    tq = FLASH_CHUNK * tk
    nblk = seq // tk
    assert nblk % 8 == 0 and nblk % FLASH_CHUNK == 0 and nblk + 8 <= LANES - HEAD_DIM
    kern = functools.partial(_fox_kernel, tk=tk, nblk=nblk)

    def col(off):
        return pl.BlockSpec((None, seq, LANES), lambda p, b, i, off=off: (b, 0, off + p))

    return pl.pallas_call(
        kern,
        grid=(npair, bsz, seq // tq),
        in_specs=[pl.BlockSpec((None, tq, LANES), lambda p, b, i: (b, i, col0 + p)),
                  col(col0 + npair), col(col0 + 2 * npair),
                  pl.BlockSpec((None, seq, LANES), lambda p, b, i: (b, 0, p))],
        out_specs=pl.BlockSpec((None, tq, LANES), lambda p, b, i: (b, i, p)),
        out_shape=jax.ShapeDtypeStruct((bsz, seq, W_C), F32),
        scratch_shapes=[pltpu.VMEM((2, seq, LANES), BF16), pltpu.VMEM((2, VT_ROWS, seq), BF16),
                        pltpu.VMEM((2, LANES, tq), BF16), pltpu.VMEM((2, tk, tk), F32),
                        pltpu.VMEM((2, 2 * tq, tq), F32), pltpu.VMEM((2, 2 * tq, tq), BF16)],
        compiler_params=_cparams("parallel", "parallel", "arbitrary"),
    )(qkv, qkv, qkv, caug)


def _layer_norm_rows(z, g, b):
    mu = jnp.mean(z, axis=-1, keepdims=True)
    zc = z - mu
    var = jnp.mean(zc * zc, axis=-1, keepdims=True)
    return zc * lax.rsqrt(var + LN_EPS) * g + b


def _mixout_kernel(oa_ref, ob_ref, oc_ref, ga_ref, gb_ref, gc_ref, wa_ref, wb_ref, wc_ref,
                   x_ref, g1_ref, lng_ref, lnb_ref, sc_ref, sh_ref, x_out, h_out, *, alpha):
    def normed(o_ref, g_ref):
        o = o_ref[...]
        ms = jnp.mean(o * o, axis=-1, keepdims=True)
        return (o * lax.rsqrt(ms + LN_EPS) * g_ref[...]).astype(BF16)

    y = jnp.dot(normed(oa_ref, ga_ref), wa_ref[...], preferred_element_type=F32)
    y = y + jnp.dot(normed(ob_ref, gb_ref), wb_ref[...], preferred_element_type=F32)
    y = y + jnp.dot(normed(oc_ref, gc_ref), wc_ref[...], preferred_element_type=F32)
    x1 = _layer_norm_rows(alpha * x_ref[...] + (1.0 + g1_ref[...]) * y, lng_ref[...], lnb_ref[...])
    x_out[...] = x1
    h_out[...] = (x1 * (1.0 + sc_ref[...]) + sh_ref[...]).astype(BF16)


def _mix_out(o_a, o_b, o_c, g_a, g_b, g_c, w_out_bf, x, g1, ln_g, ln_b, sc2, sh2, alpha):
    bsz, seq, d = x.shape
    tm = 256
    w_a, w_b, w_c = w_out_bf[:W_A], w_out_bf[W_A:W_A + W_B], w_out_bf[W_A + W_B:]

    def act(width):
        return pl.BlockSpec((None, tm, width), lambda b, i: (b, i, 0))

    def const(shape):
        return pl.BlockSpec(shape, lambda b, i: (0,) * len(shape))

    bvec = pl.BlockSpec((None, 1, d), lambda b, i: (b, 0, 0))
    return pl.pallas_call(
        functools.partial(_mixout_kernel, alpha=alpha),
        grid=(bsz, seq // tm),
        in_specs=[act(W_A), act(W_B), act(W_C),
                  const((1, W_A)), const((1, W_B)), const((1, W_C)),
                  const((W_A, d)), const((W_B, d)), const((W_C, d)),
                  act(d), bvec, const((1, d)), const((1, d)), bvec, bvec],
        out_specs=[act(d), act(d)],
        out_shape=[jax.ShapeDtypeStruct((bsz, seq, d), F32),
                   jax.ShapeDtypeStruct((bsz, seq, d), BF16)],
        compiler_params=_cparams("parallel", "parallel"),
    )(o_a, o_b, o_c, g_a.reshape(1, -1), g_b.reshape(1, -1), g_c.reshape(1, -1),
      w_a, w_b, w_c, x, g1, ln_g.reshape(1, d), ln_b.reshape(1, d), sc2, sh2)


FFN_HALO = 8
FFN_SUB = 256
FFN_ROWS = 512


def _ffn_up_kernel(h_ref, wa_ref, wb_ref, cwa_ref, cwb_ref, cba_ref, cbb_ref, o_ref, bufa, bufb):
    i = pl.program_id(2)
    tm, tn = o_ref.shape

    @pl.when(i == 0)
    def _():
        bufa[0:FFN_HALO, :] = jnp.zeros((FFN_HALO, tn), F32)
        bufb[0:FFN_HALO, :] = jnp.zeros((FFN_HALO, tn), F32)

    subs = [(r * FFN_ROWS, slice(c * FFN_SUB, (c + 1) * FFN_SUB))
            for r in range(tm // FFN_ROWS) for c in range(tn // FFN_SUB)]
    prods = []
    for r0, cs in subs:
        h = h_ref[r0:r0 + FFN_ROWS, :]
        prods.append((jnp.dot(h, wa_ref[:, cs], preferred_element_type=F32),
                      jnp.dot(h, wb_ref[:, cs], preferred_element_type=F32)))

    def conv(u, cw_ref, cb_ref, buf, r0, cs):
        base = FFN_HALO + r0
        buf[base:base + FFN_ROWS, cs] = u
        return (cw_ref[2:3, cs] * u
                + cw_ref[1:2, cs] * buf[base - 1:base - 1 + FFN_ROWS, cs]
                + cw_ref[0:1, cs] * buf[base - 2:base - 2 + FFN_ROWS, cs]
                + cb_ref[:, cs])

    for (r0, cs), (ua, ub) in zip(subs, prods):
        a = conv(ua, cwa_ref, cba_ref, bufa, r0, cs)
        b = conv(ub, cwb_ref, cbb_ref, bufb, r0, cs)
        o_ref[r0:r0 + FFN_ROWS, cs] = (a * (1.0 / (1.0 + jnp.exp(-a))) * b).astype(o_ref.dtype)
    for buf in (bufa, bufb):
        buf[0:FFN_HALO, :] = buf[tm:tm + FFN_HALO, :]


def _ffn_up(h, w_up_bf, layer, conv_w, conv_b):
    bsz, seq, d = h.shape
    dff = w_up_bf.shape[2] // 2
    tm, tn = 1024, 512
    nj = dff // tn
    cb = conv_b.reshape(1, -1)
    return pl.pallas_call(
        _ffn_up_kernel,
        grid=(nj, bsz, seq // tm),
        in_specs=[pl.BlockSpec((None, tm, d), lambda j, b, i: (b, i, 0)),
                  pl.BlockSpec((None, d, tn), lambda j, b, i: (layer, 0, j)),
                  pl.BlockSpec((None, d, tn), lambda j, b, i: (layer, 0, nj + j)),
                  pl.BlockSpec((CONV_WIDTH, tn), lambda j, b, i: (0, j)),
                  pl.BlockSpec((CONV_WIDTH, tn), lambda j, b, i: (0, nj + j)),
                  pl.BlockSpec((1, tn), lambda j, b, i: (0, j)),
                  pl.BlockSpec((1, tn), lambda j, b, i: (0, nj + j))],
        out_specs=pl.BlockSpec((None, tm, tn), lambda j, b, i: (b, i, j)),
        out_shape=jax.ShapeDtypeStruct((bsz, seq, dff), BF16),
        scratch_shapes=[pltpu.VMEM((tm + FFN_HALO, tn), F32)] * 2,
        compiler_params=_cparams("parallel", "parallel", "arbitrary"),
    )(h, w_up_bf, w_up_bf, conv_w, conv_w, cb, cb)


def _ffn_down_kernel(g_ref, w_ref, x_ref, g2_ref, lng_ref, lnb_ref, sc_ref, sh_ref,
                     x_out, *maybe_h_out, alpha):
    y = jnp.dot(g_ref[...], w_ref[...], preferred_element_type=F32)
    x2 = _layer_norm_rows(alpha * x_ref[...] + (1.0 + g2_ref[...]) * y, lng_ref[...], lnb_ref[...])
    x_out[...] = x2
    for h_out in maybe_h_out:
        h_out[...] = (x2 * (1.0 + sc_ref[...]) + sh_ref[...]).astype(BF16)


def _ffn_down(g, w_down_bf, layer, x, g2, ln_g, ln_b, sc_next, sh_next, alpha, with_h):
    bsz, seq, d = x.shape
    dff = g.shape[2]
    tm = 256
    act = pl.BlockSpec((None, tm, d), lambda b, i: (b, i, 0))
    bvec = pl.BlockSpec((None, 1, d), lambda b, i: (b, 0, 0))
    const = pl.BlockSpec((1, d), lambda b, i: (0, 0))
    out_specs = [act, act] if with_h else [act]
    out_shape = [jax.ShapeDtypeStruct((bsz, seq, d), F32)]
    if with_h:
        out_shape.append(jax.ShapeDtypeStruct((bsz, seq, d), BF16))
    return pl.pallas_call(
        functools.partial(_ffn_down_kernel, alpha=alpha),
        grid=(bsz, seq // tm),
        in_specs=[pl.BlockSpec((None, tm, dff), lambda b, i: (b, i, 0)),
                  pl.BlockSpec((None, dff, d), lambda b, i: (layer, 0, 0),
                               pipeline_mode=pl.Buffered(1)),
                  act, bvec, const, const, bvec, bvec],
        out_specs=out_specs,
        out_shape=out_shape,
        compiler_params=_cparams("parallel", "parallel"),
    )(g, w_down_bf, x, g2, ln_g.reshape(1, d), ln_b.reshape(1, d), sc_next, sh_next)


def kernel(x, c, rel_bias, w_ada, b_ada, w_in, b_f, g_mix_a, g_mix_b, g_mix_c, w_out,
           ln1_g, ln1_b, w_up, conv_w, conv_b, w_down, ln2_g, ln2_b):
    bsz, seq, d = x.shape
    depth = w_ada.shape[0]
    alpha = (2 * depth) ** 0.25
    n_qkv = 3 * (W_A + W_B + W_C)

    mod = _ada_mod(c, w_ada, b_ada)

    def mod_vec(l, idx):
        return mod[l, :, idx * d:(idx + 1) * d].reshape(bsz, 1, d)

    bm_a = _dilated_bias_tiles(rel_bias[:, :H_A])
    bt_b = _moba_bias_tiles(rel_bias[:, H_A:], seq)

    w_in_bf, w_up_bf, w_down_bf = w_in.astype(BF16), w_up.astype(BF16), w_down.astype(BF16)

    h = _modulate(x, mod_vec(0, 1), mod_vec(0, 0))
    for l in range(depth):
        w_f = jnp.pad(w_in_bf[l, :, n_qkv:], ((0, 0), (0, LANES - H_C)))
        qkv, f = _inproj(h.reshape(bsz * seq, d), w_in_bf, l, n_qkv, w_f)
        qkv = qkv.reshape(bsz, seq, n_qkv)
        caug = _forget_keys(f.reshape(bsz, seq, LANES), b_f[l])

        o_a = _dilated_attention(qkv, bm_a, bsz, seq)
        o_b = _moba_attention(qkv, bt_b, bsz, seq, 3 * W_A // LANES)
        o_c = _fox_attention(qkv, caug, bsz, seq, 3 * (W_A + W_B) // LANES)

        x, h = _mix_out(o_a, o_b, o_c, g_mix_a[l], g_mix_b[l], g_mix_c[l], w_out[l].astype(BF16),
                        x, mod_vec(l, 2), ln1_g[l], ln1_b[l], mod_vec(l, 4), mod_vec(l, 3), alpha)
        g = _ffn_up(h, w_up_bf, l, conv_w[l], conv_b[l])
        last = l == depth - 1
        nxt = min(l + 1, depth - 1)
        outs = _ffn_down(g, w_down_bf, l, x, mod_vec(l, 5), ln2_g[l], ln2_b[l],
                         mod_vec(nxt, 1), mod_vec(nxt, 0), alpha, with_h=not last)
        x = outs[0]
        if not last:
            h = outs[1]
    return x
```

```python
import functools
import math

import numpy as np
import jax
import jax.numpy as jnp
from jax import lax
from jax.experimental import pallas as pl
from jax.experimental.pallas import tpu as pltpu

F32 = jnp.float32
BF16 = jnp.bfloat16

HEAD_DIM = 64
LANES = 128
H_A, H_B, H_C = 12, 10, 10
W_A, W_B, W_C = H_A * HEAD_DIM, H_B * HEAD_DIM, H_C * HEAD_DIM
DIL_PATTERNS = ((128, 1), (512, 4), (2048, 16))
DIL_W = 128
MOBA_BLOCK = 256
MOBA_TOPK = 3
N_BUCKETS = 32
REL_MAX_DIST = 2048
CONV_WIDTH = 3
LN_EPS = 1e-5
NEG = -1e30
ATTN_SCALE = HEAD_DIM ** -0.5
LOG2E = math.log2(math.e)
VMEM_LIMIT = 56 * 1024 * 1024


def _cparams(*sem):
    return pltpu.CompilerParams(dimension_semantics=sem, vmem_limit_bytes=VMEM_LIMIT)


def _t5_bucket(dist):
    n = jnp.maximum(dist, 0)
    max_exact = N_BUCKETS // 2
    nf = jnp.maximum(n, 1).astype(F32)
    large = max_exact + (jnp.log(nf / max_exact) / math.log(REL_MAX_DIST / max_exact)
                         * (N_BUCKETS - max_exact)).astype(jnp.int32)
    large = jnp.minimum(large, N_BUCKETS - 1)
    return jnp.where(n < max_exact, n, large)


def _ada_kernel(c_ref, w_ref, b_ref, o_ref):
    c = c_ref[...]
    cond = c * (1.0 / (1.0 + jnp.exp(-c)))
    o_ref[...] = jnp.dot(cond, w_ref[...], precision=lax.Precision.HIGHEST,
                         preferred_element_type=F32) + b_ref[...]


def _ada_mod(c, w_ada, b_ada):
    depth, d, n = w_ada.shape
    bsz = c.shape[0]
    tn = 1024
    return pl.pallas_call(
        _ada_kernel,
        grid=(depth, n // tn),
        in_specs=[pl.BlockSpec((bsz, d), lambda l, j: (0, 0)),
                  pl.BlockSpec((None, d, tn), lambda l, j: (l, 0, j)),
                  pl.BlockSpec((None, 1, tn), lambda l, j: (l, 0, j))],
        out_specs=pl.BlockSpec((None, bsz, tn), lambda l, j: (l, 0, j)),
        out_shape=jax.ShapeDtypeStruct((depth, bsz, n), F32),
        compiler_params=_cparams("parallel", "parallel"),
    )(c, w_ada, b_ada.reshape(depth, 1, n))


def _modulate_kernel(x_ref, sc_ref, sh_ref, o_ref):
    o_ref[...] = (x_ref[...] * (1.0 + sc_ref[...]) + sh_ref[...]).astype(o_ref.dtype)


def _modulate(x, sc, sh):
    bsz, s, d = x.shape
    tm = 512
    vec = pl.BlockSpec((None, 1, d), lambda b, i: (b, 0, 0))
    return pl.pallas_call(
        _modulate_kernel,
        grid=(bsz, s // tm),
        in_specs=[pl.BlockSpec((None, tm, d), lambda b, i: (b, i, 0)), vec, vec],
        out_specs=pl.BlockSpec((None, tm, d), lambda b, i: (b, i, 0)),
        out_shape=jax.ShapeDtypeStruct((bsz, s, d), BF16),
        compiler_params=_cparams("parallel", "parallel"),
    )(x, sc, sh)


def _inproj_kernel(a_ref, w_ref, wf_ref, o_ref, f_ref):
    a = a_ref[...]
    o_ref[...] = jnp.dot(a, w_ref[...], preferred_element_type=F32)

    @pl.when(pl.program_id(1) == 0)
    def _():
        f_ref[...] = jnp.dot(a, wf_ref[...], preferred_element_type=F32)


def _inproj(h, w_in_bf, layer, n, w_f):
    t, d = h.shape
    tm, tn = 1024, 1024
    return pl.pallas_call(
        _inproj_kernel,
        grid=(t // tm, n // tn),
        in_specs=[pl.BlockSpec((tm, d), lambda i, j: (i, 0)),
                  pl.BlockSpec((None, d, tn), lambda i, j: (layer, 0, j)),
                  pl.BlockSpec((d, LANES), lambda i, j: (0, 0))],
        out_specs=[pl.BlockSpec((tm, tn), lambda i, j: (i, j)),
                   pl.BlockSpec((tm, LANES), lambda i, j: (i, 0))],
        out_shape=[jax.ShapeDtypeStruct((t, n), F32),
                   jax.ShapeDtypeStruct((t, LANES), F32)],
        compiler_params=_cparams("parallel", "arbitrary"),
    )(h, w_in_bf, w_f)


TOEP_LEN = 512
TOEP_ROWS = 256


TOEP_BATCH = 8


def _toeplitz_kernel(rr_ref, o_ref):
    for t in range(TOEP_BATCH):
        x = jnp.broadcast_to(rr_ref[t], (TOEP_ROWS, TOEP_LEN))
        y = pltpu.roll(x, TOEP_LEN - TOEP_ROWS + 1, 1, stride=1, stride_axis=0)
        o_ref[t] = y[:, :o_ref.shape[2]]


def _toeplitz(rr, cols):
    g = rr.shape[0]
    assert g % TOEP_BATCH == 0
    return pl.pallas_call(
        _toeplitz_kernel,
        grid=(g // TOEP_BATCH,),
        in_specs=[pl.BlockSpec((TOEP_BATCH, 1, TOEP_LEN), lambda i: (i, 0, 0))],
        out_specs=pl.BlockSpec((TOEP_BATCH, TOEP_ROWS, cols), lambda i: (i, 0, 0)),
        out_shape=jax.ShapeDtypeStruct((g, TOEP_ROWS, cols), F32),
        compiler_params=_cparams("parallel"),
    )(rr.reshape(g, 1, TOEP_LEN))


def _bias_of_distance(rel, dist, valid):
    bias = jnp.moveaxis(rel[_t5_bucket(dist)], -1, 0)
    return jnp.where(valid, bias * LOG2E, NEG).astype(F32)


def _dilated_bias_tiles(rel_a):
    c = jnp.arange(TOEP_LEN)
    rows = []
    for (_, dil) in DIL_PATTERNS:
        d0 = c - (TOEP_ROWS - 1) + DIL_W
        d1 = c - (TOEP_ROWS - 1)
        rows.append(jnp.stack([_bias_of_distance(rel_a, d0 * dil, (d0 >= 0) & (d0 <= DIL_W)),
                               _bias_of_distance(rel_a, d1 * dil, d1 >= 0)]))
    rr = jnp.stack(rows)
    out = _toeplitz(rr.reshape(-1, TOEP_LEN), DIL_W)
    return out.reshape(len(DIL_PATTERNS), 2, H_A, TOEP_ROWS, DIL_W)


def _moba_num_tiles(seq):
    blk = MOBA_BLOCK
    d_all = np.arange(max(seq, 2))
    nf = np.maximum(d_all, 1).astype(np.float32)
    half = N_BUCKETS // 2
    big = half + (np.log(nf / half) / math.log(REL_MAX_DIST / half) * (N_BUCKETS - half)).astype(np.int32)
    bucket = np.where(d_all < half, d_all, np.minimum(big, N_BUCKETS - 1))
    last_change = int(np.max(np.nonzero(np.diff(bucket))[0])) + 1
    first_const = min(seq // blk - 1, -(-(last_change + 2 + blk - 1) // blk))
    return first_const + 1


def _moba_bias_tiles(rel_b, seq):
    ntile = _moba_num_tiles(seq)
    c = jnp.arange(TOEP_LEN)[None, :]
    d = jnp.arange(ntile)[:, None] * MOBA_BLOCK + c - (TOEP_ROWS - 1)
    rr = _bias_of_distance(rel_b, d, d >= 0)
    out = _toeplitz(rr.reshape(-1, TOEP_LEN), MOBA_BLOCK)
    return out.reshape(H_B, ntile, TOEP_ROWS, MOBA_BLOCK)


def _head_rows():
    row = lax.broadcasted_iota(jnp.int32, (LANES, 1), 0)
    return row < HEAD_DIM, row >= HEAD_DIM


DIL_UNROLL = 4


def _dilated_kernel(q_ref, k_ref, v_ref, bm_ref, o_ref, og_s, lse_s, sblk_s, pblk_s, vts_s, *, seq):
    w = DIL_W
    own_rows = _head_rows()
    for g, (_, dil) in enumerate(DIL_PATTERNS):
        nb = (seq // dil) // w

        def rows(start, size, dil=dil):
            return pl.ds(start, size) if dil == 1 else pl.ds(start, size, stride=dil)

        def trip(it, carry, g=g, dil=dil, nb=nb, rows=rows):
            metas = []
            for u in range(DIL_UNROLL):
                idx = it * DIL_UNROLL + u
                r = idx // nb
                n = idx % nb
                q_rows = rows(n * w * dil + r, w)
                kv_rows = rows(jnp.maximum(n - 1, 0) * w * dil + r, 2 * w)
                var = jnp.where(n == 0, 1, 0)
                qt = q_ref[q_rows, :].T * (ATTN_SCALE * LOG2E)
                k = k_ref[kv_rows, :].astype(BF16)
                vt = v_ref[kv_rows, :].T
                tops = []
                for h in range(2):
                    vts_s[u, h] = jnp.concatenate(
                        [vt[h * HEAD_DIM:(h + 1) * HEAD_DIM, :], _ones_rows(2 * w)], axis=0).astype(BF16)
                    s = jnp.dot(k, jnp.where(own_rows[h], qt, 0.0).astype(BF16),
                                preferred_element_type=F32) + bm_ref[g, var, h]
                    sblk_s[u, h] = s
                    tops.append(jnp.max(s, axis=0, keepdims=True))
                metas.append((q_rows, tops))
            pvs = []
            for u, (_, tops) in enumerate(metas):
                for h in range(2):
                    pblk_s[u, h] = jnp.exp2(sblk_s[u, h] - tops[h]).astype(BF16)
                    pvs.append(jnp.dot(vts_s[u, h], pblk_s[u, h], preferred_element_type=F32))
            for u, (q_rows, tops) in enumerate(metas):
                pair = pvs[2 * u:2 * u + 2]
                o_t = jnp.concatenate([pv[:HEAD_DIM] / pv[HEAD_DIM:HEAD_DIM + 1] for pv in pair], axis=0)
                lse_t = jnp.concatenate(
                    [jnp.broadcast_to(m + jnp.log(pv[HEAD_DIM:HEAD_DIM + 1]) * LOG2E, (HEAD_DIM, w))
                     for m, pv in zip(tops, pair)], axis=0)
                og_s[g, q_rows, :] = o_t.T
                lse_s[g, q_rows, :] = lse_t.T
            return carry

        lax.fori_loop(0, nb * dil // DIL_UNROLL, trip, 0)

    top = jnp.maximum(jnp.maximum(lse_s[0], lse_s[1]), lse_s[2])
    num = jnp.zeros_like(top)
    den = jnp.zeros_like(top)
    for g in range(len(DIL_PATTERNS)):
        wg = jnp.exp2(lse_s[g] - top)
        num = num + wg * og_s[g]
        den = den + wg
    o_ref[...] = num / den


def _dilated_attention(qkv, bm_a, bsz, seq):
    npair = H_A // 2
    npat = len(DIL_PATTERNS)
    kern = functools.partial(_dilated_kernel, seq=seq)

    def col(off):
        return pl.BlockSpec((None, seq, LANES), lambda p, b, off=off: (b, 0, off + p))

    return pl.pallas_call(
        kern,
        grid=(npair, bsz),
        in_specs=[col(0), col(npair), col(2 * npair),
                  pl.BlockSpec((npat, 2, 2, 2 * DIL_W, DIL_W), lambda p, b: (0, 0, p, 0, 0))],
        out_specs=pl.BlockSpec((None, seq, LANES), lambda p, b: (b, 0, p)),
        out_shape=jax.ShapeDtypeStruct((bsz, seq, W_A), F32),
        scratch_shapes=[pltpu.VMEM((npat, seq, LANES), F32)] * 2 + [
            pltpu.VMEM((DIL_UNROLL, 2, 2 * DIL_W, DIL_W), F32),
            pltpu.VMEM((DIL_UNROLL, 2, 2 * DIL_W, DIL_W), BF16),
            pltpu.VMEM((DIL_UNROLL, 2, VT_ROWS, 2 * DIL_W), BF16)],
        compiler_params=_cparams("parallel", "parallel"),
    )(qkv, qkv, qkv, bm_a)


VT_ROWS = HEAD_DIM + 16


def _ones_rows(width):
    row = lax.broadcasted_iota(jnp.int32, (VT_ROWS - HEAD_DIM, width), 0)
    return jnp.where(row == 0, 1.0, 0.0)


def _transpose_to(dst_ref, src_ref, chunk=512):
    for c in range(src_ref.shape[0] // chunk):
        cols = slice(c * chunk, (c + 1) * chunk)
        t = src_ref[cols, :].T
        for h in range(2):
            dst_ref[h, :, cols] = jnp.concatenate(
                [t[h * HEAD_DIM:(h + 1) * HEAD_DIM, :], _ones_rows(chunk)], axis=0).astype(BF16)


FLASH_CHUNK = 2


def _own_tile(qi, tk, tq):
    col = lax.broadcasted_iota(jnp.int32, (1, tq), 1)
    return qi * (tq // tk) + col // tk


def _causal_rows(qi, tk, nblk, o_ref, kaug_s, vt_s, qa_s, ssc, psc, acc_s, bias_fn):
    ck = FLASH_CHUNK * tk
    tq = ck

    def stage(c, u):
        r0 = (c % 2) * ck + u * tk
        return slice(r0, r0 + tk)

    def scores(c, h, maybe_diag):
        top = None
        for u in range(FLASH_CHUNK):
            kj = c * FLASH_CHUNK + u
            s = jnp.dot(kaug_s[h, kj * tk:(kj + 1) * tk, :], qa_s[h], preferred_element_type=F32)
            s = bias_fn(h, kj, s, maybe_diag)
            ssc[h, stage(c, u), :] = s
            cmax = jnp.max(s, axis=0, keepdims=True)
            top = cmax if top is None else jnp.maximum(top, cmax)
        return top

    def values(c, h, top, m):
        for u in range(FLASH_CHUNK):
            psc[h, stage(c, u), :] = jnp.exp2(ssc[h, stage(c, u), :] - top).astype(BF16)
        slot = slice((c % 2) * ck, (c % 2 + 1) * ck)
        pv = jnp.dot(vt_s[h, :, c * ck:(c + 1) * ck], psc[h, slot, :], preferred_element_type=F32)
        acc_s[h] = pv if m is None else jnp.exp2(m - top) * acc_s[h] + pv

    def row_pass(nchunk):
        tops = {(0, h): scores(0, h, nchunk == 1) for h in range(2)}
        for c in range(nchunk):
            if c + 1 < nchunk:
                for h in range(2):
                    tops[(c + 1, h)] = jnp.maximum(tops[(c, h)], scores(c + 1, h, c + 2 == nchunk))
            for h in range(2):
                values(c, h, tops[(c, h)], tops[(c - 1, h)] if c else None)
        return jnp.concatenate([acc_s[h, :HEAD_DIM] / acc_s[h, HEAD_DIM:HEAD_DIM + 1]
                                for h in range(2)], axis=0).T

    for nchunk in range(1, nblk // FLASH_CHUNK + 1):
        @pl.when(qi == nchunk - 1)
        def _(nchunk=nchunk):
            o_ref[...] = row_pass(nchunk)


def _moba_kernel(q_ref, k_ref, v_ref, bt_ref, o_ref,
                 kaug_s, vt_s, kmean_s, qa_s, ssc, psc, acc_s, *, nblk, ntile):
    blk = MOBA_BLOCK
    qi = pl.program_id(2)
    spare = LANES - HEAD_DIM - nblk

    @pl.when(qi == 0)
    def _():
        lane = lax.broadcasted_iota(jnp.int32, (1, LANES), 1)
        for n in range(nblk):
            k = k_ref[n * blk:(n + 1) * blk, :]
            kaug_s[0, n * blk:(n + 1) * blk, :] = jnp.where(
                lane < HEAD_DIM, k, (lane == HEAD_DIM + n).astype(F32)).astype(BF16)
            kaug_s[1, n * blk:(n + 1) * blk, :] = jnp.where(
                lane >= HEAD_DIM, k, (lane == n).astype(F32)).astype(BF16)
            kmean_s[n:n + 1, :] = jnp.mean(k, axis=0, keepdims=True)
        _transpose_to(vt_s, v_ref)

    tq = FLASH_CHUNK * blk
    qt = q_ref[...].T
    own_rows = _head_rows()
    blk_id = lax.broadcasted_iota(jnp.int32, (nblk, 1), 0)
    blk_idf = blk_id.astype(F32)
    own_blk = _own_tile(qi, blk, tq)
    past = blk_id < own_blk
    for h in range(2):
        qh = jnp.where(own_rows[h], qt, 0.0)
        gate = jnp.dot(kmean_s[...], qh, precision=lax.Precision.HIGHEST,
                       preferred_element_type=F32)
        gate = jnp.where(past, gate, NEG)
        chosen = jnp.zeros(gate.shape, jnp.bool_)
        for _ in range(MOBA_TOPK):
            mx = jnp.max(gate, axis=0, keepdims=True)
            first_idx = jnp.min(jnp.where(gate == mx, blk_idf, float(nblk)), axis=0, keepdims=True)
            pick = blk_idf == first_idx
            chosen = jnp.logical_or(chosen, pick)
            gate = jnp.where(pick, -jnp.inf, gate)
        keep = jnp.logical_or(jnp.logical_and(chosen, past), blk_id == own_blk)
        selb = jnp.where(keep, 0.0, NEG)
        qs = qt[h * HEAD_DIM:(h + 1) * HEAD_DIM, :] * (ATTN_SCALE * LOG2E)
        pad = jnp.zeros((spare, tq), F32)
        parts = [qs, selb, pad] if h == 0 else [selb, pad, qs]
        qa_s[h] = jnp.concatenate(parts, axis=0).astype(BF16)

    def bias_fn(h, kj, s, _):
        tiles = [bt_ref[h, jnp.clip(qi * FLASH_CHUNK + a - kj, 0, ntile - 1)]
                 for a in range(FLASH_CHUNK)]
        return s + jnp.concatenate(tiles, axis=1)

    _causal_rows(qi, blk, nblk, o_ref, kaug_s, vt_s, qa_s, ssc, psc, acc_s, bias_fn)


def _moba_attention(qkv, bt_b, bsz, seq, col0):
    npair = H_B // 2
    nblk = seq // MOBA_BLOCK
    ntile = bt_b.shape[1]
    assert nblk % 8 == 0 and nblk <= LANES - HEAD_DIM
    assert nblk % FLASH_CHUNK == 0
    kern = functools.partial(_moba_kernel, nblk=nblk, ntile=ntile)
    blk = MOBA_BLOCK
    tq = FLASH_CHUNK * blk

    def col(off):
        return pl.BlockSpec((None, seq, LANES), lambda p, b, i, off=off: (b, 0, off + p))

    return pl.pallas_call(
        kern,
        grid=(npair, bsz, seq // tq),
        in_specs=[pl.BlockSpec((None, tq, LANES), lambda p, b, i: (b, i, col0 + p)),
                  col(col0 + npair), col(col0 + 2 * npair),
                  pl.BlockSpec((2, ntile, blk, blk), lambda p, b, i: (p, 0, 0, 0))],
        out_specs=pl.BlockSpec((None, tq, LANES), lambda p, b, i: (b, i, p)),
        out_shape=jax.ShapeDtypeStruct((bsz, seq, W_B), F32),
        scratch_shapes=[pltpu.VMEM((2, seq, LANES), BF16), pltpu.VMEM((2, VT_ROWS, seq), BF16),
                        pltpu.VMEM((nblk, LANES), F32), pltpu.VMEM((2, LANES, tq), BF16),
                        pltpu.VMEM((2, 2 * tq, tq), F32), pltpu.VMEM((2, 2 * tq, tq), BF16),
                        pltpu.VMEM((2, VT_ROWS, tq), F32)],
        compiler_params=_cparams("parallel", "parallel", "arbitrary"),
    )(qkv, qkv, qkv, bt_b)


CUM_SPLIT = 3
CUM_CHUNK = 256
FOX_TILE = 256


def _fgate_kernel(f_ref, bf_ref, pm_ref, o_ref, cum_s):
    seq = f_ref.shape[0]
    ch = CUM_CHUNK
    i = lax.broadcasted_iota(jnp.int32, (ch, ch), 0)
    j = lax.broadcasted_iota(jnp.int32, (ch, ch), 1)
    lower = (j <= i).astype(F32)
    carry = jnp.zeros((1, LANES), F32)
    for c in range(seq // ch):
        z = f_ref[c * ch:(c + 1) * ch, :] + bf_ref[...]
        ls = jnp.minimum(z, 0.0) - jnp.log(1.0 + jnp.exp(-jnp.abs(z)))
        cum = jnp.dot(lower, ls, precision=lax.Precision.HIGHEST, preferred_element_type=F32) + carry
        cum_s[c * ch:(c + 1) * ch, :] = cum
        carry = cum[ch - 1:ch, :]
    rest = cum_s[...] * (-LOG2E)
    out = None
    for t in range(CUM_SPLIT):
        term = rest.astype(BF16)
        rest = rest - term.astype(F32)
        part = jnp.dot(term, pm_ref[t], preferred_element_type=F32)
        out = part if out is None else out + part
    o_ref[...] = out.astype(BF16)


def _forget_keys(f, b_f):
    bsz, seq, _ = f.shape
    npair = H_C // 2
    nblk = seq // FOX_TILE
    pm = np.zeros((CUM_SPLIT, LANES, npair * LANES), np.float32)
    for t in range(CUM_SPLIT):
        for p in range(npair):
            pm[t, 2 * p, p * LANES + HEAD_DIM + nblk + t] = 1.0
            pm[t, 2 * p + 1, p * LANES + nblk + t] = 1.0
    bf = jnp.pad(b_f, (0, LANES - b_f.shape[0])).reshape(1, LANES)
    return pl.pallas_call(
        _fgate_kernel,
        grid=(bsz,),
        in_specs=[pl.BlockSpec((None, seq, LANES), lambda b: (b, 0, 0)),
                  pl.BlockSpec((1, LANES), lambda b: (0, 0)),
                  pl.BlockSpec((CUM_SPLIT, LANES, npair * LANES), lambda b: (0, 0, 0))],
        out_specs=pl.BlockSpec((None, seq, npair * LANES), lambda b: (b, 0, 0)),
        out_shape=jax.ShapeDtypeStruct((bsz, seq, npair * LANES), BF16),
        scratch_shapes=[pltpu.VMEM((seq, LANES), F32)],
        compiler_params=_cparams("parallel"),
    )(f, bf, jnp.asarray(pm, BF16))


def _fox_kernel(q_ref, k_ref, v_ref, ca_ref, o_ref,
                kaug_s, vt_s, qa_s, cm_s, ssc, psc, acc_s, *, tk, nblk):
    qi = pl.program_id(2)
    spare = LANES - HEAD_DIM - nblk - 8

    @pl.when(qi == 0)
    def _():
        lane = lax.broadcasted_iota(jnp.int32, (1, LANES), 1)
        for n in range(nblk):
            k = k_ref[n * tk:(n + 1) * tk, :]
            ca = ca_ref[n * tk:(n + 1) * tk, :].astype(F32)
            kaug_s[0, n * tk:(n + 1) * tk, :] = jnp.where(
                lane < HEAD_DIM, k, ca + (lane == HEAD_DIM + n).astype(F32)).astype(BF16)
            kaug_s[1, n * tk:(n + 1) * tk, :] = jnp.where(
                lane >= HEAD_DIM, k, ca + (lane == n).astype(F32)).astype(BF16)
        _transpose_to(vt_s, v_ref)
        key = lax.broadcasted_iota(jnp.int32, (tk, tk), 0)
        qry = lax.broadcasted_iota(jnp.int32, (tk, tk), 1)
        cm_s[0] = jnp.zeros((tk, tk), F32)
        cm_s[1] = jnp.where(key <= qry, 0.0, NEG)

    tq = FLASH_CHUNK * tk
    qt = q_ref[...].T * (ATTN_SCALE * LOG2E)
    blk_id = lax.broadcasted_iota(jnp.int32, (nblk, 1), 0)
    future = jnp.where(blk_id <= _own_tile(qi, tk, tq), 0.0, NEG)
    sub = lax.broadcasted_iota(jnp.int32, (8, 1), 0)
    ones = jnp.broadcast_to(jnp.where(sub < CUM_SPLIT, 1.0, 0.0), (8, tq))
    pad = jnp.zeros((spare, tq), F32)
    for h in range(2):
        qs = qt[h * HEAD_DIM:(h + 1) * HEAD_DIM, :]
        parts = [qs, future, ones, pad] if h == 0 else [future, ones, pad, qs]
        qa_s[h] = jnp.concatenate(parts, axis=0).astype(BF16)

    def bias_fn(h, kj, s, maybe_diag):
        if not maybe_diag:
            return s
        tiles = [cm_s[jnp.where(kj == qi * FLASH_CHUNK + a, 1, 0)] for a in range(FLASH_CHUNK)]
        return s + jnp.concatenate(tiles, axis=1)

    _causal_rows(qi, tk, nblk, o_ref, kaug_s, vt_s, qa_s, ssc, psc, acc_s, bias_fn)


def _fox_attention(qkv, caug, bsz, seq, col0):
    npair = H_C // 2
    tk = FOX_TILE
    tq = FLASH_CHUNK * tk
    nblk = seq // tk
    assert nblk % 8 == 0 and nblk % FLASH_CHUNK == 0 and nblk + 8 <= LANES - HEAD_DIM
    kern = functools.partial(_fox_kernel, tk=tk, nblk=nblk)

    def col(off):
        return pl.BlockSpec((None, seq, LANES), lambda p, b, i, off=off: (b, 0, off + p))

    return pl.pallas_call(
        kern,
        grid=(npair, bsz, seq // tq),
        in_specs=[pl.BlockSpec((None, tq, LANES), lambda p, b, i: (b, i, col0 + p)),
                  col(col0 + npair), col(col0 + 2 * npair),
                  pl.BlockSpec((None, seq, LANES), lambda p, b, i: (b, 0, p))],
        out_specs=pl.BlockSpec((None, tq, LANES), lambda p, b, i: (b, i, p)),
        out_shape=jax.ShapeDtypeStruct((bsz, seq, W_C), F32),
        scratch_shapes=[pltpu.VMEM((2, seq, LANES), BF16), pltpu.VMEM((2, VT_ROWS, seq), BF16),
                        pltpu.VMEM((2, LANES, tq), BF16), pltpu.VMEM((2, tk, tk), F32),
                        pltpu.VMEM((2, 2 * tq, tq), F32), pltpu.VMEM((2, 2 * tq, tq), BF16),
                        pltpu.VMEM((2, VT_ROWS, tq), F32)],
        compiler_params=_cparams("parallel", "parallel", "arbitrary"),
    )(qkv, qkv, qkv, caug)


def _layer_norm_rows(z, g, b):
    mu = jnp.mean(z, axis=-1, keepdims=True)
    zc = z - mu
    var = jnp.mean(zc * zc, axis=-1, keepdims=True)
    return zc * lax.rsqrt(var + LN_EPS) * g + b


def _mixout_kernel(oa_ref, ob_ref, oc_ref, ga_ref, gb_ref, gc_ref, wa_ref, wb_ref, wc_ref,
                   x_ref, g1_ref, lng_ref, lnb_ref, sc_ref, sh_ref, x_out, h_out, *, alpha):
    def normed(o_ref, g_ref, rows):
        o = o_ref[rows, :]
        ms = jnp.mean(o * o, axis=-1, keepdims=True)
        return (o * lax.rsqrt(ms + LN_EPS) * g_ref[...]).astype(BF16)

    blocks = [slice(r0, r0 + MIX_ROWS) for r0 in range(0, x_ref.shape[0], MIX_ROWS)]
    ys = []
    for rows in blocks:
        y = jnp.dot(normed(oa_ref, ga_ref, rows), wa_ref[...], preferred_element_type=F32)
        y = y + jnp.dot(normed(ob_ref, gb_ref, rows), wb_ref[...], preferred_element_type=F32)
        ys.append(y + jnp.dot(normed(oc_ref, gc_ref, rows), wc_ref[...], preferred_element_type=F32))
    for rows, y in zip(blocks, ys):
        x1 = _layer_norm_rows(alpha * x_ref[rows, :] + (1.0 + g1_ref[...]) * y,
                              lng_ref[...], lnb_ref[...])
        x_out[rows, :] = x1
        h_out[rows, :] = (x1 * (1.0 + sc_ref[...]) + sh_ref[...]).astype(BF16)


MIX_ROWS = 256


def _mix_out(o_a, o_b, o_c, g_a, g_b, g_c, w_out_bf, x, g1, ln_g, ln_b, sc2, sh2, alpha):
    bsz, seq, d = x.shape
    tm = 512
    w_a, w_b, w_c = w_out_bf[:W_A], w_out_bf[W_A:W_A + W_B], w_out_bf[W_A + W_B:]

    def act(width):
        return pl.BlockSpec((None, tm, width), lambda b, i: (b, i, 0))

    def const(shape):
        return pl.BlockSpec(shape, lambda b, i: (0,) * len(shape), pipeline_mode=pl.Buffered(1))

    bvec = pl.BlockSpec((None, 1, d), lambda b, i: (b, 0, 0))
    return pl.pallas_call(
        functools.partial(_mixout_kernel, alpha=alpha),
        grid=(bsz, seq // tm),
        in_specs=[act(W_A), act(W_B), act(W_C),
                  const((1, W_A)), const((1, W_B)), const((1, W_C)),
                  const((W_A, d)), const((W_B, d)), const((W_C, d)),
                  act(d), bvec, const((1, d)), const((1, d)), bvec, bvec],
        out_specs=[act(d), act(d)],
        out_shape=[jax.ShapeDtypeStruct((bsz, seq, d), F32),
                   jax.ShapeDtypeStruct((bsz, seq, d), BF16)],
        compiler_params=_cparams("parallel", "parallel"),
    )(o_a, o_b, o_c, g_a.reshape(1, -1), g_b.reshape(1, -1), g_c.reshape(1, -1),
      w_a, w_b, w_c, x, g1, ln_g.reshape(1, d), ln_b.reshape(1, d), sc2, sh2)


FFN_HALO = 8
FFN_SUB = 256
FFN_ROWS = 512


def _ffn_up_kernel(h_ref, wa_ref, wb_ref, cwa_ref, cwb_ref, cba_ref, cbb_ref, o_ref, bufa, bufb):
    i = pl.program_id(2)
    tm, tn = o_ref.shape

    @pl.when(i == 0)
    def _():
        bufa[0:FFN_HALO, :] = jnp.zeros((FFN_HALO, tn), F32)
        bufb[0:FFN_HALO, :] = jnp.zeros((FFN_HALO, tn), F32)

    subs = [(r * FFN_ROWS, slice(c * FFN_SUB, (c + 1) * FFN_SUB))
            for r in range(tm // FFN_ROWS) for c in range(tn // FFN_SUB)]
    prods = []
    for r0, cs in subs:
        h = h_ref[r0:r0 + FFN_ROWS, :]
        prods.append((jnp.dot(h, wa_ref[:, cs], preferred_element_type=F32),
                      jnp.dot(h, wb_ref[:, cs], preferred_element_type=F32)))

    def conv(u, cw_ref, cb_ref, buf, r0, cs):
        base = FFN_HALO + r0
        buf[base:base + FFN_ROWS, cs] = u
        return (cw_ref[2:3, cs] * u
                + cw_ref[1:2, cs] * buf[base - 1:base - 1 + FFN_ROWS, cs]
                + cw_ref[0:1, cs] * buf[base - 2:base - 2 + FFN_ROWS, cs]
                + cb_ref[:, cs])

    for (r0, cs), (ua, ub) in zip(subs, prods):
        a = conv(ua, cwa_ref, cba_ref, bufa, r0, cs)
        b = conv(ub, cwb_ref, cbb_ref, bufb, r0, cs)
        o_ref[r0:r0 + FFN_ROWS, cs] = (a * (1.0 / (1.0 + jnp.exp(-a))) * b).astype(o_ref.dtype)
    for buf in (bufa, bufb):
        buf[0:FFN_HALO, :] = buf[tm:tm + FFN_HALO, :]


def _ffn_up(h, w_up_bf, layer, conv_w, conv_b):
    bsz, seq, d = h.shape
    dff = w_up_bf.shape[2] // 2
    tm, tn = 1024, 512
    nj = dff // tn
    cb = conv_b.reshape(1, -1)
    return pl.pallas_call(
        _ffn_up_kernel,
        grid=(nj, bsz, seq // tm),
        in_specs=[pl.BlockSpec((None, tm, d), lambda j, b, i: (b, i, 0)),
                  pl.BlockSpec((None, d, tn), lambda j, b, i: (layer, 0, j)),
                  pl.BlockSpec((None, d, tn), lambda j, b, i: (layer, 0, nj + j)),
                  pl.BlockSpec((CONV_WIDTH, tn), lambda j, b, i: (0, j)),
                  pl.BlockSpec((CONV_WIDTH, tn), lambda j, b, i: (0, nj + j)),
                  pl.BlockSpec((1, tn), lambda j, b, i: (0, j)),
                  pl.BlockSpec((1, tn), lambda j, b, i: (0, nj + j))],
        out_specs=pl.BlockSpec((None, tm, tn), lambda j, b, i: (b, i, j)),
        out_shape=jax.ShapeDtypeStruct((bsz, seq, dff), BF16),
        scratch_shapes=[pltpu.VMEM((tm + FFN_HALO, tn), F32)] * 2,
        compiler_params=_cparams("parallel", "parallel", "arbitrary"),
    )(h, w_up_bf, w_up_bf, conv_w, conv_w, cb, cb)


def _ffn_down_kernel(g_ref, w_ref, x_ref, g2_ref, lng_ref, lnb_ref, sc_ref, sh_ref,
                     x_out, *maybe_h_out, alpha):
    y = jnp.dot(g_ref[...], w_ref[...], preferred_element_type=F32)
    x2 = _layer_norm_rows(alpha * x_ref[...] + (1.0 + g2_ref[...]) * y, lng_ref[...], lnb_ref[...])
    x_out[...] = x2
    for h_out in maybe_h_out:
        h_out[...] = (x2 * (1.0 + sc_ref[...]) + sh_ref[...]).astype(BF16)


def _ffn_down(g, w_down_bf, layer, x, g2, ln_g, ln_b, sc_next, sh_next, alpha, with_h):
    bsz, seq, d = x.shape
    dff = g.shape[2]
    tm = 256
    act = pl.BlockSpec((None, tm, d), lambda b, i: (b, i, 0))
    bvec = pl.BlockSpec((None, 1, d), lambda b, i: (b, 0, 0))
    const = pl.BlockSpec((1, d), lambda b, i: (0, 0))
    out_specs = [act, act] if with_h else [act]
    out_shape = [jax.ShapeDtypeStruct((bsz, seq, d), F32)]
    if with_h:
        out_shape.append(jax.ShapeDtypeStruct((bsz, seq, d), BF16))
    return pl.pallas_call(
        functools.partial(_ffn_down_kernel, alpha=alpha),
        grid=(bsz, seq // tm),
        in_specs=[pl.BlockSpec((None, tm, dff), lambda b, i: (b, i, 0)),
                  pl.BlockSpec((None, dff, d), lambda b, i: (layer, 0, 0),
                               pipeline_mode=pl.Buffered(1)),
                  act, bvec, const, const, bvec, bvec],
        out_specs=out_specs,
        out_shape=out_shape,
        compiler_params=_cparams("parallel", "parallel"),
    )(g, w_down_bf, x, g2, ln_g.reshape(1, d), ln_b.reshape(1, d), sc_next, sh_next)


def kernel(x, c, rel_bias, w_ada, b_ada, w_in, b_f, g_mix_a, g_mix_b, g_mix_c, w_out,
           ln1_g, ln1_b, w_up, conv_w, conv_b, w_down, ln2_g, ln2_b):
    bsz, seq, d = x.shape
    depth = w_ada.shape[0]
    alpha = (2 * depth) ** 0.25
    n_qkv = 3 * (W_A + W_B + W_C)

    mod = _ada_mod(c, w_ada, b_ada)

    def mod_vec(l, idx):
        return mod[l, :, idx * d:(idx + 1) * d].reshape(bsz, 1, d)

    bm_a = _dilated_bias_tiles(rel_bias[:, :H_A])
    bt_b = _moba_bias_tiles(rel_bias[:, H_A:], seq)

    w_in_bf, w_up_bf, w_down_bf = w_in.astype(BF16), w_up.astype(BF16), w_down.astype(BF16)

    h = _modulate(x, mod_vec(0, 1), mod_vec(0, 0))
    for l in range(depth):
        w_f = jnp.pad(w_in_bf[l, :, n_qkv:], ((0, 0), (0, LANES - H_C)))
        qkv, f = _inproj(h.reshape(bsz * seq, d), w_in_bf, l, n_qkv, w_f)
        qkv = qkv.reshape(bsz, seq, n_qkv)
        caug = _forget_keys(f.reshape(bsz, seq, LANES), b_f[l])

        o_a = _dilated_attention(qkv, bm_a, bsz, seq)
        o_b = _moba_attention(qkv, bt_b, bsz, seq, 3 * W_A // LANES)
        o_c = _fox_attention(qkv, caug, bsz, seq, 3 * (W_A + W_B) // LANES)

        x, h = _mix_out(o_a, o_b, o_c, g_mix_a[l], g_mix_b[l], g_mix_c[l], w_out[l].astype(BF16),
                        x, mod_vec(l, 2), ln1_g[l], ln1_b[l], mod_vec(l, 4), mod_vec(l, 3), alpha)
        g = _ffn_up(h, w_up_bf, l, conv_w[l], conv_b[l])
        last = l == depth - 1
        nxt = min(l + 1, depth - 1)
        outs = _ffn_down(g, w_down_bf, l, x, mod_vec(l, 5), ln2_g[l], ln2_b[l],
                         mod_vec(nxt, 1), mod_vec(nxt, 0), alpha, with_h=not last)
        x = outs[0]
        if not last:
            h = outs[1]
    return x
```

```python
import functools
import math

import numpy as np
import jax
import jax.numpy as jnp
from jax import lax
from jax.experimental import pallas as pl
from jax.experimental.pallas import tpu as pltpu

F32 = jnp.float32
BF16 = jnp.bfloat16

HEAD_DIM = 64
LANES = 128
H_A, H_B, H_C = 12, 10, 10
W_A, W_B, W_C = H_A * HEAD_DIM, H_B * HEAD_DIM, H_C * HEAD_DIM
DIL_PATTERNS = ((128, 1), (512, 4), (2048, 16))
DIL_W = 128
MOBA_BLOCK = 256
MOBA_TOPK = 3
N_BUCKETS = 32
REL_MAX_DIST = 2048
CONV_WIDTH = 3
LN_EPS = 1e-5
NEG = -1e30
ATTN_SCALE = HEAD_DIM ** -0.5
LOG2E = math.log2(math.e)
VMEM_LIMIT = 56 * 1024 * 1024


def _cparams(*sem):
    return pltpu.CompilerParams(dimension_semantics=sem, vmem_limit_bytes=VMEM_LIMIT)


def _t5_bucket(dist):
    n = jnp.maximum(dist, 0)
    max_exact = N_BUCKETS // 2
    nf = jnp.maximum(n, 1).astype(F32)
    large = max_exact + (jnp.log(nf / max_exact) / math.log(REL_MAX_DIST / max_exact)
                         * (N_BUCKETS - max_exact)).astype(jnp.int32)
    large = jnp.minimum(large, N_BUCKETS - 1)
    return jnp.where(n < max_exact, n, large)


def _ada_kernel(c_ref, w_ref, b_ref, o_ref):
    c = c_ref[...]
    cond = c * (1.0 / (1.0 + jnp.exp(-c)))
    o_ref[...] = jnp.dot(cond, w_ref[...], precision=lax.Precision.HIGHEST,
                         preferred_element_type=F32) + b_ref[...]


def _ada_mod(c, w_ada, b_ada):
    depth, d, n = w_ada.shape
    bsz = c.shape[0]
    tn = 1024
    return pl.pallas_call(
        _ada_kernel,
        grid=(depth, n // tn),
        in_specs=[pl.BlockSpec((bsz, d), lambda l, j: (0, 0)),
                  pl.BlockSpec((None, d, tn), lambda l, j: (l, 0, j)),
                  pl.BlockSpec((None, 1, tn), lambda l, j: (l, 0, j))],
        out_specs=pl.BlockSpec((None, bsz, tn), lambda l, j: (l, 0, j)),
        out_shape=jax.ShapeDtypeStruct((depth, bsz, n), F32),
        compiler_params=_cparams("parallel", "parallel"),
    )(c, w_ada, b_ada.reshape(depth, 1, n))


def _modulate_kernel(x_ref, sc_ref, sh_ref, o_ref):
    o_ref[...] = (x_ref[...] * (1.0 + sc_ref[...]) + sh_ref[...]).astype(o_ref.dtype)


def _modulate(x, sc, sh):
    bsz, s, d = x.shape
    tm = 512
    vec = pl.BlockSpec((None, 1, d), lambda b, i: (b, 0, 0))
    return pl.pallas_call(
        _modulate_kernel,
        grid=(bsz, s // tm),
        in_specs=[pl.BlockSpec((None, tm, d), lambda b, i: (b, i, 0)), vec, vec],
        out_specs=pl.BlockSpec((None, tm, d), lambda b, i: (b, i, 0)),
        out_shape=jax.ShapeDtypeStruct((bsz, s, d), BF16),
        compiler_params=_cparams("parallel", "parallel"),
    )(x, sc, sh)


def _inproj_kernel(a_ref, w_ref, wf_ref, o_ref, f_ref):
    a = a_ref[...]
    o_ref[...] = jnp.dot(a, w_ref[...], preferred_element_type=F32)

    @pl.when(pl.program_id(1) == 0)
    def _():
        f_ref[...] = jnp.dot(a, wf_ref[...], preferred_element_type=F32)


def _inproj(h, w_in_bf, layer, n, w_f):
    t, d = h.shape
    tm, tn = 1024, 1024
    return pl.pallas_call(
        _inproj_kernel,
        grid=(t // tm, n // tn),
        in_specs=[pl.BlockSpec((tm, d), lambda i, j: (i, 0)),
                  pl.BlockSpec((None, d, tn), lambda i, j: (layer, 0, j)),
                  pl.BlockSpec((d, LANES), lambda i, j: (0, 0))],
        out_specs=[pl.BlockSpec((tm, tn), lambda i, j: (i, j)),
                   pl.BlockSpec((tm, LANES), lambda i, j: (i, 0))],
        out_shape=[jax.ShapeDtypeStruct((t, n), F32),
                   jax.ShapeDtypeStruct((t, LANES), F32)],
        compiler_params=_cparams("parallel", "arbitrary"),
    )(h, w_in_bf, w_f)


TOEP_LEN = 512
TOEP_ROWS = 256


TOEP_BATCH = 8


def _toeplitz_kernel(rr_ref, o_ref):
    for t in range(TOEP_BATCH):
        x = jnp.broadcast_to(rr_ref[t], (TOEP_ROWS, TOEP_LEN))
        y = pltpu.roll(x, TOEP_LEN - TOEP_ROWS + 1, 1, stride=1, stride_axis=0)
        o_ref[t] = y[:, :o_ref.shape[2]]


def _toeplitz(rr, cols):
    g = rr.shape[0]
    assert g % TOEP_BATCH == 0
    return pl.pallas_call(
        _toeplitz_kernel,
        grid=(g // TOEP_BATCH,),
        in_specs=[pl.BlockSpec((TOEP_BATCH, 1, TOEP_LEN), lambda i: (i, 0, 0))],
        out_specs=pl.BlockSpec((TOEP_BATCH, TOEP_ROWS, cols), lambda i: (i, 0, 0)),
        out_shape=jax.ShapeDtypeStruct((g, TOEP_ROWS, cols), F32),
        compiler_params=_cparams("parallel"),
    )(rr.reshape(g, 1, TOEP_LEN))


def _bias_of_distance(rel, dist, valid):
    bias = jnp.moveaxis(rel[_t5_bucket(dist)], -1, 0)
    return jnp.where(valid, bias * LOG2E, NEG).astype(F32)


def _dilated_bias_tiles(rel_a):
    c = jnp.arange(TOEP_LEN)
    rows = []
    for (_, dil) in DIL_PATTERNS:
        d0 = c - (TOEP_ROWS - 1) + DIL_W
        d1 = c - (TOEP_ROWS - 1)
        rows.append(jnp.stack([_bias_of_distance(rel_a, d0 * dil, (d0 >= 0) & (d0 <= DIL_W)),
                               _bias_of_distance(rel_a, d1 * dil, d1 >= 0)]))
    rr = jnp.stack(rows)
    out = _toeplitz(rr.reshape(-1, TOEP_LEN), DIL_W)
    return out.reshape(len(DIL_PATTERNS), 2, H_A, TOEP_ROWS, DIL_W)


def _moba_num_tiles(seq):
    blk = MOBA_BLOCK
    d_all = np.arange(max(seq, 2))
    nf = np.maximum(d_all, 1).astype(np.float32)
    half = N_BUCKETS // 2
    big = half + (np.log(nf / half) / math.log(REL_MAX_DIST / half) * (N_BUCKETS - half)).astype(np.int32)
    bucket = np.where(d_all < half, d_all, np.minimum(big, N_BUCKETS - 1))
    last_change = int(np.max(np.nonzero(np.diff(bucket))[0])) + 1
    first_const = min(seq // blk - 1, -(-(last_change + 2 + blk - 1) // blk))
    return first_const + 1


def _moba_bias_tiles(rel_b, seq):
    ntile = _moba_num_tiles(seq)
    c = jnp.arange(TOEP_LEN)[None, :]
    d = jnp.arange(ntile)[:, None] * MOBA_BLOCK + c - (TOEP_ROWS - 1)
    rr = _bias_of_distance(rel_b, d, d >= 0)
    out = _toeplitz(rr.reshape(-1, TOEP_LEN), MOBA_BLOCK)
    return out.reshape(H_B, ntile, TOEP_ROWS, MOBA_BLOCK)


def _head_rows():
    row = lax.broadcasted_iota(jnp.int32, (LANES, 1), 0)
    return row < HEAD_DIM, row >= HEAD_DIM


DIL_UNROLL = 4


def _dilated_kernel(q_ref, k_ref, v_ref, bm_ref, o_ref, og_s, lse_s, sblk_s, pblk_s, vts_s, *, seq):
    w = DIL_W
    own_rows = _head_rows()
    for g, (_, dil) in enumerate(DIL_PATTERNS):
        nb = (seq // dil) // w

        def rows(start, size, dil=dil):
            return pl.ds(start, size) if dil == 1 else pl.ds(start, size, stride=dil)

        def trip(it, carry, g=g, dil=dil, nb=nb, rows=rows):
            metas = []
            k = None
            for u in range(DIL_UNROLL):
                idx = it * DIL_UNROLL + u
                r = idx // nb
                n = idx % nb
                q_rows = rows(n * w * dil + r, w)
                var = jnp.where(n == 0, 1, 0)
                qt = q_ref[q_rows, :].T * (ATTN_SCALE * LOG2E)
                kv_slot = u - 1 if (nb == 2 and u % 2 == 1) else u
                if kv_slot == u:
                    kv_rows = rows(jnp.maximum(n - 1, 0) * w * dil + r, 2 * w)
                    k = k_ref[kv_rows, :].astype(BF16)
                    vt = v_ref[kv_rows, :].T
                    for h in range(2):
                        vts_s[u, h] = jnp.concatenate(
                            [vt[h * HEAD_DIM:(h + 1) * HEAD_DIM, :], _ones_rows(2 * w)],
                            axis=0).astype(BF16)
                tops = []
                for h in range(2):
                    s = jnp.dot(k, jnp.where(own_rows[h], qt, 0.0).astype(BF16),
                                preferred_element_type=F32) + bm_ref[g, var, h]
                    sblk_s[u, h] = s
                    tops.append(jnp.max(s, axis=0, keepdims=True))
                metas.append((q_rows, tops, kv_slot))
            pvs = []
            for u, (_, tops, kv_slot) in enumerate(metas):
                for h in range(2):
                    pblk_s[u, h] = jnp.exp2(sblk_s[u, h] - tops[h]).astype(BF16)
                    pvs.append(jnp.dot(vts_s[kv_slot, h], pblk_s[u, h], preferred_element_type=F32))
            for u, (q_rows, tops, _) in enumerate(metas):
                pair = pvs[2 * u:2 * u + 2]
                o_t = jnp.concatenate([pv[:HEAD_DIM] / pv[HEAD_DIM:HEAD_DIM + 1] for pv in pair], axis=0)
                lse_t = jnp.concatenate(
                    [jnp.broadcast_to(m + jnp.log(pv[HEAD_DIM:HEAD_DIM + 1]) * LOG2E, (HEAD_DIM, w))
                     for m, pv in zip(tops, pair)], axis=0)
                og_s[g, q_rows, :] = o_t.T
                lse_s[g, q_rows, :] = lse_t.T
            return carry

        lax.fori_loop(0, nb * dil // DIL_UNROLL, trip, 0)

    top = jnp.maximum(jnp.maximum(lse_s[0], lse_s[1]), lse_s[2])
    num = jnp.zeros_like(top)
    den = jnp.zeros_like(top)
    for g in range(len(DIL_PATTERNS)):
        wg = jnp.exp2(lse_s[g] - top)
        num = num + wg * og_s[g]
        den = den + wg
    o_ref[...] = num / den


def _dilated_attention(qkv, bm_a, bsz, seq):
    npair = H_A // 2
    npat = len(DIL_PATTERNS)
    kern = functools.partial(_dilated_kernel, seq=seq)

    def col(off):
        return pl.BlockSpec((None, seq, LANES), lambda p, b, off=off: (b, 0, off + p))

    return pl.pallas_call(
        kern,
        grid=(npair, bsz),
        in_specs=[col(0), col(npair), col(2 * npair),
                  pl.BlockSpec((npat, 2, 2, 2 * DIL_W, DIL_W), lambda p, b: (0, 0, p, 0, 0))],
        out_specs=pl.BlockSpec((None, seq, LANES), lambda p, b: (b, 0, p)),
        out_shape=jax.ShapeDtypeStruct((bsz, seq, W_A), F32),
        scratch_shapes=[pltpu.VMEM((npat, seq, LANES), F32)] * 2 + [
            pltpu.VMEM((DIL_UNROLL, 2, 2 * DIL_W, DIL_W), F32),
            pltpu.VMEM((DIL_UNROLL, 2, 2 * DIL_W, DIL_W), BF16),
            pltpu.VMEM((DIL_UNROLL, 2, VT_ROWS, 2 * DIL_W), BF16)],
        compiler_params=_cparams("parallel", "parallel"),
    )(qkv, qkv, qkv, bm_a)


VT_ROWS = HEAD_DIM + 16


def _ones_rows(width):
    row = lax.broadcasted_iota(jnp.int32, (VT_ROWS - HEAD_DIM, width), 0)
    return jnp.where(row == 0, 1.0, 0.0)


def _transpose_to(dst_ref, src_ref, chunk=512):
    for c in range(src_ref.shape[0] // chunk):
        cols = slice(c * chunk, (c + 1) * chunk)
        t = src_ref[cols, :].T
        for h in range(2):
            dst_ref[h, :, cols] = jnp.concatenate(
                [t[h * HEAD_DIM:(h + 1) * HEAD_DIM, :], _ones_rows(chunk)], axis=0).astype(BF16)


FLASH_CHUNK = 2


def _own_tile(qi, tk, tq):
    col = lax.broadcasted_iota(jnp.int32, (1, tq), 1)
    return qi * (tq // tk) + col // tk


def _causal_rows(qi, tk, nblk, o_ref, kaug_s, vt_s, qa_s, ssc, psc, acc_s, bias_fn):
    ck = FLASH_CHUNK * tk
    tq = ck

    def stage(c, u):
        r0 = (c % 2) * ck + u * tk
        return slice(r0, r0 + tk)

    def scores(c, h, maybe_diag):
        top = None
        for u in range(FLASH_CHUNK):
            kj = c * FLASH_CHUNK + u
            s = jnp.dot(kaug_s[h, kj * tk:(kj + 1) * tk, :], qa_s[h], preferred_element_type=F32)
            s = bias_fn(h, kj, s, maybe_diag)
            ssc[h, stage(c, u), :] = s
            cmax = jnp.max(s, axis=0, keepdims=True)
            top = cmax if top is None else jnp.maximum(top, cmax)
        return top

    def values(c, h, top, m):
        for u in range(FLASH_CHUNK):
            psc[h, stage(c, u), :] = jnp.exp2(ssc[h, stage(c, u), :] - top).astype(BF16)
        slot = slice((c % 2) * ck, (c % 2 + 1) * ck)
        pv = jnp.dot(vt_s[h, :, c * ck:(c + 1) * ck], psc[h, slot, :], preferred_element_type=F32)
        acc_s[h] = pv if m is None else jnp.exp2(m - top) * acc_s[h] + pv

    def row_pass(nchunk):
        tops = {(0, h): scores(0, h, nchunk == 1) for h in range(2)}
        for c in range(nchunk):
            if c + 1 < nchunk:
                for h in range(2):
                    tops[(c + 1, h)] = jnp.maximum(tops[(c, h)], scores(c + 1, h, c + 2 == nchunk))
            for h in range(2):
                values(c, h, tops[(c, h)], tops[(c - 1, h)] if c else None)
        return jnp.concatenate([acc_s[h, :HEAD_DIM] / acc_s[h, HEAD_DIM:HEAD_DIM + 1]
                                for h in range(2)], axis=0).T

    for nchunk in range(1, nblk // FLASH_CHUNK + 1):
        @pl.when(qi == nchunk - 1)
        def _(nchunk=nchunk):
            o_ref[...] = row_pass(nchunk)


def _moba_kernel(q_ref, k_ref, v_ref, bt_ref, o_ref,
                 kaug_s, vt_s, kmean_s, qa_s, ssc, psc, acc_s, *, nblk, ntile):
    blk = MOBA_BLOCK
    qi = pl.program_id(2)
    spare = LANES - HEAD_DIM - nblk

    @pl.when(qi == 0)
    def _():
        lane = lax.broadcasted_iota(jnp.int32, (1, LANES), 1)
        for n in range(nblk):
            k = k_ref[n * blk:(n + 1) * blk, :]
            kaug_s[0, n * blk:(n + 1) * blk, :] = jnp.where(
                lane < HEAD_DIM, k, (lane == HEAD_DIM + n).astype(F32)).astype(BF16)
            kaug_s[1, n * blk:(n + 1) * blk, :] = jnp.where(
                lane >= HEAD_DIM, k, (lane == n).astype(F32)).astype(BF16)
            kmean_s[n:n + 1, :] = jnp.mean(k, axis=0, keepdims=True)
        _transpose_to(vt_s, v_ref)

    tq = FLASH_CHUNK * blk
    qt = q_ref[...].T
    own_rows = _head_rows()
    blk_id = lax.broadcasted_iota(jnp.int32, (nblk, 1), 0)
    blk_idf = blk_id.astype(F32)
    own_blk = _own_tile(qi, blk, tq)
    past = blk_id < own_blk
    for h in range(2):
        qh = jnp.where(own_rows[h], qt, 0.0)
        gate = jnp.dot(kmean_s[...], qh, precision=lax.Precision.HIGHEST,
                       preferred_element_type=F32)
        gate = jnp.where(past, gate, NEG)
        chosen = jnp.zeros(gate.shape, jnp.bool_)
        for _ in range(MOBA_TOPK):
            mx = jnp.max(gate, axis=0, keepdims=True)
            first_idx = jnp.min(jnp.where(gate == mx, blk_idf, float(nblk)), axis=0, keepdims=True)
            pick = blk_idf == first_idx
            chosen = jnp.logical_or(chosen, pick)
            gate = jnp.where(pick, -jnp.inf, gate)
        keep = jnp.logical_or(jnp.logical_and(chosen, past), blk_id == own_blk)
        selb = jnp.where(keep, 0.0, NEG)
        qs = qt[h * HEAD_DIM:(h + 1) * HEAD_DIM, :] * (ATTN_SCALE * LOG2E)
        pad = jnp.zeros((spare, tq), F32)
        parts = [qs, selb, pad] if h == 0 else [selb, pad, qs]
        qa_s[h] = jnp.concatenate(parts, axis=0).astype(BF16)

    def bias_fn(h, kj, s, _):
        tiles = [bt_ref[h, jnp.clip(qi * FLASH_CHUNK + a - kj, 0, ntile - 1)]
                 for a in range(FLASH_CHUNK)]
        return s + jnp.concatenate(tiles, axis=1)

    _causal_rows(qi, blk, nblk, o_ref, kaug_s, vt_s, qa_s, ssc, psc, acc_s, bias_fn)


def _moba_attention(qkv, bt_b, bsz, seq, col0):
    npair = H_B // 2
    nblk = seq // MOBA_BLOCK
    ntile = bt_b.shape[1]
    assert nblk % 8 == 0 and nblk <= LANES - HEAD_DIM
    assert nblk % FLASH_CHUNK == 0
    kern = functools.partial(_moba_kernel, nblk=nblk, ntile=ntile)
    blk = MOBA_BLOCK
    tq = FLASH_CHUNK * blk

    def col(off):
        return pl.BlockSpec((None, seq, LANES), lambda p, b, i, off=off: (b, 0, off + p))

    return pl.pallas_call(
        kern,
        grid=(npair, bsz, seq // tq),
        in_specs=[pl.BlockSpec((None, tq, LANES), lambda p, b, i: (b, i, col0 + p)),
                  col(col0 + npair), col(col0 + 2 * npair),
                  pl.BlockSpec((2, ntile, blk, blk), lambda p, b, i: (p, 0, 0, 0))],
        out_specs=pl.BlockSpec((None, tq, LANES), lambda p, b, i: (b, i, p)),
        out_shape=jax.ShapeDtypeStruct((bsz, seq, W_B), F32),
        scratch_shapes=[pltpu.VMEM((2, seq, LANES), BF16), pltpu.VMEM((2, VT_ROWS, seq), BF16),
                        pltpu.VMEM((nblk, LANES), F32), pltpu.VMEM((2, LANES, tq), BF16),
                        pltpu.VMEM((2, 2 * tq, tq), F32), pltpu.VMEM((2, 2 * tq, tq), BF16),
                        pltpu.VMEM((2, VT_ROWS, tq), F32)],
        compiler_params=_cparams("parallel", "parallel", "arbitrary"),
    )(qkv, qkv, qkv, bt_b)


CUM_SPLIT = 3
CUM_CHUNK = 256
FOX_TILE = 256


def _fgate_kernel(f_ref, bf_ref, pm_ref, o_ref, cum_s):
    seq = f_ref.shape[0]
    ch = CUM_CHUNK
    i = lax.broadcasted_iota(jnp.int32, (ch, ch), 0)
    j = lax.broadcasted_iota(jnp.int32, (ch, ch), 1)
    lower = (j <= i).astype(F32)
    carry = jnp.zeros((1, LANES), F32)
    for c in range(seq // ch):
        z = f_ref[c * ch:(c + 1) * ch, :] + bf_ref[...]
        ls = jnp.minimum(z, 0.0) - jnp.log(1.0 + jnp.exp(-jnp.abs(z)))
        cum = jnp.dot(lower, ls, precision=lax.Precision.HIGHEST, preferred_element_type=F32) + carry
        cum_s[c * ch:(c + 1) * ch, :] = cum
        carry = cum[ch - 1:ch, :]
    rest = cum_s[...] * (-LOG2E)
    out = None
    for t in range(CUM_SPLIT):
        term = rest.astype(BF16)
        rest = rest - term.astype(F32)
        part = jnp.dot(term, pm_ref[t], preferred_element_type=F32)
        out = part if out is None else out + part
    o_ref[...] = out.astype(BF16)


def _forget_keys(f, b_f):
    bsz, seq, _ = f.shape
    npair = H_C // 2
    nblk = seq // FOX_TILE
    pm = np.zeros((CUM_SPLIT, LANES, npair * LANES), np.float32)
    for t in range(CUM_SPLIT):
        for p in range(npair):
            pm[t, 2 * p, p * LANES + HEAD_DIM + nblk + t] = 1.0
            pm[t, 2 * p + 1, p * LANES + nblk + t] = 1.0
    bf = jnp.pad(b_f, (0, LANES - b_f.shape[0])).reshape(1, LANES)
    return pl.pallas_call(
        _fgate_kernel,
        grid=(bsz,),
        in_specs=[pl.BlockSpec((None, seq, LANES), lambda b: (b, 0, 0)),
                  pl.BlockSpec((1, LANES), lambda b: (0, 0)),
                  pl.BlockSpec((CUM_SPLIT, LANES, npair * LANES), lambda b: (0, 0, 0))],
        out_specs=pl.BlockSpec((None, seq, npair * LANES), lambda b: (b, 0, 0)),
        out_shape=jax.ShapeDtypeStruct((bsz, seq, npair * LANES), BF16),
        scratch_shapes=[pltpu.VMEM((seq, LANES), F32)],
        compiler_params=_cparams("parallel"),
    )(f, bf, jnp.asarray(pm, BF16))


def _fox_kernel(q_ref, k_ref, v_ref, ca_ref, o_ref,
                kaug_s, vt_s, qa_s, cm_s, ssc, psc, acc_s, *, tk, nblk):
    qi = pl.program_id(2)
    spare = LANES - HEAD_DIM - nblk - 8

    @pl.when(qi == 0)
    def _():
        lane = lax.broadcasted_iota(jnp.int32, (1, LANES), 1)
        for n in range(nblk):
            k = k_ref[n * tk:(n + 1) * tk, :]
            ca = ca_ref[n * tk:(n + 1) * tk, :].astype(F32)
            kaug_s[0, n * tk:(n + 1) * tk, :] = jnp.where(
                lane < HEAD_DIM, k, ca + (lane == HEAD_DIM + n).astype(F32)).astype(BF16)
            kaug_s[1, n * tk:(n + 1) * tk, :] = jnp.where(
                lane >= HEAD_DIM, k, ca + (lane == n).astype(F32)).astype(BF16)
        _transpose_to(vt_s, v_ref)
        key = lax.broadcasted_iota(jnp.int32, (tk, tk), 0)
        qry = lax.broadcasted_iota(jnp.int32, (tk, tk), 1)
        cm_s[0] = jnp.zeros((tk, tk), F32)
        cm_s[1] = jnp.where(key <= qry, 0.0, NEG)

    tq = FLASH_CHUNK * tk
    qt = q_ref[...].T * (ATTN_SCALE * LOG2E)
    blk_id = lax.broadcasted_iota(jnp.int32, (nblk, 1), 0)
    future = jnp.where(blk_id <= _own_tile(qi, tk, tq), 0.0, NEG)
    sub = lax.broadcasted_iota(jnp.int32, (8, 1), 0)
    ones = jnp.broadcast_to(jnp.where(sub < CUM_SPLIT, 1.0, 0.0), (8, tq))
    pad = jnp.zeros((spare, tq), F32)
    for h in range(2):
        qs = qt[h * HEAD_DIM:(h + 1) * HEAD_DIM, :]
        parts = [qs, future, ones, pad] if h == 0 else [future, ones, pad, qs]
        qa_s[h] = jnp.concatenate(parts, axis=0).astype(BF16)

    def bias_fn(h, kj, s, maybe_diag):
        if not maybe_diag:
            return s
        tiles = [cm_s[jnp.where(kj == qi * FLASH_CHUNK + a, 1, 0)] for a in range(FLASH_CHUNK)]
        return s + jnp.concatenate(tiles, axis=1)

    _causal_rows(qi, tk, nblk, o_ref, kaug_s, vt_s, qa_s, ssc, psc, acc_s, bias_fn)


def _fox_attention(qkv, caug, bsz, seq, col0):
    npair = H_C // 2
    tk = FOX_TILE
    tq = FLASH_CHUNK * tk
    nblk = seq // tk
    assert nblk % 8 == 0 and nblk % FLASH_CHUNK == 0 and nblk + 8 <= LANES - HEAD_DIM
    kern = functools.partial(_fox_kernel, tk=tk, nblk=nblk)

    def col(off):
        return pl.BlockSpec((None, seq, LANES), lambda p, b, i, off=off: (b, 0, off + p))

    return pl.pallas_call(
        kern,
        grid=(npair, bsz, seq // tq),
        in_specs=[pl.BlockSpec((None, tq, LANES), lambda p, b, i: (b, i, col0 + p)),
                  col(col0 + npair), col(col0 + 2 * npair),
                  pl.BlockSpec((None, seq, LANES), lambda p, b, i: (b, 0, p))],
        out_specs=pl.BlockSpec((None, tq, LANES), lambda p, b, i: (b, i, p)),
        out_shape=jax.ShapeDtypeStruct((bsz, seq, W_C), F32),
        scratch_shapes=[pltpu.VMEM((2, seq, LANES), BF16), pltpu.VMEM((2, VT_ROWS, seq), BF16),
                        pltpu.VMEM((2, LANES, tq), BF16), pltpu.VMEM((2, tk, tk), F32),
                        pltpu.VMEM((2, 2 * tq, tq), F32), pltpu.VMEM((2, 2 * tq, tq), BF16),
                        pltpu.VMEM((2, VT_ROWS, tq), F32)],
        compiler_params=_cparams("parallel", "parallel", "arbitrary"),
    )(qkv, qkv, qkv, caug)


def _layer_norm_rows(z, g, b):
    mu = jnp.mean(z, axis=-1, keepdims=True)
    zc = z - mu
    var = jnp.mean(zc * zc, axis=-1, keepdims=True)
    return zc * lax.rsqrt(var + LN_EPS) * g + b


def _mixout_kernel(oa_ref, ob_ref, oc_ref, ga_ref, gb_ref, gc_ref, wa_ref, wb_ref, wc_ref,
                   x_ref, g1_ref, lng_ref, lnb_ref, sc_ref, sh_ref, x_out, h_out, *, alpha):
    def normed(o_ref, g_ref, rows):
        o = o_ref[rows, :]
        ms = jnp.mean(o * o, axis=-1, keepdims=True)
        return (o * lax.rsqrt(ms + LN_EPS) * g_ref[...]).astype(BF16)

    blocks = [slice(r0, r0 + MIX_ROWS) for r0 in range(0, x_ref.shape[0], MIX_ROWS)]
    ys = []
    for rows in blocks:
        y = jnp.dot(normed(oa_ref, ga_ref, rows), wa_ref[...], preferred_element_type=F32)
        y = y + jnp.dot(normed(ob_ref, gb_ref, rows), wb_ref[...], preferred_element_type=F32)
        ys.append(y + jnp.dot(normed(oc_ref, gc_ref, rows), wc_ref[...], preferred_element_type=F32))
    for rows, y in zip(blocks, ys):
        x1 = _layer_norm_rows(alpha * x_ref[rows, :] + (1.0 + g1_ref[...]) * y,
                              lng_ref[...], lnb_ref[...])
        x_out[rows, :] = x1
        h_out[rows, :] = (x1 * (1.0 + sc_ref[...]) + sh_ref[...]).astype(BF16)


MIX_ROWS = 256


def _mix_out(o_a, o_b, o_c, g_a, g_b, g_c, w_out_bf, x, g1, ln_g, ln_b, sc2, sh2, alpha):
    bsz, seq, d = x.shape
    tm = 512
    w_a, w_b, w_c = w_out_bf[:W_A], w_out_bf[W_A:W_A + W_B], w_out_bf[W_A + W_B:]

    def act(width):
        return pl.BlockSpec((None, tm, width), lambda b, i: (b, i, 0))

    def const(shape):
        return pl.BlockSpec(shape, lambda b, i: (0,) * len(shape), pipeline_mode=pl.Buffered(1))

    bvec = pl.BlockSpec((None, 1, d), lambda b, i: (b, 0, 0))
    return pl.pallas_call(
        functools.partial(_mixout_kernel, alpha=alpha),
        grid=(bsz, seq // tm),
        in_specs=[act(W_A), act(W_B), act(W_C),
                  const((1, W_A)), const((1, W_B)), const((1, W_C)),
                  const((W_A, d)), const((W_B, d)), const((W_C, d)),
                  act(d), bvec, const((1, d)), const((1, d)), bvec, bvec],
        out_specs=[act(d), act(d)],
        out_shape=[jax.ShapeDtypeStruct((bsz, seq, d), F32),
                   jax.ShapeDtypeStruct((bsz, seq, d), BF16)],
        compiler_params=_cparams("parallel", "parallel"),
    )(o_a, o_b, o_c, g_a.reshape(1, -1), g_b.reshape(1, -1), g_c.reshape(1, -1),
      w_a, w_b, w_c, x, g1, ln_g.reshape(1, d), ln_b.reshape(1, d), sc2, sh2)


FFN_HALO = 8
FFN_SUB = 256
FFN_ROWS = 256


def _ffn_up_kernel(h_ref, wa_ref, wb_ref, cwa_ref, cwb_ref, cba_ref, cbb_ref, o_ref,
                   bufa, bufb, wa_bf, wb_bf):
    i = pl.program_id(2)
    tm, tn = o_ref.shape

    @pl.when(jnp.logical_and(pl.program_id(1) == 0, i == 0))
    def _():
        wa_bf[...] = wa_ref[...].astype(BF16)
        wb_bf[...] = wb_ref[...].astype(BF16)

    @pl.when(i == 0)
    def _():
        bufa[0:FFN_HALO, :] = jnp.zeros((FFN_HALO, tn), F32)
        bufb[0:FFN_HALO, :] = jnp.zeros((FFN_HALO, tn), F32)

    subs = [(r * FFN_ROWS, slice(c * FFN_SUB, (c + 1) * FFN_SUB))
            for r in range(tm // FFN_ROWS) for c in range(tn // FFN_SUB)]
    prods = []
    for r0, cs in subs:
        h = h_ref[r0:r0 + FFN_ROWS, :]
        prods.append((jnp.dot(h, wa_bf[:, cs], preferred_element_type=F32),
                      jnp.dot(h, wb_bf[:, cs], preferred_element_type=F32)))

    def conv(u, cw_ref, cb_ref, buf, r0, cs):
        base = FFN_HALO + r0
        buf[base:base + FFN_ROWS, cs] = u
        return (cw_ref[2:3, cs] * u
                + cw_ref[1:2, cs] * buf[base - 1:base - 1 + FFN_ROWS, cs]
                + cw_ref[0:1, cs] * buf[base - 2:base - 2 + FFN_ROWS, cs]
                + cb_ref[:, cs])

    for (r0, cs), (ua, ub) in zip(subs, prods):
        a = conv(ua, cwa_ref, cba_ref, bufa, r0, cs)
        b = conv(ub, cwb_ref, cbb_ref, bufb, r0, cs)
        o_ref[r0:r0 + FFN_ROWS, cs] = (a * (1.0 / (1.0 + jnp.exp(-a))) * b).astype(o_ref.dtype)
    for buf in (bufa, bufb):
        buf[0:FFN_HALO, :] = buf[tm:tm + FFN_HALO, :]


def _ffn_up(h, w_up, layer, conv_w, conv_b):
    bsz, seq, d = h.shape
    dff = w_up.shape[2] // 2
    tm, tn = 1024, 512
    nj = dff // tn
    cb = conv_b.reshape(1, -1)
    return pl.pallas_call(
        _ffn_up_kernel,
        grid=(nj, bsz, seq // tm),
        in_specs=[pl.BlockSpec((None, tm, d), lambda j, b, i: (b, i, 0)),
                  pl.BlockSpec((None, d, tn), lambda j, b, i: (layer, 0, j)),
                  pl.BlockSpec((None, d, tn), lambda j, b, i: (layer, 0, nj + j)),
                  pl.BlockSpec((CONV_WIDTH, tn), lambda j, b, i: (0, j)),
                  pl.BlockSpec((CONV_WIDTH, tn), lambda j, b, i: (0, nj + j)),
                  pl.BlockSpec((1, tn), lambda j, b, i: (0, j)),
                  pl.BlockSpec((1, tn), lambda j, b, i: (0, nj + j))],
        out_specs=pl.BlockSpec((None, tm, tn), lambda j, b, i: (b, i, j)),
        out_shape=jax.ShapeDtypeStruct((bsz, seq, dff), BF16),
        scratch_shapes=[pltpu.VMEM((tm + FFN_HALO, tn), F32)] * 2 + [pltpu.VMEM((d, tn), BF16)] * 2,
        compiler_params=_cparams("parallel", "arbitrary", "arbitrary"),
    )(h, w_up, w_up, conv_w, conv_w, cb, cb)


def _ffn_down_kernel(g_ref, w_ref, x_ref, g2_ref, lng_ref, lnb_ref, sc_ref, sh_ref,
                     x_out, *maybe_h_out, alpha):
    y = jnp.dot(g_ref[...], w_ref[...], preferred_element_type=F32)
    x2 = _layer_norm_rows(alpha * x_ref[...] + (1.0 + g2_ref[...]) * y, lng_ref[...], lnb_ref[...])
    x_out[...] = x2
    for h_out in maybe_h_out:
        h_out[...] = (x2 * (1.0 + sc_ref[...]) + sh_ref[...]).astype(BF16)


def _ffn_down(g, w_down_bf, layer, x, g2, ln_g, ln_b, sc_next, sh_next, alpha, with_h):
    bsz, seq, d = x.shape
    dff = g.shape[2]
    tm = 256
    act = pl.BlockSpec((None, tm, d), lambda b, i: (b, i, 0))
    bvec = pl.BlockSpec((None, 1, d), lambda b, i: (b, 0, 0))
    const = pl.BlockSpec((1, d), lambda b, i: (0, 0))
    out_specs = [act, act] if with_h else [act]
    out_shape = [jax.ShapeDtypeStruct((bsz, seq, d), F32)]
    if with_h:
        out_shape.append(jax.ShapeDtypeStruct((bsz, seq, d), BF16))
    return pl.pallas_call(
        functools.partial(_ffn_down_kernel, alpha=alpha),
        grid=(bsz, seq // tm),
        in_specs=[pl.BlockSpec((None, tm, dff), lambda b, i: (b, i, 0)),
                  pl.BlockSpec((None, dff, d), lambda b, i: (layer, 0, 0),
                               pipeline_mode=pl.Buffered(1)),
                  act, bvec, const, const, bvec, bvec],
        out_specs=out_specs,
        out_shape=out_shape,
        compiler_params=_cparams("parallel", "parallel"),
    )(g, w_down_bf, x, g2, ln_g.reshape(1, d), ln_b.reshape(1, d), sc_next, sh_next)


def kernel(x, c, rel_bias, w_ada, b_ada, w_in, b_f, g_mix_a, g_mix_b, g_mix_c, w_out,
           ln1_g, ln1_b, w_up, conv_w, conv_b, w_down, ln2_g, ln2_b):
    bsz, seq, d = x.shape
    depth = w_ada.shape[0]
    alpha = (2 * depth) ** 0.25
    n_qkv = 3 * (W_A + W_B + W_C)

    mod = _ada_mod(c, w_ada, b_ada)

    def mod_vec(l, idx):
        return mod[l, :, idx * d:(idx + 1) * d].reshape(bsz, 1, d)

    bm_a = _dilated_bias_tiles(rel_bias[:, :H_A])
    bt_b = _moba_bias_tiles(rel_bias[:, H_A:], seq)

    w_in_bf, w_down_bf = w_in.astype(BF16), w_down.astype(BF16)

    h = _modulate(x, mod_vec(0, 1), mod_vec(0, 0))
    for l in range(depth):
        w_f = jnp.pad(w_in_bf[l, :, n_qkv:], ((0, 0), (0, LANES - H_C)))
        qkv, f = _inproj(h.reshape(bsz * seq, d), w_in_bf, l, n_qkv, w_f)
        qkv = qkv.reshape(bsz, seq, n_qkv)
        caug = _forget_keys(f.reshape(bsz, seq, LANES), b_f[l])

        o_a = _dilated_attention(qkv, bm_a, bsz, seq)
        o_b = _moba_attention(qkv, bt_b, bsz, seq, 3 * W_A // LANES)
        o_c = _fox_attention(qkv, caug, bsz, seq, 3 * (W_A + W_B) // LANES)

        x, h = _mix_out(o_a, o_b, o_c, g_mix_a[l], g_mix_b[l], g_mix_c[l], w_out[l].astype(BF16),
                        x, mod_vec(l, 2), ln1_g[l], ln1_b[l], mod_vec(l, 4), mod_vec(l, 3), alpha)
        g = _ffn_up(h, w_up, l, conv_w[l], conv_b[l])
        last = l == depth - 1
        nxt = min(l + 1, depth - 1)
        outs = _ffn_down(g, w_down_bf, l, x, mod_vec(l, 5), ln2_g[l], ln2_b[l],
                         mod_vec(nxt, 1), mod_vec(nxt, 0), alpha, with_h=not last)
        x = outs[0]
        if not last:
            h = outs[1]
    return x
```

```python
import functools
import math

import numpy as np
import jax
import jax.numpy as jnp
from jax import lax
from jax.experimental import pallas as pl
from jax.experimental.pallas import tpu as pltpu

F32 = jnp.float32
BF16 = jnp.bfloat16

HEAD_DIM = 64
LANES = 128
H_A, H_B, H_C = 12, 10, 10
W_A, W_B, W_C = H_A * HEAD_DIM, H_B * HEAD_DIM, H_C * HEAD_DIM
DIL_PATTERNS = ((128, 1), (512, 4), (2048, 16))
DIL_W = 128
MOBA_BLOCK = 256
MOBA_TOPK = 3
N_BUCKETS = 32
REL_MAX_DIST = 2048
CONV_WIDTH = 3
LN_EPS = 1e-5
NEG = -1e30
ATTN_SCALE = HEAD_DIM ** -0.5
LOG2E = math.log2(math.e)
VMEM_LIMIT = 56 * 1024 * 1024


def _cparams(*sem):
    return pltpu.CompilerParams(dimension_semantics=sem, vmem_limit_bytes=VMEM_LIMIT)


def _t5_bucket(dist):
    n = jnp.maximum(dist, 0)
    max_exact = N_BUCKETS // 2
    nf = jnp.maximum(n, 1).astype(F32)
    large = max_exact + (jnp.log(nf / max_exact) / math.log(REL_MAX_DIST / max_exact)
                         * (N_BUCKETS - max_exact)).astype(jnp.int32)
    large = jnp.minimum(large, N_BUCKETS - 1)
    return jnp.where(n < max_exact, n, large)


def _ada_kernel(c_ref, w_ref, b_ref, o_ref):
    c = c_ref[...]
    cond = c * (1.0 / (1.0 + jnp.exp(-c)))
    o_ref[...] = jnp.dot(cond, w_ref[...], precision=lax.Precision.HIGHEST,
                         preferred_element_type=F32) + b_ref[...]


def _ada_mod(c, w_ada, b_ada):
    depth, d, n = w_ada.shape
    bsz = c.shape[0]
    tn = 1024
    return pl.pallas_call(
        _ada_kernel,
        grid=(depth, n // tn),
        in_specs=[pl.BlockSpec((bsz, d), lambda l, j: (0, 0)),
                  pl.BlockSpec((None, d, tn), lambda l, j: (l, 0, j)),
                  pl.BlockSpec((None, 1, tn), lambda l, j: (l, 0, j))],
        out_specs=pl.BlockSpec((None, bsz, tn), lambda l, j: (l, 0, j)),
        out_shape=jax.ShapeDtypeStruct((depth, bsz, n), F32),
        compiler_params=_cparams("parallel", "parallel"),
    )(c, w_ada, b_ada.reshape(depth, 1, n))


def _inproj_kernel(a_ref, w_ref, wf_ref, o_ref, f_ref):
    a = a_ref[...]
    o_ref[...] = jnp.dot(a, w_ref[...], preferred_element_type=F32)

    @pl.when(pl.program_id(1) == 0)
    def _():
        f_ref[...] = jnp.dot(a, wf_ref[...], preferred_element_type=F32)


def _inproj_modulated_kernel(x_ref, sc_ref, sh_ref, w_ref, wf_ref, o_ref, f_ref, h_s):
    @pl.when(pl.program_id(1) == 0)
    def _():
        h_s[...] = (x_ref[...] * (1.0 + sc_ref[...]) + sh_ref[...]).astype(BF16)
        f_ref[...] = jnp.dot(h_s[...], wf_ref[...], preferred_element_type=F32)

    o_ref[...] = jnp.dot(h_s[...], w_ref[...], preferred_element_type=F32)


def _inproj_modulated(x, sc, sh, w_in_bf, layer, n, w_f):
    bsz, seq, d = x.shape
    tm, tn = 1024, 1024
    per_b = seq // tm
    vec = pl.BlockSpec((None, 1, d), lambda i, j: (i // per_b, 0, 0))
    return pl.pallas_call(
        _inproj_modulated_kernel,
        grid=(bsz * per_b, n // tn),
        in_specs=[pl.BlockSpec((None, tm, d), lambda i, j: (i // per_b, i % per_b, 0)), vec, vec,
                  pl.BlockSpec((None, d, tn), lambda i, j: (layer, 0, j)),
                  pl.BlockSpec((d, LANES), lambda i, j: (0, 0))],
        out_specs=[pl.BlockSpec((tm, tn), lambda i, j: (i, j)),
                   pl.BlockSpec((tm, LANES), lambda i, j: (i, 0))],
        out_shape=[jax.ShapeDtypeStruct((bsz * seq, n), F32),
                   jax.ShapeDtypeStruct((bsz * seq, LANES), F32)],
        scratch_shapes=[pltpu.VMEM((tm, d), BF16)],
        compiler_params=_cparams("parallel", "arbitrary"),
    )(x, sc, sh, w_in_bf, w_f)


def _inproj(h, w_in_bf, layer, n, w_f):
    t, d = h.shape
    tm, tn = 1024, 1024
    return pl.pallas_call(
        _inproj_kernel,
        grid=(t // tm, n // tn),
        in_specs=[pl.BlockSpec((tm, d), lambda i, j: (i, 0)),
                  pl.BlockSpec((None, d, tn), lambda i, j: (layer, 0, j)),
                  pl.BlockSpec((d, LANES), lambda i, j: (0, 0))],
        out_specs=[pl.BlockSpec((tm, tn), lambda i, j: (i, j)),
                   pl.BlockSpec((tm, LANES), lambda i, j: (i, 0))],
        out_shape=[jax.ShapeDtypeStruct((t, n), F32),
                   jax.ShapeDtypeStruct((t, LANES), F32)],
        compiler_params=_cparams("parallel", "arbitrary"),
    )(h, w_in_bf, w_f)


TOEP_LEN = 512
TOEP_ROWS = 256


TOEP_BATCH = 8


def _toeplitz_kernel(rr_ref, o_ref):
    for t in range(TOEP_BATCH):
        x = jnp.broadcast_to(rr_ref[t], (TOEP_ROWS, TOEP_LEN))
        y = pltpu.roll(x, TOEP_LEN - TOEP_ROWS + 1, 1, stride=1, stride_axis=0)
        o_ref[t] = y[:, :o_ref.shape[2]]


def _toeplitz(rr, cols):
    g = rr.shape[0]
    assert g % TOEP_BATCH == 0
    return pl.pallas_call(
        _toeplitz_kernel,
        grid=(g // TOEP_BATCH,),
        in_specs=[pl.BlockSpec((TOEP_BATCH, 1, TOEP_LEN), lambda i: (i, 0, 0))],
        out_specs=pl.BlockSpec((TOEP_BATCH, TOEP_ROWS, cols), lambda i: (i, 0, 0)),
        out_shape=jax.ShapeDtypeStruct((g, TOEP_ROWS, cols), F32),
        compiler_params=_cparams("parallel"),
    )(rr.reshape(g, 1, TOEP_LEN))


def _bias_of_distance(rel, dist, valid):
    bias = jnp.moveaxis(rel[_t5_bucket(dist)], -1, 0)
    return jnp.where(valid, bias * LOG2E, NEG).astype(F32)


def _dilated_bias_tiles(rel_a):
    c = jnp.arange(TOEP_LEN)
    rows = []
    for (_, dil) in DIL_PATTERNS:
        d0 = c - (TOEP_ROWS - 1) + DIL_W
        d1 = c - (TOEP_ROWS - 1)
        rows.append(jnp.stack([_bias_of_distance(rel_a, d0 * dil, (d0 >= 0) & (d0 <= DIL_W)),
                               _bias_of_distance(rel_a, d1 * dil, d1 >= 0)]))
    rr = jnp.stack(rows)
    out = _toeplitz(rr.reshape(-1, TOEP_LEN), DIL_W)
    return out.reshape(len(DIL_PATTERNS), 2, H_A, TOEP_ROWS, DIL_W)


def _moba_num_tiles(seq):
    blk = MOBA_BLOCK
    d_all = np.arange(max(seq, 2))
    nf = np.maximum(d_all, 1).astype(np.float32)
    half = N_BUCKETS // 2
    big = half + (np.log(nf / half) / math.log(REL_MAX_DIST / half) * (N_BUCKETS - half)).astype(np.int32)
    bucket = np.where(d_all < half, d_all, np.minimum(big, N_BUCKETS - 1))
    last_change = int(np.max(np.nonzero(np.diff(bucket))[0])) + 1
    first_const = min(seq // blk - 1, -(-(last_change + 2 + blk - 1) // blk))
    return first_const + 1


def _moba_bias_tiles(rel_b, seq):
    ntile = _moba_num_tiles(seq)
    c = jnp.arange(TOEP_LEN)[None, :]
    d = jnp.arange(ntile)[:, None] * MOBA_BLOCK + c - (TOEP_ROWS - 1)
    rr = _bias_of_distance(rel_b, d, d >= 0)
    out = _toeplitz(rr.reshape(-1, TOEP_LEN), MOBA_BLOCK)
    return out.reshape(H_B, ntile, TOEP_ROWS, MOBA_BLOCK)


def _head_rows():
    row = lax.broadcasted_iota(jnp.int32, (LANES, 1), 0)
    return row < HEAD_DIM, row >= HEAD_DIM


DIL_UNROLL = 8


def _dilated_kernel(q_ref, k_ref, v_ref, bm_ref, o_ref, og_s, lse_s, sblk_s, pblk_s, vts_s, *, seq):
    w = DIL_W
    own_rows = _head_rows()
    for g, (_, dil) in enumerate(DIL_PATTERNS):
        nb = (seq // dil) // w

        def rows(start, size, dil=dil):
            return pl.ds(start, size) if dil == 1 else pl.ds(start, size, stride=dil)

        def trip(it, carry, g=g, dil=dil, nb=nb, rows=rows):
            metas = []
            k = None
            for u in range(DIL_UNROLL):
                idx = it * DIL_UNROLL + u
                r = idx // nb
                n = idx % nb
                q_rows = rows(n * w * dil + r, w)
                var = jnp.where(n == 0, 1, 0)
                qt = q_ref[q_rows, :].T * (ATTN_SCALE * LOG2E)
                kv_slot = u - 1 if (nb == 2 and u % 2 == 1) else u
                if kv_slot == u:
                    kv_rows = rows(jnp.maximum(n - 1, 0) * w * dil + r, 2 * w)
                    k = k_ref[kv_rows, :].astype(BF16)
                    vt = v_ref[kv_rows, :].T
                    for h in range(2):
                        vts_s[u, h] = jnp.concatenate(
                            [vt[h * HEAD_DIM:(h + 1) * HEAD_DIM, :], _ones_rows(2 * w)],
                            axis=0).astype(BF16)
                tops = []
                for h in range(2):
                    s = jnp.dot(k, jnp.where(own_rows[h], qt, 0.0).astype(BF16),
                                preferred_element_type=F32) + bm_ref[g, var, h]
                    sblk_s[u, h] = s
                    tops.append(jnp.max(s, axis=0, keepdims=True))
                metas.append((q_rows, tops, kv_slot))
            pvs = []
            for u, (_, tops, kv_slot) in enumerate(metas):
                for h in range(2):
                    pblk_s[u, h] = jnp.exp2(sblk_s[u, h] - tops[h]).astype(BF16)
                    pvs.append(jnp.dot(vts_s[kv_slot, h], pblk_s[u, h], preferred_element_type=F32))
            for u, (q_rows, tops, _) in enumerate(metas):
                pair = pvs[2 * u:2 * u + 2]
                o_t = jnp.concatenate([pv[:HEAD_DIM] / pv[HEAD_DIM:HEAD_DIM + 1] for pv in pair], axis=0)
                lse_t = jnp.concatenate(
                    [jnp.broadcast_to(m + jnp.log(pv[HEAD_DIM:HEAD_DIM + 1]) * LOG2E, (HEAD_DIM, w))
                     for m, pv in zip(tops, pair)], axis=0)
                og_s[g, q_rows, :] = o_t.T
                lse_s[g, q_rows, :] = lse_t.T
            return carry

        lax.fori_loop(0, nb * dil // DIL_UNROLL, trip, 0)

    top = jnp.maximum(jnp.maximum(lse_s[0], lse_s[1]), lse_s[2])
    num = jnp.zeros_like(top)
    den = jnp.zeros_like(top)
    for g in range(len(DIL_PATTERNS)):
        wg = jnp.exp2(lse_s[g] - top)
        num = num + wg * og_s[g]
        den = den + wg
    o_ref[...] = num / den


def _dilated_attention(qkv, bm_a, bsz, seq):
    npair = H_A // 2
    npat = len(DIL_PATTERNS)
    kern = functools.partial(_dilated_kernel, seq=seq)

    def col(off):
        return pl.BlockSpec((None, seq, LANES), lambda p, b, off=off: (b, 0, off + p))

    return pl.pallas_call(
        kern,
        grid=(npair, bsz),
        in_specs=[col(0), col(npair), col(2 * npair),
                  pl.BlockSpec((npat, 2, 2, 2 * DIL_W, DIL_W), lambda p, b: (0, 0, p, 0, 0))],
        out_specs=pl.BlockSpec((None, seq, LANES), lambda p, b: (b, 0, p)),
        out_shape=jax.ShapeDtypeStruct((bsz, seq, W_A), F32),
        scratch_shapes=[pltpu.VMEM((npat, seq, LANES), F32)] * 2 + [
            pltpu.VMEM((DIL_UNROLL, 2, 2 * DIL_W, DIL_W), F32),
            pltpu.VMEM((DIL_UNROLL, 2, 2 * DIL_W, DIL_W), BF16),
            pltpu.VMEM((DIL_UNROLL, 2, VT_ROWS, 2 * DIL_W), BF16)],
        compiler_params=_cparams("parallel", "parallel"),
    )(qkv, qkv, qkv, bm_a)


VT_ROWS = HEAD_DIM + 16


def _ones_rows(width):
    row = lax.broadcasted_iota(jnp.int32, (VT_ROWS - HEAD_DIM, width), 0)
    return jnp.where(row == 0, 1.0, 0.0)


def _transpose_to(dst_ref, src_ref, chunk=512):
    for c in range(src_ref.shape[0] // chunk):
        cols = slice(c * chunk, (c + 1) * chunk)
        t = src_ref[cols, :].T
        for h in range(2):
            dst_ref[h, :, cols] = jnp.concatenate(
                [t[h * HEAD_DIM:(h + 1) * HEAD_DIM, :], _ones_rows(chunk)], axis=0).astype(BF16)


FLASH_CHUNK = 2


def _own_tile(qi, tk, tq):
    col = lax.broadcasted_iota(jnp.int32, (1, tq), 1)
    return qi * (tq // tk) + col // tk


def _causal_rows(qi, tk, nblk, o_ref, kaug_s, vt_s, qa_s, ssc, psc, acc_s, bias_fn):
    ck = FLASH_CHUNK * tk
    tq = ck

    def stage(c, u):
        r0 = (c % 2) * ck + u * tk
        return slice(r0, r0 + tk)

    def scores(c, h, maybe_diag):
        top = None
        for u in range(FLASH_CHUNK):
            kj = c * FLASH_CHUNK + u
            s = jnp.dot(kaug_s[h, kj * tk:(kj + 1) * tk, :], qa_s[h], preferred_element_type=F32)
            s = bias_fn(h, kj, s, maybe_diag)
            ssc[h, stage(c, u), :] = s
            cmax = jnp.max(s, axis=0, keepdims=True)
            top = cmax if top is None else jnp.maximum(top, cmax)
        return top

    def values(c, h, top, m):
        for u in range(FLASH_CHUNK):
            psc[h, stage(c, u), :] = jnp.exp2(ssc[h, stage(c, u), :] - top).astype(BF16)
        slot = slice((c % 2) * ck, (c % 2 + 1) * ck)
        pv = jnp.dot(vt_s[h, :, c * ck:(c + 1) * ck], psc[h, slot, :], preferred_element_type=F32)
        acc_s[h] = pv if m is None else jnp.exp2(m - top) * acc_s[h] + pv

    def row_pass(nchunk):
        tops = {(0, h): scores(0, h, nchunk == 1) for h in range(2)}
        for c in range(nchunk):
            if c + 1 < nchunk:
                for h in range(2):
                    tops[(c + 1, h)] = jnp.maximum(tops[(c, h)], scores(c + 1, h, c + 2 == nchunk))
            for h in range(2):
                values(c, h, tops[(c, h)], tops[(c - 1, h)] if c else None)
        return jnp.concatenate([acc_s[h, :HEAD_DIM] / acc_s[h, HEAD_DIM:HEAD_DIM + 1]
                                for h in range(2)], axis=0).T

    for nchunk in range(1, nblk // FLASH_CHUNK + 1):
        @pl.when(qi == nchunk - 1)
        def _(nchunk=nchunk):
            o_ref[...] = row_pass(nchunk)


def _moba_kernel(q_ref, k_ref, v_ref, bt_ref, o_ref,
                 kaug_s, vt_s, kmean_s, qa_s, ssc, psc, acc_s, *, nblk, ntile):
    blk = MOBA_BLOCK
    qi = pl.program_id(2)
    spare = LANES - HEAD_DIM - nblk

    @pl.when(qi == 0)
    def _():
        lane = lax.broadcasted_iota(jnp.int32, (1, LANES), 1)
        for n in range(nblk):
            k = k_ref[n * blk:(n + 1) * blk, :]
            kaug_s[0, n * blk:(n + 1) * blk, :] = jnp.where(
                lane < HEAD_DIM, k, (lane == HEAD_DIM + n).astype(F32)).astype(BF16)
            kaug_s[1, n * blk:(n + 1) * blk, :] = jnp.where(
                lane >= HEAD_DIM, k, (lane == n).astype(F32)).astype(BF16)
            kmean_s[n:n + 1, :] = jnp.mean(k, axis=0, keepdims=True)
        _transpose_to(vt_s, v_ref)

    tq = FLASH_CHUNK * blk
    qt = q_ref[...].T
    own_rows = _head_rows()
    blk_id = lax.broadcasted_iota(jnp.int32, (nblk, 1), 0)
    blk_idf = blk_id.astype(F32)
    own_blk = _own_tile(qi, blk, tq)
    past = blk_id < own_blk
    for h in range(2):
        qh = jnp.where(own_rows[h], qt, 0.0)
        gate = jnp.dot(kmean_s[...], qh, precision=lax.Precision.HIGHEST,
                       preferred_element_type=F32)
        gate = jnp.where(past, gate, NEG)
        chosen = jnp.zeros(gate.shape, jnp.bool_)
        for _ in range(MOBA_TOPK):
            mx = jnp.max(gate, axis=0, keepdims=True)
            first_idx = jnp.min(jnp.where(gate == mx, blk_idf, float(nblk)), axis=0, keepdims=True)
            pick = blk_idf == first_idx
            chosen = jnp.logical_or(chosen, pick)
            gate = jnp.where(pick, -jnp.inf, gate)
        keep = jnp.logical_or(jnp.logical_and(chosen, past), blk_id == own_blk)
        selb = jnp.where(keep, 0.0, NEG)
        qs = qt[h * HEAD_DIM:(h + 1) * HEAD_DIM, :] * (ATTN_SCALE * LOG2E)
        pad = jnp.zeros((spare, tq), F32)
        parts = [qs, selb, pad] if h == 0 else [selb, pad, qs]
        qa_s[h] = jnp.concatenate(parts, axis=0).astype(BF16)

    def bias_fn(h, kj, s, _):
        tiles = [bt_ref[h, jnp.clip(qi * FLASH_CHUNK + a - kj, 0, ntile - 1)]
                 for a in range(FLASH_CHUNK)]
        return s + jnp.concatenate(tiles, axis=1)

    _causal_rows(qi, blk, nblk, o_ref, kaug_s, vt_s, qa_s, ssc, psc, acc_s, bias_fn)


def _moba_attention(qkv, bt_b, bsz, seq, col0):
    npair = H_B // 2
    nblk = seq // MOBA_BLOCK
    ntile = bt_b.shape[1]
    assert nblk % 8 == 0 and nblk <= LANES - HEAD_DIM
    assert nblk % FLASH_CHUNK == 0
    kern = functools.partial(_moba_kernel, nblk=nblk, ntile=ntile)
    blk = MOBA_BLOCK
    tq = FLASH_CHUNK * blk

    def col(off):
        return pl.BlockSpec((None, seq, LANES), lambda p, b, i, off=off: (b, 0, off + p))

    return pl.pallas_call(
        kern,
        grid=(npair, bsz, seq // tq),
        in_specs=[pl.BlockSpec((None, tq, LANES), lambda p, b, i: (b, i, col0 + p)),
                  col(col0 + npair), col(col0 + 2 * npair),
                  pl.BlockSpec((2, ntile, blk, blk), lambda p, b, i: (p, 0, 0, 0))],
        out_specs=pl.BlockSpec((None, tq, LANES), lambda p, b, i: (b, i, p)),
        out_shape=jax.ShapeDtypeStruct((bsz, seq, W_B), F32),
        scratch_shapes=[pltpu.VMEM((2, seq, LANES), BF16), pltpu.VMEM((2, VT_ROWS, seq), BF16),
                        pltpu.VMEM((nblk, LANES), F32), pltpu.VMEM((2, LANES, tq), BF16),
                        pltpu.VMEM((2, 2 * tq, tq), F32), pltpu.VMEM((2, 2 * tq, tq), BF16),
                        pltpu.VMEM((2, VT_ROWS, tq), F32)],
        compiler_params=_cparams("parallel", "parallel", "arbitrary"),
    )(qkv, qkv, qkv, bt_b)


CUM_SPLIT = 3
CUM_CHUNK = 256
FOX_TILE = 256


def _fgate_kernel(f_ref, bf_ref, pm_ref, o_ref, cum_s):
    seq = f_ref.shape[0]
    ch = CUM_CHUNK
    i = lax.broadcasted_iota(jnp.int32, (ch, ch), 0)
    j = lax.broadcasted_iota(jnp.int32, (ch, ch), 1)
    lower = (j <= i).astype(F32)
    carry = jnp.zeros((1, LANES), F32)
    for c in range(seq // ch):
        z = f_ref[c * ch:(c + 1) * ch, :] + bf_ref[...]
        ls = jnp.minimum(z, 0.0) - jnp.log(1.0 + jnp.exp(-jnp.abs(z)))
        cum = jnp.dot(lower, ls, precision=lax.Precision.HIGHEST, preferred_element_type=F32) + carry
        cum_s[c * ch:(c + 1) * ch, :] = cum
        carry = cum[ch - 1:ch, :]
    rest = cum_s[...] * (-LOG2E)
    out = None
    for t in range(CUM_SPLIT):
        term = rest.astype(BF16)
        rest = rest - term.astype(F32)
        part = jnp.dot(term, pm_ref[t], preferred_element_type=F32)
        out = part if out is None else out + part
    o_ref[...] = out.astype(BF16)


def _forget_keys(f, b_f):
    bsz, seq, _ = f.shape
    npair = H_C // 2
    nblk = seq // FOX_TILE
    pm = np.zeros((CUM_SPLIT, LANES, npair * LANES), np.float32)
    for t in range(CUM_SPLIT):
        for p in range(npair):
            pm[t, 2 * p, p * LANES + HEAD_DIM + nblk + t] = 1.0
            pm[t, 2 * p + 1, p * LANES + nblk + t] = 1.0
    bf = jnp.pad(b_f, (0, LANES - b_f.shape[0])).reshape(1, LANES)
    return pl.pallas_call(
        _fgate_kernel,
        grid=(bsz,),
        in_specs=[pl.BlockSpec((None, seq, LANES), lambda b: (b, 0, 0)),
                  pl.BlockSpec((1, LANES), lambda b: (0, 0)),
                  pl.BlockSpec((CUM_SPLIT, LANES, npair * LANES), lambda b: (0, 0, 0))],
        out_specs=pl.BlockSpec((None, seq, npair * LANES), lambda b: (b, 0, 0)),
        out_shape=jax.ShapeDtypeStruct((bsz, seq, npair * LANES), BF16),
        scratch_shapes=[pltpu.VMEM((seq, LANES), F32)],
        compiler_params=_cparams("parallel"),
    )(f, bf, jnp.asarray(pm, BF16))


def _fox_kernel(q_ref, k_ref, v_ref, ca_ref, o_ref,
                kaug_s, vt_s, qa_s, cm_s, ssc, psc, acc_s, *, tk, nblk):
    qi = pl.program_id(2)
    spare = LANES - HEAD_DIM - nblk - 8

    @pl.when(qi == 0)
    def _():
        lane = lax.broadcasted_iota(jnp.int32, (1, LANES), 1)
        for n in range(nblk):
            k = k_ref[n * tk:(n + 1) * tk, :]
            ca = ca_ref[n * tk:(n + 1) * tk, :].astype(F32)
            kaug_s[0, n * tk:(n + 1) * tk, :] = jnp.where(
                lane < HEAD_DIM, k, ca + (lane == HEAD_DIM + n).astype(F32)).astype(BF16)
            kaug_s[1, n * tk:(n + 1) * tk, :] = jnp.where(
                lane >= HEAD_DIM, k, ca + (lane == n).astype(F32)).astype(BF16)
        _transpose_to(vt_s, v_ref)
        key = lax.broadcasted_iota(jnp.int32, (tk, tk), 0)
        qry = lax.broadcasted_iota(jnp.int32, (tk, tk), 1)
        cm_s[0] = jnp.zeros((tk, tk), F32)
        cm_s[1] = jnp.where(key <= qry, 0.0, NEG)

    tq = FLASH_CHUNK * tk
    qt = q_ref[...].T * (ATTN_SCALE * LOG2E)
    blk_id = lax.broadcasted_iota(jnp.int32, (nblk, 1), 0)
    future = jnp.where(blk_id <= _own_tile(qi, tk, tq), 0.0, NEG)
    sub = lax.broadcasted_iota(jnp.int32, (8, 1), 0)
    ones = jnp.broadcast_to(jnp.where(sub < CUM_SPLIT, 1.0, 0.0), (8, tq))
    pad = jnp.zeros((spare, tq), F32)
    for h in range(2):
        qs = qt[h * HEAD_DIM:(h + 1) * HEAD_DIM, :]
        parts = [qs, future, ones, pad] if h == 0 else [future, ones, pad, qs]
        qa_s[h] = jnp.concatenate(parts, axis=0).astype(BF16)

    def bias_fn(h, kj, s, maybe_diag):
        if not maybe_diag:
            return s
        tiles = [cm_s[jnp.where(kj == qi * FLASH_CHUNK + a, 1, 0)] for a in range(FLASH_CHUNK)]
        return s + jnp.concatenate(tiles, axis=1)

    _causal_rows(qi, tk, nblk, o_ref, kaug_s, vt_s, qa_s, ssc, psc, acc_s, bias_fn)


def _fox_attention(qkv, caug, bsz, seq, col0):
    npair = H_C // 2
    tk = FOX_TILE
    tq = FLASH_CHUNK * tk
    nblk = seq // tk
    assert nblk % 8 == 0 and nblk % FLASH_CHUNK == 0 and nblk + 8 <= LANES - HEAD_DIM
    kern = functools.partial(_fox_kernel, tk=tk, nblk=nblk)

    def col(off):
        return pl.BlockSpec((None, seq, LANES), lambda p, b, i, off=off: (b, 0, off + p))

    return pl.pallas_call(
        kern,
        grid=(npair, bsz, seq // tq),
        in_specs=[pl.BlockSpec((None, tq, LANES), lambda p, b, i: (b, i, col0 + p)),
                  col(col0 + npair), col(col0 + 2 * npair),
                  pl.BlockSpec((None, seq, LANES), lambda p, b, i: (b, 0, p))],
        out_specs=pl.BlockSpec((None, tq, LANES), lambda p, b, i: (b, i, p)),
        out_shape=jax.ShapeDtypeStruct((bsz, seq, W_C), F32),
        scratch_shapes=[pltpu.VMEM((2, seq, LANES), BF16), pltpu.VMEM((2, VT_ROWS, seq), BF16),
                        pltpu.VMEM((2, LANES, tq), BF16), pltpu.VMEM((2, tk, tk), F32),
                        pltpu.VMEM((2, 2 * tq, tq), F32), pltpu.VMEM((2, 2 * tq, tq), BF16),
                        pltpu.VMEM((2, VT_ROWS, tq), F32)],
        compiler_params=_cparams("parallel", "parallel", "arbitrary"),
    )(qkv, qkv, qkv, caug)


def _layer_norm_rows(z, g, b):
    mu = jnp.mean(z, axis=-1, keepdims=True)
    zc = z - mu
    var = jnp.mean(zc * zc, axis=-1, keepdims=True)
    return zc * lax.rsqrt(var + LN_EPS) * g + b


def _mixout_kernel(oa_ref, ob_ref, oc_ref, ga_ref, gb_ref, gc_ref, wa_ref, wb_ref, wc_ref,
                   x_ref, g1_ref, lng_ref, lnb_ref, sc_ref, sh_ref, x_out, h_out, *, alpha):
    def normed(o_ref, g_ref, rows):
        o = o_ref[rows, :]
        ms = jnp.mean(o * o, axis=-1, keepdims=True)
        return (o * lax.rsqrt(ms + LN_EPS) * g_ref[...]).astype(BF16)

    blocks = [slice(r0, r0 + MIX_ROWS) for r0 in range(0, x_ref.shape[0], MIX_ROWS)]
    ys = []
    for rows in blocks:
        y = jnp.dot(normed(oa_ref, ga_ref, rows), wa_ref[...], preferred_element_type=F32)
        y = y + jnp.dot(normed(ob_ref, gb_ref, rows), wb_ref[...], preferred_element_type=F32)
        ys.append(y + jnp.dot(normed(oc_ref, gc_ref, rows), wc_ref[...], preferred_element_type=F32))
    for rows, y in zip(blocks, ys):
        x1 = _layer_norm_rows(alpha * x_ref[rows, :] + (1.0 + g1_ref[...]) * y,
                              lng_ref[...], lnb_ref[...])
        x_out[rows, :] = x1
        h_out[rows, :] = (x1 * (1.0 + sc_ref[...]) + sh_ref[...]).astype(BF16)


MIX_ROWS = 256


def _mix_out(o_a, o_b, o_c, g_a, g_b, g_c, w_out_bf, x, g1, ln_g, ln_b, sc2, sh2, alpha):
    bsz, seq, d = x.shape
    tm = 512
    w_a, w_b, w_c = w_out_bf[:W_A], w_out_bf[W_A:W_A + W_B], w_out_bf[W_A + W_B:]

    def act(width):
        return pl.BlockSpec((None, tm, width), lambda b, i: (b, i, 0))

    def const(shape):
        return pl.BlockSpec(shape, lambda b, i: (0,) * len(shape), pipeline_mode=pl.Buffered(1))

    bvec = pl.BlockSpec((None, 1, d), lambda b, i: (b, 0, 0))
    return pl.pallas_call(
        functools.partial(_mixout_kernel, alpha=alpha),
        grid=(bsz, seq // tm),
        in_specs=[act(W_A), act(W_B), act(W_C),
                  const((1, W_A)), const((1, W_B)), const((1, W_C)),
                  const((W_A, d)), const((W_B, d)), const((W_C, d)),
                  act(d), bvec, const((1, d)), const((1, d)), bvec, bvec],
        out_specs=[act(d), act(d)],
        out_shape=[jax.ShapeDtypeStruct((bsz, seq, d), F32),
                   jax.ShapeDtypeStruct((bsz, seq, d), BF16)],
        compiler_params=_cparams("parallel", "parallel"),
    )(o_a, o_b, o_c, g_a.reshape(1, -1), g_b.reshape(1, -1), g_c.reshape(1, -1),
      w_a, w_b, w_c, x, g1, ln_g.reshape(1, d), ln_b.reshape(1, d), sc2, sh2)


FFN_HALO = 8
FFN_SUB = 256
FFN_ROWS = 256


def _ffn_up_kernel(h_ref, wa_ref, wb_ref, cwa_ref, cwb_ref, cba_ref, cbb_ref, o_ref,
                   bufa, bufb, wa_bf, wb_bf):
    i = pl.program_id(2)
    tm, tn = o_ref.shape

    @pl.when(jnp.logical_and(pl.program_id(1) == 0, i == 0))
    def _():
        wa_bf[...] = wa_ref[...].astype(BF16)
        wb_bf[...] = wb_ref[...].astype(BF16)

    @pl.when(i == 0)
    def _():
        bufa[0:FFN_HALO, :] = jnp.zeros((FFN_HALO, tn), F32)
        bufb[0:FFN_HALO, :] = jnp.zeros((FFN_HALO, tn), F32)

    subs = [(r * FFN_ROWS, slice(c * FFN_SUB, (c + 1) * FFN_SUB))
            for r in range(tm // FFN_ROWS) for c in range(tn // FFN_SUB)]
    prods = []
    for r0, cs in subs:
        h = h_ref[r0:r0 + FFN_ROWS, :]
        prods.append((jnp.dot(h, wa_bf[:, cs], preferred_element_type=F32),
                      jnp.dot(h, wb_bf[:, cs], preferred_element_type=F32)))

    def conv(u, cw_ref, cb_ref, buf, r0, cs):
        base = FFN_HALO + r0
        buf[base:base + FFN_ROWS, cs] = u
        return (cw_ref[2:3, cs] * u
                + cw_ref[1:2, cs] * buf[base - 1:base - 1 + FFN_ROWS, cs]
                + cw_ref[0:1, cs] * buf[base - 2:base - 2 + FFN_ROWS, cs]
                + cb_ref[:, cs])

    for (r0, cs), (ua, ub) in zip(subs, prods):
        a = conv(ua, cwa_ref, cba_ref, bufa, r0, cs)
        b = conv(ub, cwb_ref, cbb_ref, bufb, r0, cs)
        o_ref[r0:r0 + FFN_ROWS, cs] = (a * (1.0 / (1.0 + jnp.exp(-a))) * b).astype(o_ref.dtype)
    for buf in (bufa, bufb):
        buf[0:FFN_HALO, :] = buf[tm:tm + FFN_HALO, :]


def _ffn_up(h, w_up, layer, conv_w, conv_b):
    bsz, seq, d = h.shape
    dff = w_up.shape[2] // 2
    tm, tn = 1024, 512
    nj = dff // tn
    cb = conv_b.reshape(1, -1)
    return pl.pallas_call(
        _ffn_up_kernel,
        grid=(nj, bsz, seq // tm),
        in_specs=[pl.BlockSpec((None, tm, d), lambda j, b, i: (b, i, 0)),
                  pl.BlockSpec((None, d, tn), lambda j, b, i: (layer, 0, j)),
                  pl.BlockSpec((None, d, tn), lambda j, b, i: (layer, 0, nj + j)),
                  pl.BlockSpec((CONV_WIDTH, tn), lambda j, b, i: (0, j)),
                  pl.BlockSpec((CONV_WIDTH, tn), lambda j, b, i: (0, nj + j)),
                  pl.BlockSpec((1, tn), lambda j, b, i: (0, j)),
                  pl.BlockSpec((1, tn), lambda j, b, i: (0, nj + j))],
        out_specs=pl.BlockSpec((None, tm, tn), lambda j, b, i: (b, i, j)),
        out_shape=jax.ShapeDtypeStruct((bsz, seq, dff), BF16),
        scratch_shapes=[pltpu.VMEM((tm + FFN_HALO, tn), F32)] * 2 + [pltpu.VMEM((d, tn), BF16)] * 2,
        compiler_params=_cparams("parallel", "arbitrary", "arbitrary"),
    )(h, w_up, w_up, conv_w, conv_w, cb, cb)


def _ffn_down_kernel(g_ref, w_ref, x_ref, g2_ref, lng_ref, lnb_ref, sc_ref, sh_ref,
                     x_out, *maybe_h_out, alpha):
    y = jnp.dot(g_ref[...], w_ref[...], preferred_element_type=F32)
    x2 = _layer_norm_rows(alpha * x_ref[...] + (1.0 + g2_ref[...]) * y, lng_ref[...], lnb_ref[...])
    x_out[...] = x2
    for h_out in maybe_h_out:
        h_out[...] = (x2 * (1.0 + sc_ref[...]) + sh_ref[...]).astype(BF16)


def _ffn_down(g, w_down_bf, layer, x, g2, ln_g, ln_b, sc_next, sh_next, alpha, with_h):
    bsz, seq, d = x.shape
    dff = g.shape[2]
    tm = 256
    act = pl.BlockSpec((None, tm, d), lambda b, i: (b, i, 0))
    bvec = pl.BlockSpec((None, 1, d), lambda b, i: (b, 0, 0))
    const = pl.BlockSpec((1, d), lambda b, i: (0, 0))
    out_specs = [act, act] if with_h else [act]
    out_shape = [jax.ShapeDtypeStruct((bsz, seq, d), F32)]
    if with_h:
        out_shape.append(jax.ShapeDtypeStruct((bsz, seq, d), BF16))
    return pl.pallas_call(
        functools.partial(_ffn_down_kernel, alpha=alpha),
        grid=(bsz, seq // tm),
        in_specs=[pl.BlockSpec((None, tm, dff), lambda b, i: (b, i, 0)),
                  pl.BlockSpec((None, dff, d), lambda b, i: (layer, 0, 0),
                               pipeline_mode=pl.Buffered(1)),
                  act, bvec, const, const, bvec, bvec],
        out_specs=out_specs,
        out_shape=out_shape,
        compiler_params=_cparams("parallel", "parallel"),
    )(g, w_down_bf, x, g2, ln_g.reshape(1, d), ln_b.reshape(1, d), sc_next, sh_next)


def kernel(x, c, rel_bias, w_ada, b_ada, w_in, b_f, g_mix_a, g_mix_b, g_mix_c, w_out,
           ln1_g, ln1_b, w_up, conv_w, conv_b, w_down, ln2_g, ln2_b):
    bsz, seq, d = x.shape
    depth = w_ada.shape[0]
    alpha = (2 * depth) ** 0.25
    n_qkv = 3 * (W_A + W_B + W_C)

    mod = _ada_mod(c, w_ada, b_ada)

    def mod_vec(l, idx):
        return mod[l, :, idx * d:(idx + 1) * d].reshape(bsz, 1, d)

    bm_a = _dilated_bias_tiles(rel_bias[:, :H_A])
    bt_b = _moba_bias_tiles(rel_bias[:, H_A:], seq)

    w_in_bf, w_down_bf = w_in.astype(BF16), w_down.astype(BF16)

    for l in range(depth):
        w_f = jnp.pad(w_in_bf[l, :, n_qkv:], ((0, 0), (0, LANES - H_C)))
        if l == 0:
            qkv, f = _inproj_modulated(x, mod_vec(0, 1), mod_vec(0, 0), w_in_bf, l, n_qkv, w_f)
        else:
            qkv, f = _inproj(h.reshape(bsz * seq, d), w_in_bf, l, n_qkv, w_f)
        qkv = qkv.reshape(bsz, seq, n_qkv)
        caug = _forget_keys(f.reshape(bsz, seq, LANES), b_f[l])

        o_a = _dilated_attention(qkv, bm_a, bsz, seq)
        o_b = _moba_attention(qkv, bt_b, bsz, seq, 3 * W_A // LANES)
        o_c = _fox_attention(qkv, caug, bsz, seq, 3 * (W_A + W_B) // LANES)

        x, h = _mix_out(o_a, o_b, o_c, g_mix_a[l], g_mix_b[l], g_mix_c[l], w_out[l].astype(BF16),
                        x, mod_vec(l, 2), ln1_g[l], ln1_b[l], mod_vec(l, 4), mod_vec(l, 3), alpha)
        g = _ffn_up(h, w_up, l, conv_w[l], conv_b[l])
        last = l == depth - 1
        nxt = min(l + 1, depth - 1)
        outs = _ffn_down(g, w_down_bf, l, x, mod_vec(l, 5), ln2_g[l], ln2_b[l],
                         mod_vec(nxt, 1), mod_vec(nxt, 0), alpha, with_h=not last)
        x = outs[0]
        if not last:
            h = outs[1]
    return x
```

```python
import functools
import math

import numpy as np
import jax
import jax.numpy as jnp
from jax import lax
from jax.experimental import pallas as pl
from jax.experimental.pallas import tpu as pltpu

F32 = jnp.float32
BF16 = jnp.bfloat16

HEAD_DIM = 64
LANES = 128
H_A, H_B, H_C = 12, 10, 10
W_A, W_B, W_C = H_A * HEAD_DIM, H_B * HEAD_DIM, H_C * HEAD_DIM
DIL_PATTERNS = ((128, 1), (512, 4), (2048, 16))
DIL_W = 128
MOBA_BLOCK = 256
MOBA_TOPK = 3
N_BUCKETS = 32
REL_MAX_DIST = 2048
CONV_WIDTH = 3
LN_EPS = 1e-5
NEG = -1e30
ATTN_SCALE = HEAD_DIM ** -0.5
LOG2E = math.log2(math.e)
VMEM_LIMIT = 56 * 1024 * 1024


def _cparams(*sem):
    return pltpu.CompilerParams(dimension_semantics=sem, vmem_limit_bytes=VMEM_LIMIT)


def _t5_bucket(dist):
    n = jnp.maximum(dist, 0)
    max_exact = N_BUCKETS // 2
    nf = jnp.maximum(n, 1).astype(F32)
    large = max_exact + (jnp.log(nf / max_exact) / math.log(REL_MAX_DIST / max_exact)
                         * (N_BUCKETS - max_exact)).astype(jnp.int32)
    large = jnp.minimum(large, N_BUCKETS - 1)
    return jnp.where(n < max_exact, n, large)


def _ada_kernel(c_ref, w_ref, b_ref, o_ref):
    c = c_ref[...]
    cond = c * (1.0 / (1.0 + jnp.exp(-c)))
    o_ref[...] = jnp.dot(cond, w_ref[...], precision=lax.Precision.HIGHEST,
                         preferred_element_type=F32) + b_ref[...]


def _ada_mod(c, w_ada, b_ada):
    depth, d, n = w_ada.shape
    bsz = c.shape[0]
    tn = 1024
    return pl.pallas_call(
        _ada_kernel,
        grid=(depth, n // tn),
        in_specs=[pl.BlockSpec((bsz, d), lambda l, j: (0, 0)),
                  pl.BlockSpec((None, d, tn), lambda l, j: (l, 0, j)),
                  pl.BlockSpec((None, 1, tn), lambda l, j: (l, 0, j))],
        out_specs=pl.BlockSpec((None, bsz, tn), lambda l, j: (l, 0, j)),
        out_shape=jax.ShapeDtypeStruct((depth, bsz, n), F32),
        compiler_params=_cparams("parallel", "parallel"),
    )(c, w_ada, b_ada.reshape(depth, 1, n))


def _inproj_kernel(a_ref, w_ref, wf_ref, o_ref, f_ref):
    a = a_ref[...]
    o_ref[...] = jnp.dot(a, w_ref[...], preferred_element_type=F32)

    @pl.when(pl.program_id(1) == 0)
    def _():
        f_ref[...] = jnp.dot(a, wf_ref[...], preferred_element_type=F32)


def _inproj_modulated_kernel(x_ref, sc_ref, sh_ref, w_ref, wf_ref, o_ref, f_ref, h_s):
    @pl.when(pl.program_id(1) == 0)
    def _():
        h_s[...] = (x_ref[...] * (1.0 + sc_ref[...]) + sh_ref[...]).astype(BF16)
        f_ref[...] = jnp.dot(h_s[...], wf_ref[...], preferred_element_type=F32)

    o_ref[...] = jnp.dot(h_s[...], w_ref[...], preferred_element_type=F32)


def _inproj_modulated(x, sc, sh, w_in_bf, layer, n, w_f):
    bsz, seq, d = x.shape
    tm, tn = 1024, 1024
    per_b = seq // tm
    vec = pl.BlockSpec((None, 1, d), lambda i, j: (i // per_b, 0, 0))
    return pl.pallas_call(
        _inproj_modulated_kernel,
        grid=(bsz * per_b, n // tn),
        in_specs=[pl.BlockSpec((None, tm, d), lambda i, j: (i // per_b, i % per_b, 0)), vec, vec,
                  pl.BlockSpec((None, d, tn), lambda i, j: (layer, 0, j)),
                  pl.BlockSpec((d, LANES), lambda i, j: (0, 0))],
        out_specs=[pl.BlockSpec((tm, tn), lambda i, j: (i, j)),
                   pl.BlockSpec((tm, LANES), lambda i, j: (i, 0))],
        out_shape=[jax.ShapeDtypeStruct((bsz * seq, n), F32),
                   jax.ShapeDtypeStruct((bsz * seq, LANES), F32)],
        scratch_shapes=[pltpu.VMEM((tm, d), BF16)],
        compiler_params=_cparams("parallel", "arbitrary"),
    )(x, sc, sh, w_in_bf, w_f)


def _inproj(h, w_in_bf, layer, n, w_f):
    t, d = h.shape
    tm, tn = 1024, 1024
    return pl.pallas_call(
        _inproj_kernel,
        grid=(t // tm, n // tn),
        in_specs=[pl.BlockSpec((tm, d), lambda i, j: (i, 0)),
                  pl.BlockSpec((None, d, tn), lambda i, j: (layer, 0, j)),
                  pl.BlockSpec((d, LANES), lambda i, j: (0, 0))],
        out_specs=[pl.BlockSpec((tm, tn), lambda i, j: (i, j)),
                   pl.BlockSpec((tm, LANES), lambda i, j: (i, 0))],
        out_shape=[jax.ShapeDtypeStruct((t, n), F32),
                   jax.ShapeDtypeStruct((t, LANES), F32)],
        compiler_params=_cparams("parallel", "arbitrary"),
    )(h, w_in_bf, w_f)


TOEP_LEN = 512
TOEP_ROWS = 256


TOEP_BATCH = 8


def _toeplitz_kernel(rr_ref, o_ref):
    for t in range(TOEP_BATCH):
        x = jnp.broadcast_to(rr_ref[t], (TOEP_ROWS, TOEP_LEN))
        y = pltpu.roll(x, TOEP_LEN - TOEP_ROWS + 1, 1, stride=1, stride_axis=0)
        o_ref[t] = y[:, :o_ref.shape[2]]


def _toeplitz(rr, cols):
    g = rr.shape[0]
    assert g % TOEP_BATCH == 0
    return pl.pallas_call(
        _toeplitz_kernel,
        grid=(g // TOEP_BATCH,),
        in_specs=[pl.BlockSpec((TOEP_BATCH, 1, TOEP_LEN), lambda i: (i, 0, 0))],
        out_specs=pl.BlockSpec((TOEP_BATCH, TOEP_ROWS, cols), lambda i: (i, 0, 0)),
        out_shape=jax.ShapeDtypeStruct((g, TOEP_ROWS, cols), F32),
        compiler_params=_cparams("parallel"),
    )(rr.reshape(g, 1, TOEP_LEN))


def _bias_of_distance(rel, dist, valid):
    bias = jnp.moveaxis(rel[_t5_bucket(dist)], -1, 0)
    return jnp.where(valid, bias * LOG2E, NEG).astype(F32)


def _dilated_bias_tiles(rel_a):
    c = jnp.arange(TOEP_LEN)
    rows = []
    for (_, dil) in DIL_PATTERNS:
        d0 = c - (TOEP_ROWS - 1) + DIL_W
        d1 = c - (TOEP_ROWS - 1)
        rows.append(jnp.stack([_bias_of_distance(rel_a, d0 * dil, (d0 >= 0) & (d0 <= DIL_W)),
                               _bias_of_distance(rel_a, d1 * dil, d1 >= 0)]))
    rr = jnp.stack(rows)
    out = _toeplitz(rr.reshape(-1, TOEP_LEN), DIL_W)
    return out.reshape(len(DIL_PATTERNS), 2, H_A, TOEP_ROWS, DIL_W)


def _moba_num_tiles(seq):
    blk = MOBA_BLOCK
    d_all = np.arange(max(seq, 2))
    nf = np.maximum(d_all, 1).astype(np.float32)
    half = N_BUCKETS // 2
    big = half + (np.log(nf / half) / math.log(REL_MAX_DIST / half) * (N_BUCKETS - half)).astype(np.int32)
    bucket = np.where(d_all < half, d_all, np.minimum(big, N_BUCKETS - 1))
    last_change = int(np.max(np.nonzero(np.diff(bucket))[0])) + 1
    first_const = min(seq // blk - 1, -(-(last_change + 2 + blk - 1) // blk))
    return first_const + 1


def _moba_bias_tiles(rel_b, seq):
    ntile = _moba_num_tiles(seq)
    c = jnp.arange(TOEP_LEN)[None, :]
    d = jnp.arange(ntile)[:, None] * MOBA_BLOCK + c - (TOEP_ROWS - 1)
    rr = _bias_of_distance(rel_b, d, d >= 0)
    out = _toeplitz(rr.reshape(-1, TOEP_LEN), MOBA_BLOCK)
    return out.reshape(H_B, ntile, TOEP_ROWS, MOBA_BLOCK)


def _head_rows():
    row = lax.broadcasted_iota(jnp.int32, (LANES, 1), 0)
    return row < HEAD_DIM, row >= HEAD_DIM


DIL_UNROLL = 16


def _dilated_kernel(q_ref, k_ref, v_ref, bm_ref, o_ref, og_s, lse_s, sblk_s, pblk_s, vts_s, *, seq):
    w = DIL_W
    own_rows = _head_rows()
    for g, (_, dil) in enumerate(DIL_PATTERNS):
        nb = (seq // dil) // w

        def rows(start, size, dil=dil):
            return pl.ds(start, size) if dil == 1 else pl.ds(start, size, stride=dil)

        def trip(it, carry, g=g, dil=dil, nb=nb, rows=rows):
            metas = []
            k = None
            for u in range(DIL_UNROLL):
                idx = it * DIL_UNROLL + u
                r = idx // nb
                n = idx % nb
                q_rows = rows(n * w * dil + r, w)
                var = jnp.where(n == 0, 1, 0)
                qt = q_ref[q_rows, :].T * (ATTN_SCALE * LOG2E)
                kv_slot = u - 1 if (nb == 2 and u % 2 == 1) else u
                if kv_slot == u:
                    kv_rows = rows(jnp.maximum(n - 1, 0) * w * dil + r, 2 * w)
                    k = k_ref[kv_rows, :].astype(BF16)
                    vt = v_ref[kv_rows, :].T
                    for h in range(2):
                        vts_s[u, h] = jnp.concatenate(
                            [vt[h * HEAD_DIM:(h + 1) * HEAD_DIM, :], _ones_rows(2 * w)],
                            axis=0).astype(BF16)
                tops = []
                for h in range(2):
                    s = jnp.dot(k, jnp.where(own_rows[h], qt, 0.0).astype(BF16),
                                preferred_element_type=F32) + bm_ref[g, var, h]
                    sblk_s[u, h] = s
                    tops.append(jnp.max(s, axis=0, keepdims=True))
                metas.append((q_rows, tops, kv_slot))
            pvs = []
            for u, (_, tops, kv_slot) in enumerate(metas):
                for h in range(2):
                    pblk_s[u, h] = jnp.exp2(sblk_s[u, h] - tops[h]).astype(BF16)
                    pvs.append(jnp.dot(vts_s[kv_slot, h], pblk_s[u, h], preferred_element_type=F32))
            for u, (q_rows, tops, _) in enumerate(metas):
                pair = pvs[2 * u:2 * u + 2]
                o_t = jnp.concatenate([pv[:HEAD_DIM] / pv[HEAD_DIM:HEAD_DIM + 1] for pv in pair], axis=0)
                lse_t = jnp.concatenate(
                    [jnp.broadcast_to(m + jnp.log(pv[HEAD_DIM:HEAD_DIM + 1]) * LOG2E, (HEAD_DIM, w))
                     for m, pv in zip(tops, pair)], axis=0)
                og_s[g, q_rows, :] = o_t.T
                lse_s[g, q_rows, :] = lse_t.T
            return carry

        lax.fori_loop(0, nb * dil // DIL_UNROLL, trip, 0)

    top = jnp.maximum(jnp.maximum(lse_s[0], lse_s[1]), lse_s[2])
    num = jnp.zeros_like(top)
    den = jnp.zeros_like(top)
    for g in range(len(DIL_PATTERNS)):
        wg = jnp.exp2(lse_s[g] - top)
        num = num + wg * og_s[g]
        den = den + wg
    o_ref[...] = num / den


def _dilated_attention(qkv, bm_a, bsz, seq):
    npair = H_A // 2
    npat = len(DIL_PATTERNS)
    kern = functools.partial(_dilated_kernel, seq=seq)

    def col(off):
        return pl.BlockSpec((None, seq, LANES), lambda p, b, off=off: (b, 0, off + p))

    return pl.pallas_call(
        kern,
        grid=(npair, bsz),
        in_specs=[col(0), col(npair), col(2 * npair),
                  pl.BlockSpec((npat, 2, 2, 2 * DIL_W, DIL_W), lambda p, b: (0, 0, p, 0, 0))],
        out_specs=pl.BlockSpec((None, seq, LANES), lambda p, b: (b, 0, p)),
        out_shape=jax.ShapeDtypeStruct((bsz, seq, W_A), F32),
        scratch_shapes=[pltpu.VMEM((npat, seq, LANES), F32)] * 2 + [
            pltpu.VMEM((DIL_UNROLL, 2, 2 * DIL_W, DIL_W), F32),
            pltpu.VMEM((DIL_UNROLL, 2, 2 * DIL_W, DIL_W), BF16),
            pltpu.VMEM((DIL_UNROLL, 2, VT_ROWS, 2 * DIL_W), BF16)],
        compiler_params=_cparams("parallel", "parallel"),
    )(qkv, qkv, qkv, bm_a)


VT_ROWS = HEAD_DIM + 16


def _ones_rows(width):
    row = lax.broadcasted_iota(jnp.int32, (VT_ROWS - HEAD_DIM, width), 0)
    return jnp.where(row == 0, 1.0, 0.0)


def _transpose_to(dst_ref, src_ref, chunk=512):
    for c in range(src_ref.shape[0] // chunk):
        cols = slice(c * chunk, (c + 1) * chunk)
        t = src_ref[cols, :].T
        for h in range(2):
            dst_ref[h, :, cols] = jnp.concatenate(
                [t[h * HEAD_DIM:(h + 1) * HEAD_DIM, :], _ones_rows(chunk)], axis=0).astype(BF16)


FLASH_CHUNK = 2


def _own_tile(qi, tk, tq):
    col = lax.broadcasted_iota(jnp.int32, (1, tq), 1)
    return qi * (tq // tk) + col // tk


def _causal_rows(qi, tk, nblk, o_ref, kaug_s, vt_s, qa_s, ssc, psc, acc_s, bias_fn):
    ck = FLASH_CHUNK * tk
    tq = ck

    def stage(c, u):
        r0 = (c % 2) * ck + u * tk
        return slice(r0, r0 + tk)

    def scores(c, h, maybe_diag):
        top = None
        for u in range(FLASH_CHUNK):
            kj = c * FLASH_CHUNK + u
            s = jnp.dot(kaug_s[h, kj * tk:(kj + 1) * tk, :], qa_s[h], preferred_element_type=F32)
            s = bias_fn(h, kj, s, maybe_diag)
            ssc[h, stage(c, u), :] = s
            cmax = jnp.max(s, axis=0, keepdims=True)
            top = cmax if top is None else jnp.maximum(top, cmax)
        return top

    def values(c, h, top, m):
        for u in range(FLASH_CHUNK):
            psc[h, stage(c, u), :] = jnp.exp2(ssc[h, stage(c, u), :] - top).astype(BF16)
        slot = slice((c % 2) * ck, (c % 2 + 1) * ck)
        pv = jnp.dot(vt_s[h, :, c * ck:(c + 1) * ck], psc[h, slot, :], preferred_element_type=F32)
        acc_s[h] = pv if m is None else jnp.exp2(m - top) * acc_s[h] + pv

    def row_pass(nchunk):
        tops = {(0, h): scores(0, h, nchunk == 1) for h in range(2)}
        for c in range(nchunk):
            if c + 1 < nchunk:
                for h in range(2):
                    tops[(c + 1, h)] = jnp.maximum(tops[(c, h)], scores(c + 1, h, c + 2 == nchunk))
            for h in range(2):
                values(c, h, tops[(c, h)], tops[(c - 1, h)] if c else None)
        return jnp.concatenate([acc_s[h, :HEAD_DIM] / acc_s[h, HEAD_DIM:HEAD_DIM + 1]
                                for h in range(2)], axis=0).T

    for nchunk in range(1, nblk // FLASH_CHUNK + 1):
        @pl.when(qi == nchunk - 1)
        def _(nchunk=nchunk):
            o_ref[...] = row_pass(nchunk)


def _moba_kernel(q_ref, k_ref, v_ref, bt_ref, o_ref,
                 kaug_s, vt_s, kmean_s, qa_s, ssc, psc, acc_s, *, nblk, ntile):
    blk = MOBA_BLOCK
    qi = pl.program_id(2)
    spare = LANES - HEAD_DIM - nblk

    @pl.when(qi == 0)
    def _():
        lane = lax.broadcasted_iota(jnp.int32, (1, LANES), 1)
        for n in range(nblk):
            k = k_ref[n * blk:(n + 1) * blk, :]
            kaug_s[0, n * blk:(n + 1) * blk, :] = jnp.where(
                lane < HEAD_DIM, k, (lane == HEAD_DIM + n).astype(F32)).astype(BF16)
            kaug_s[1, n * blk:(n + 1) * blk, :] = jnp.where(
                lane >= HEAD_DIM, k, (lane == n).astype(F32)).astype(BF16)
            kmean_s[n:n + 1, :] = jnp.mean(k, axis=0, keepdims=True)
        _transpose_to(vt_s, v_ref)

    tq = FLASH_CHUNK * blk
    qt = q_ref[...].T
    own_rows = _head_rows()
    blk_id = lax.broadcasted_iota(jnp.int32, (nblk, 1), 0)
    own_blk = _own_tile(qi, blk, tq)
    past = blk_id < own_blk
    for h in range(2):
        qh = jnp.where(own_rows[h], qt, 0.0)
        gate = jnp.dot(kmean_s[...], qh, precision=lax.Precision.HIGHEST,
                       preferred_element_type=F32)
        gate = jnp.where(past, gate, NEG)
        beaten_by = jnp.zeros(gate.shape, F32)
        for m in range(nblk):
            gm = gate[m:m + 1, :]
            wins = jnp.logical_or(gm > gate, jnp.logical_and(gm == gate, blk_id > m))
            beaten_by = beaten_by + jnp.where(wins, 1.0, 0.0)
        chosen = beaten_by < float(MOBA_TOPK)
        keep = jnp.logical_or(jnp.logical_and(chosen, past), blk_id == own_blk)
        selb = jnp.where(keep, 0.0, NEG)
        qs = qt[h * HEAD_DIM:(h + 1) * HEAD_DIM, :] * (ATTN_SCALE * LOG2E)
        pad = jnp.zeros((spare, tq), F32)
        parts = [qs, selb, pad] if h == 0 else [selb, pad, qs]
        qa_s[h] = jnp.concatenate(parts, axis=0).astype(BF16)

    def bias_fn(h, kj, s, _):
        tiles = [bt_ref[h, jnp.clip(qi * FLASH_CHUNK + a - kj, 0, ntile - 1)]
                 for a in range(FLASH_CHUNK)]
        return s + jnp.concatenate(tiles, axis=1)

    _causal_rows(qi, blk, nblk, o_ref, kaug_s, vt_s, qa_s, ssc, psc, acc_s, bias_fn)


def _moba_attention(qkv, bt_b, bsz, seq, col0):
    npair = H_B // 2
    nblk = seq // MOBA_BLOCK
    ntile = bt_b.shape[1]
    assert nblk % 8 == 0 and nblk <= LANES - HEAD_DIM
    assert nblk % FLASH_CHUNK == 0
    kern = functools.partial(_moba_kernel, nblk=nblk, ntile=ntile)
    blk = MOBA_BLOCK
    tq = FLASH_CHUNK * blk

    def col(off):
        return pl.BlockSpec((None, seq, LANES), lambda p, b, i, off=off: (b, 0, off + p))

    return pl.pallas_call(
        kern,
        grid=(npair, bsz, seq // tq),
        in_specs=[pl.BlockSpec((None, tq, LANES), lambda p, b, i: (b, i, col0 + p)),
                  col(col0 + npair), col(col0 + 2 * npair),
                  pl.BlockSpec((2, ntile, blk, blk), lambda p, b, i: (p, 0, 0, 0))],
        out_specs=pl.BlockSpec((None, tq, LANES), lambda p, b, i: (b, i, p)),
        out_shape=jax.ShapeDtypeStruct((bsz, seq, W_B), F32),
        scratch_shapes=[pltpu.VMEM((2, seq, LANES), BF16), pltpu.VMEM((2, VT_ROWS, seq), BF16),
                        pltpu.VMEM((nblk, LANES), F32), pltpu.VMEM((2, LANES, tq), BF16),
                        pltpu.VMEM((2, 2 * tq, tq), F32), pltpu.VMEM((2, 2 * tq, tq), BF16),
                        pltpu.VMEM((2, VT_ROWS, tq), F32)],
        compiler_params=_cparams("parallel", "parallel", "arbitrary"),
    )(qkv, qkv, qkv, bt_b)


CUM_SPLIT = 3
CUM_CHUNK = 256
FOX_TILE = 256


def _fgate_kernel(f_ref, bf_ref, pm_ref, o_ref, cum_s):
    seq = f_ref.shape[0]
    ch = CUM_CHUNK
    i = lax.broadcasted_iota(jnp.int32, (ch, ch), 0)
    j = lax.broadcasted_iota(jnp.int32, (ch, ch), 1)
    lower = (j <= i).astype(F32)
    carry = jnp.zeros((1, LANES), F32)
    for c in range(seq // ch):
        z = f_ref[c * ch:(c + 1) * ch, :] + bf_ref[...]
        ls = jnp.minimum(z, 0.0) - jnp.log(1.0 + jnp.exp(-jnp.abs(z)))
        cum = jnp.dot(lower, ls, precision=lax.Precision.HIGHEST, preferred_element_type=F32) + carry
        cum_s[c * ch:(c + 1) * ch, :] = cum
        carry = cum[ch - 1:ch, :]
    rest = cum_s[...] * (-LOG2E)
    out = None
    for t in range(CUM_SPLIT):
        term = rest.astype(BF16)
        rest = rest - term.astype(F32)
        part = jnp.dot(term, pm_ref[t], preferred_element_type=F32)
        out = part if out is None else out + part
    o_ref[...] = out.astype(BF16)


def _forget_keys(f, b_f):
    bsz, seq, _ = f.shape
    npair = H_C // 2
    nblk = seq // FOX_TILE
    pm = np.zeros((CUM_SPLIT, LANES, npair * LANES), np.float32)
    for t in range(CUM_SPLIT):
        for p in range(npair):
            pm[t, 2 * p, p * LANES + HEAD_DIM + nblk + t] = 1.0
            pm[t, 2 * p + 1, p * LANES + nblk + t] = 1.0
    bf = jnp.pad(b_f, (0, LANES - b_f.shape[0])).reshape(1, LANES)
    return pl.pallas_call(
        _fgate_kernel,
        grid=(bsz,),
        in_specs=[pl.BlockSpec((None, seq, LANES), lambda b: (b, 0, 0)),
                  pl.BlockSpec((1, LANES), lambda b: (0, 0)),
                  pl.BlockSpec((CUM_SPLIT, LANES, npair * LANES), lambda b: (0, 0, 0))],
        out_specs=pl.BlockSpec((None, seq, npair * LANES), lambda b: (b, 0, 0)),
        out_shape=jax.ShapeDtypeStruct((bsz, seq, npair * LANES), BF16),
        scratch_shapes=[pltpu.VMEM((seq, LANES), F32)],
        compiler_params=_cparams("parallel"),
    )(f, bf, jnp.asarray(pm, BF16))


def _fox_kernel(q_ref, k_ref, v_ref, ca_ref, o_ref,
                kaug_s, vt_s, qa_s, cm_s, ssc, psc, acc_s, *, tk, nblk):
    qi = pl.program_id(2)
    spare = LANES - HEAD_DIM - nblk - 8

    @pl.when(qi == 0)
    def _():
        lane = lax.broadcasted_iota(jnp.int32, (1, LANES), 1)
        for n in range(nblk):
            k = k_ref[n * tk:(n + 1) * tk, :]
            ca = ca_ref[n * tk:(n + 1) * tk, :].astype(F32)
            kaug_s[0, n * tk:(n + 1) * tk, :] = jnp.where(
                lane < HEAD_DIM, k, ca + (lane == HEAD_DIM + n).astype(F32)).astype(BF16)
            kaug_s[1, n * tk:(n + 1) * tk, :] = jnp.where(
                lane >= HEAD_DIM, k, ca + (lane == n).astype(F32)).astype(BF16)
        _transpose_to(vt_s, v_ref)
        key = lax.broadcasted_iota(jnp.int32, (tk, tk), 0)
        qry = lax.broadcasted_iota(jnp.int32, (tk, tk), 1)
        cm_s[0] = jnp.zeros((tk, tk), F32)
        cm_s[1] = jnp.where(key <= qry, 0.0, NEG)

    tq = FLASH_CHUNK * tk
    qt = q_ref[...].T * (ATTN_SCALE * LOG2E)
    blk_id = lax.broadcasted_iota(jnp.int32, (nblk, 1), 0)
    future = jnp.where(blk_id <= _own_tile(qi, tk, tq), 0.0, NEG)
    sub = lax.broadcasted_iota(jnp.int32, (8, 1), 0)
    ones = jnp.broadcast_to(jnp.where(sub < CUM_SPLIT, 1.0, 0.0), (8, tq))
    pad = jnp.zeros((spare, tq), F32)
    for h in range(2):
        qs = qt[h * HEAD_DIM:(h + 1) * HEAD_DIM, :]
        parts = [qs, future, ones, pad] if h == 0 else [future, ones, pad, qs]
        qa_s[h] = jnp.concatenate(parts, axis=0).astype(BF16)

    def bias_fn(h, kj, s, maybe_diag):
        if not maybe_diag:
            return s
        tiles = [cm_s[jnp.where(kj == qi * FLASH_CHUNK + a, 1, 0)] for a in range(FLASH_CHUNK)]
        return s + jnp.concatenate(tiles, axis=1)

    _causal_rows(qi, tk, nblk, o_ref, kaug_s, vt_s, qa_s, ssc, psc, acc_s, bias_fn)


def _fox_attention(qkv, caug, bsz, seq, col0):
    npair = H_C // 2
    tk = FOX_TILE
    tq = FLASH_CHUNK * tk
    nblk = seq // tk
    assert nblk % 8 == 0 and nblk % FLASH_CHUNK == 0 and nblk + 8 <= LANES - HEAD_DIM
    kern = functools.partial(_fox_kernel, tk=tk, nblk=nblk)

    def col(off):
        return pl.BlockSpec((None, seq, LANES), lambda p, b, i, off=off: (b, 0, off + p))

    return pl.pallas_call(
        kern,
        grid=(npair, bsz, seq // tq),
        in_specs=[pl.BlockSpec((None, tq, LANES), lambda p, b, i: (b, i, col0 + p)),
                  col(col0 + npair), col(col0 + 2 * npair),
                  pl.BlockSpec((None, seq, LANES), lambda p, b, i: (b, 0, p))],
        out_specs=pl.BlockSpec((None, tq, LANES), lambda p, b, i: (b, i, p)),
        out_shape=jax.ShapeDtypeStruct((bsz, seq, W_C), F32),
        scratch_shapes=[pltpu.VMEM((2, seq, LANES), BF16), pltpu.VMEM((2, VT_ROWS, seq), BF16),
                        pltpu.VMEM((2, LANES, tq), BF16), pltpu.VMEM((2, tk, tk), F32),
                        pltpu.VMEM((2, 2 * tq, tq), F32), pltpu.VMEM((2, 2 * tq, tq), BF16),
                        pltpu.VMEM((2, VT_ROWS, tq), F32)],
        compiler_params=_cparams("parallel", "parallel", "arbitrary"),
    )(qkv, qkv, qkv, caug)


def _layer_norm_rows(z, g, b):
    mu = jnp.mean(z, axis=-1, keepdims=True)
    zc = z - mu
    var = jnp.mean(zc * zc, axis=-1, keepdims=True)
    return zc * lax.rsqrt(var + LN_EPS) * g + b


def _mixout_kernel(oa_ref, ob_ref, oc_ref, ga_ref, gb_ref, gc_ref, wa_ref, wb_ref, wc_ref,
                   x_ref, g1_ref, lng_ref, lnb_ref, sc_ref, sh_ref, x_out, h_out, *, alpha):
    def normed(o_ref, g_ref, rows):
        o = o_ref[rows, :]
        ms = jnp.mean(o * o, axis=-1, keepdims=True)
        return (o * lax.rsqrt(ms + LN_EPS) * g_ref[...]).astype(BF16)

    blocks = [slice(r0, r0 + MIX_ROWS) for r0 in range(0, x_ref.shape[0], MIX_ROWS)]
    ys = []
    for rows in blocks:
        y = jnp.dot(normed(oa_ref, ga_ref, rows), wa_ref[...], preferred_element_type=F32)
        y = y + jnp.dot(normed(ob_ref, gb_ref, rows), wb_ref[...], preferred_element_type=F32)
        ys.append(y + jnp.dot(normed(oc_ref, gc_ref, rows), wc_ref[...], preferred_element_type=F32))
    for rows, y in zip(blocks, ys):
        x1 = _layer_norm_rows(alpha * x_ref[rows, :] + (1.0 + g1_ref[...]) * y,
                              lng_ref[...], lnb_ref[...])
        x_out[rows, :] = x1
        h_out[rows, :] = (x1 * (1.0 + sc_ref[...]) + sh_ref[...]).astype(BF16)


MIX_ROWS = 256


def _mix_out(o_a, o_b, o_c, g_a, g_b, g_c, w_out_bf, x, g1, ln_g, ln_b, sc2, sh2, alpha):
    bsz, seq, d = x.shape
    tm = 512
    w_a, w_b, w_c = w_out_bf[:W_A], w_out_bf[W_A:W_A + W_B], w_out_bf[W_A + W_B:]

    def act(width):
        return pl.BlockSpec((None, tm, width), lambda b, i: (b, i, 0))

    def const(shape):
        return pl.BlockSpec(shape, lambda b, i: (0,) * len(shape), pipeline_mode=pl.Buffered(1))

    bvec = pl.BlockSpec((None, 1, d), lambda b, i: (b, 0, 0))
    return pl.pallas_call(
        functools.partial(_mixout_kernel, alpha=alpha),
        grid=(bsz, seq // tm),
        in_specs=[act(W_A), act(W_B), act(W_C),
                  const((1, W_A)), const((1, W_B)), const((1, W_C)),
                  const((W_A, d)), const((W_B, d)), const((W_C, d)),
                  act(d), bvec, const((1, d)), const((1, d)), bvec, bvec],
        out_specs=[act(d), act(d)],
        out_shape=[jax.ShapeDtypeStruct((bsz, seq, d), F32),
                   jax.ShapeDtypeStruct((bsz, seq, d), BF16)],
        compiler_params=_cparams("parallel", "parallel"),
    )(o_a, o_b, o_c, g_a.reshape(1, -1), g_b.reshape(1, -1), g_c.reshape(1, -1),
      w_a, w_b, w_c, x, g1, ln_g.reshape(1, d), ln_b.reshape(1, d), sc2, sh2)


FFN_HALO = 8
FFN_SUB = 256
FFN_ROWS = 256


def _ffn_up_kernel(h_ref, wa_ref, wb_ref, cwa_ref, cwb_ref, cba_ref, cbb_ref, o_ref,
                   bufa, bufb, wa_bf, wb_bf):
    i = pl.program_id(2)
    tm, tn = o_ref.shape

    @pl.when(jnp.logical_and(pl.program_id(1) == 0, i == 0))
    def _():
        wa_bf[...] = wa_ref[...].astype(BF16)
        wb_bf[...] = wb_ref[...].astype(BF16)

    @pl.when(i == 0)
    def _():
        bufa[0:FFN_HALO, :] = jnp.zeros((FFN_HALO, tn), F32)
        bufb[0:FFN_HALO, :] = jnp.zeros((FFN_HALO, tn), F32)

    subs = [(r * FFN_ROWS, slice(c * FFN_SUB, (c + 1) * FFN_SUB))
            for r in range(tm // FFN_ROWS) for c in range(tn // FFN_SUB)]
    prods = []
    for r0, cs in subs:
        h = h_ref[r0:r0 + FFN_ROWS, :]
        prods.append((jnp.dot(h, wa_bf[:, cs], preferred_element_type=F32),
                      jnp.dot(h, wb_bf[:, cs], preferred_element_type=F32)))

    def conv(u, cw_ref, cb_ref, buf, r0, cs):
        base = FFN_HALO + r0
        buf[base:base + FFN_ROWS, cs] = u
        return (cw_ref[2:3, cs] * u
                + cw_ref[1:2, cs] * buf[base - 1:base - 1 + FFN_ROWS, cs]
                + cw_ref[0:1, cs] * buf[base - 2:base - 2 + FFN_ROWS, cs]
                + cb_ref[:, cs])

    for (r0, cs), (ua, ub) in zip(subs, prods):
        a = conv(ua, cwa_ref, cba_ref, bufa, r0, cs)
        b = conv(ub, cwb_ref, cbb_ref, bufb, r0, cs)
        o_ref[r0:r0 + FFN_ROWS, cs] = (a * (1.0 / (1.0 + jnp.exp(-a))) * b).astype(o_ref.dtype)
    for buf in (bufa, bufb):
        buf[0:FFN_HALO, :] = buf[tm:tm + FFN_HALO, :]


def _ffn_up(h, w_up, layer, conv_w, conv_b):
    bsz, seq, d = h.shape
    dff = w_up.shape[2] // 2
    tm, tn = 1024, 512
    nj = dff // tn
    cb = conv_b.reshape(1, -1)
    return pl.pallas_call(
        _ffn_up_kernel,
        grid=(nj, bsz, seq // tm),
        in_specs=[pl.BlockSpec((None, tm, d), lambda j, b, i: (b, i, 0)),
                  pl.BlockSpec((None, d, tn), lambda j, b, i: (layer, 0, j)),
                  pl.BlockSpec((None, d, tn), lambda j, b, i: (layer, 0, nj + j)),
                  pl.BlockSpec((CONV_WIDTH, tn), lambda j, b, i: (0, j)),
                  pl.BlockSpec((CONV_WIDTH, tn), lambda j, b, i: (0, nj + j)),
                  pl.BlockSpec((1, tn), lambda j, b, i: (0, j)),
                  pl.BlockSpec((1, tn), lambda j, b, i: (0, nj + j))],
        out_specs=pl.BlockSpec((None, tm, tn), lambda j, b, i: (b, i, j)),
        out_shape=jax.ShapeDtypeStruct((bsz, seq, dff), BF16),
        scratch_shapes=[pltpu.VMEM((tm + FFN_HALO, tn), F32)] * 2 + [pltpu.VMEM((d, tn), BF16)] * 2,
        compiler_params=_cparams("parallel", "arbitrary", "arbitrary"),
    )(h, w_up, w_up, conv_w, conv_w, cb, cb)


def _ffn_down_kernel(g_ref, w_ref, x_ref, g2_ref, lng_ref, lnb_ref, sc_ref, sh_ref,
                     x_out, *maybe_h_out, alpha):
    y = jnp.dot(g_ref[...], w_ref[...], preferred_element_type=F32)
    x2 = _layer_norm_rows(alpha * x_ref[...] + (1.0 + g2_ref[...]) * y, lng_ref[...], lnb_ref[...])
    x_out[...] = x2
    for h_out in maybe_h_out:
        h_out[...] = (x2 * (1.0 + sc_ref[...]) + sh_ref[...]).astype(BF16)


def _ffn_down(g, w_down_bf, layer, x, g2, ln_g, ln_b, sc_next, sh_next, alpha, with_h):
    bsz, seq, d = x.shape
    dff = g.shape[2]
    tm = 256
    act = pl.BlockSpec((None, tm, d), lambda b, i: (b, i, 0))
    bvec = pl.BlockSpec((None, 1, d), lambda b, i: (b, 0, 0))
    const = pl.BlockSpec((1, d), lambda b, i: (0, 0))
    out_specs = [act, act] if with_h else [act]
    out_shape = [jax.ShapeDtypeStruct((bsz, seq, d), F32)]
    if with_h:
        out_shape.append(jax.ShapeDtypeStruct((bsz, seq, d), BF16))
    return pl.pallas_call(
        functools.partial(_ffn_down_kernel, alpha=alpha),
        grid=(bsz, seq // tm),
        in_specs=[pl.BlockSpec((None, tm, dff), lambda b, i: (b, i, 0)),
                  pl.BlockSpec((None, dff, d), lambda b, i: (layer, 0, 0),
                               pipeline_mode=pl.Buffered(1)),
                  act, bvec, const, const, bvec, bvec],
        out_specs=out_specs,
        out_shape=out_shape,
        compiler_params=_cparams("parallel", "parallel"),
    )(g, w_down_bf, x, g2, ln_g.reshape(1, d), ln_b.reshape(1, d), sc_next, sh_next)


def kernel(x, c, rel_bias, w_ada, b_ada, w_in, b_f, g_mix_a, g_mix_b, g_mix_c, w_out,
           ln1_g, ln1_b, w_up, conv_w, conv_b, w_down, ln2_g, ln2_b):
    bsz, seq, d = x.shape
    depth = w_ada.shape[0]
    alpha = (2 * depth) ** 0.25
    n_qkv = 3 * (W_A + W_B + W_C)

    mod = _ada_mod(c, w_ada, b_ada)

    def mod_vec(l, idx):
        return mod[l, :, idx * d:(idx + 1) * d].reshape(bsz, 1, d)

    bm_a = _dilated_bias_tiles(rel_bias[:, :H_A])
    bt_b = _moba_bias_tiles(rel_bias[:, H_A:], seq)

    w_in_bf, w_down_bf = w_in.astype(BF16), w_down.astype(BF16)

    for l in range(depth):
        w_f = jnp.pad(w_in_bf[l, :, n_qkv:], ((0, 0), (0, LANES - H_C)))
        if l == 0:
            qkv, f = _inproj_modulated(x, mod_vec(0, 1), mod_vec(0, 0), w_in_bf, l, n_qkv, w_f)
        else:
            qkv, f = _inproj(h.reshape(bsz * seq, d), w_in_bf, l, n_qkv, w_f)
        qkv = qkv.reshape(bsz, seq, n_qkv)
        caug = _forget_keys(f.reshape(bsz, seq, LANES), b_f[l])

        o_a = _dilated_attention(qkv, bm_a, bsz, seq)
        o_b = _moba_attention(qkv, bt_b, bsz, seq, 3 * W_A // LANES)
        o_c = _fox_attention(qkv, caug, bsz, seq, 3 * (W_A + W_B) // LANES)

        x, h = _mix_out(o_a, o_b, o_c, g_mix_a[l], g_mix_b[l], g_mix_c[l], w_out[l].astype(BF16),
                        x, mod_vec(l, 2), ln1_g[l], ln1_b[l], mod_vec(l, 4), mod_vec(l, 3), alpha)
        g = _ffn_up(h, w_up, l, conv_w[l], conv_b[l])
        last = l == depth - 1
        nxt = min(l + 1, depth - 1)
        outs = _ffn_down(g, w_down_bf, l, x, mod_vec(l, 5), ln2_g[l], ln2_b[l],
                         mod_vec(nxt, 1), mod_vec(nxt, 0), alpha, with_h=not last)
        x = outs[0]
        if not last:
            h = outs[1]
    return x
```

```python
import functools
import math

import numpy as np
import jax
import jax.numpy as jnp
from jax import lax
from jax.experimental import pallas as pl
from jax.experimental.pallas import tpu as pltpu

F32 = jnp.float32
BF16 = jnp.bfloat16

HEAD_DIM = 64
LANES = 128
H_A, H_B, H_C = 12, 10, 10
W_A, W_B, W_C = H_A * HEAD_DIM, H_B * HEAD_DIM, H_C * HEAD_DIM
DIL_PATTERNS = ((128, 1), (512, 4), (2048, 16))
DIL_W = 128
MOBA_BLOCK = 256
MOBA_TOPK = 3
N_BUCKETS = 32
REL_MAX_DIST = 2048
CONV_WIDTH = 3
LN_EPS = 1e-5
NEG = -1e30
ATTN_SCALE = HEAD_DIM ** -0.5
LOG2E = math.log2(math.e)
VMEM_LIMIT = 56 * 1024 * 1024


def _cparams(*sem):
    return pltpu.CompilerParams(dimension_semantics=sem, vmem_limit_bytes=VMEM_LIMIT)


def _t5_bucket(dist):
    n = jnp.maximum(dist, 0)
    max_exact = N_BUCKETS // 2
    nf = jnp.maximum(n, 1).astype(F32)
    large = max_exact + (jnp.log(nf / max_exact) / math.log(REL_MAX_DIST / max_exact)
                         * (N_BUCKETS - max_exact)).astype(jnp.int32)
    large = jnp.minimum(large, N_BUCKETS - 1)
    return jnp.where(n < max_exact, n, large)


def _ada_kernel(c_ref, w_ref, b_ref, o_ref):
    c = c_ref[...]
    cond = c * (1.0 / (1.0 + jnp.exp(-c)))
    o_ref[...] = jnp.dot(cond, w_ref[...], precision=lax.Precision.HIGHEST,
                         preferred_element_type=F32) + b_ref[...]


def _ada_mod(c, w_ada, b_ada):
    depth, d, n = w_ada.shape
    bsz = c.shape[0]
    tn = 1024
    return pl.pallas_call(
        _ada_kernel,
        grid=(depth, n // tn),
        in_specs=[pl.BlockSpec((bsz, d), lambda l, j: (0, 0)),
                  pl.BlockSpec((None, d, tn), lambda l, j: (l, 0, j)),
                  pl.BlockSpec((None, 1, tn), lambda l, j: (l, 0, j))],
        out_specs=pl.BlockSpec((None, bsz, tn), lambda l, j: (l, 0, j)),
        out_shape=jax.ShapeDtypeStruct((depth, bsz, n), F32),
        compiler_params=_cparams("parallel", "parallel"),
    )(c, w_ada, b_ada.reshape(depth, 1, n))


def _inproj_kernel(a_ref, w_ref, wf_ref, o_ref, f_ref):
    a = a_ref[...]
    o_ref[...] = jnp.dot(a, w_ref[...], preferred_element_type=F32)

    @pl.when(pl.program_id(1) == 0)
    def _():
        f_ref[...] = jnp.dot(a, wf_ref[...], preferred_element_type=F32)


def _inproj_modulated_kernel(x_ref, sc_ref, sh_ref, w_ref, wf_ref, o_ref, f_ref, h_s):
    @pl.when(pl.program_id(1) == 0)
    def _():
        h_s[...] = (x_ref[...] * (1.0 + sc_ref[...]) + sh_ref[...]).astype(BF16)
        f_ref[...] = jnp.dot(h_s[...], wf_ref[...], preferred_element_type=F32)

    o_ref[...] = jnp.dot(h_s[...], w_ref[...], preferred_element_type=F32)


def _inproj_modulated(x, sc, sh, w_in_bf, layer, n, w_f):
    bsz, seq, d = x.shape
    tm, tn = 1024, 1024
    per_b = seq // tm
    vec = pl.BlockSpec((None, 1, d), lambda i, j: (i // per_b, 0, 0))
    return pl.pallas_call(
        _inproj_modulated_kernel,
        grid=(bsz * per_b, n // tn),
        in_specs=[pl.BlockSpec((None, tm, d), lambda i, j: (i // per_b, i % per_b, 0)), vec, vec,
                  pl.BlockSpec((None, d, tn), lambda i, j: (layer, 0, j)),
                  pl.BlockSpec((d, LANES), lambda i, j: (0, 0))],
        out_specs=[pl.BlockSpec((tm, tn), lambda i, j: (i, j)),
                   pl.BlockSpec((tm, LANES), lambda i, j: (i, 0))],
        out_shape=[jax.ShapeDtypeStruct((bsz * seq, n), F32),
                   jax.ShapeDtypeStruct((bsz * seq, LANES), F32)],
        scratch_shapes=[pltpu.VMEM((tm, d), BF16)],
        compiler_params=_cparams("parallel", "arbitrary"),
    )(x, sc, sh, w_in_bf, w_f)


def _inproj(h, w_in_bf, layer, n, w_f):
    t, d = h.shape
    tm, tn = 1024, 1024
    return pl.pallas_call(
        _inproj_kernel,
        grid=(t // tm, n // tn),
        in_specs=[pl.BlockSpec((tm, d), lambda i, j: (i, 0)),
                  pl.BlockSpec((None, d, tn), lambda i, j: (layer, 0, j)),
                  pl.BlockSpec((d, LANES), lambda i, j: (0, 0))],
        out_specs=[pl.BlockSpec((tm, tn), lambda i, j: (i, j)),
                   pl.BlockSpec((tm, LANES), lambda i, j: (i, 0))],
        out_shape=[jax.ShapeDtypeStruct((t, n), F32),
                   jax.ShapeDtypeStruct((t, LANES), F32)],
        compiler_params=_cparams("parallel", "arbitrary"),
    )(h, w_in_bf, w_f)


TOEP_LEN = 512
TOEP_ROWS = 256


TOEP_BATCH = 8


def _toeplitz_kernel(rr_ref, o_ref):
    for t in range(TOEP_BATCH):
        x = jnp.broadcast_to(rr_ref[t], (TOEP_ROWS, TOEP_LEN))
        y = pltpu.roll(x, TOEP_LEN - TOEP_ROWS + 1, 1, stride=1, stride_axis=0)
        o_ref[t] = y[:, :o_ref.shape[2]]


def _toeplitz(rr, cols):
    g = rr.shape[0]
    assert g % TOEP_BATCH == 0
    return pl.pallas_call(
        _toeplitz_kernel,
        grid=(g // TOEP_BATCH,),
        in_specs=[pl.BlockSpec((TOEP_BATCH, 1, TOEP_LEN), lambda i: (i, 0, 0))],
        out_specs=pl.BlockSpec((TOEP_BATCH, TOEP_ROWS, cols), lambda i: (i, 0, 0)),
        out_shape=jax.ShapeDtypeStruct((g, TOEP_ROWS, cols), F32),
        compiler_params=_cparams("parallel"),
    )(rr.reshape(g, 1, TOEP_LEN))


def _bias_of_distance(rel, dist, valid):
    bias = jnp.moveaxis(rel[_t5_bucket(dist)], -1, 0)
    return jnp.where(valid, bias * LOG2E, NEG).astype(F32)


def _dilated_bias_tiles(rel_a):
    c = jnp.arange(TOEP_LEN)
    rows = []
    for (_, dil) in DIL_PATTERNS:
        d0 = c - (TOEP_ROWS - 1) + DIL_W
        d1 = c - (TOEP_ROWS - 1)
        rows.append(jnp.stack([_bias_of_distance(rel_a, d0 * dil, (d0 >= 0) & (d0 <= DIL_W)),
                               _bias_of_distance(rel_a, d1 * dil, d1 >= 0)]))
    rr = jnp.stack(rows)
    out = _toeplitz(rr.reshape(-1, TOEP_LEN), DIL_W)
    return out.reshape(len(DIL_PATTERNS), 2, H_A, TOEP_ROWS, DIL_W)


def _moba_num_tiles(seq):
    blk = MOBA_BLOCK
    d_all = np.arange(max(seq, 2))
    nf = np.maximum(d_all, 1).astype(np.float32)
    half = N_BUCKETS // 2
    big = half + (np.log(nf / half) / math.log(REL_MAX_DIST / half) * (N_BUCKETS - half)).astype(np.int32)
    bucket = np.where(d_all < half, d_all, np.minimum(big, N_BUCKETS - 1))
    last_change = int(np.max(np.nonzero(np.diff(bucket))[0])) + 1
    first_const = min(seq // blk - 1, -(-(last_change + 2 + blk - 1) // blk))
    return first_const + 1


def _moba_bias_tiles(rel_b, seq):
    ntile = _moba_num_tiles(seq)
    c = jnp.arange(TOEP_LEN)[None, :]
    d = jnp.arange(ntile)[:, None] * MOBA_BLOCK + c - (TOEP_ROWS - 1)
    rr = _bias_of_distance(rel_b, d, d >= 0)
    out = _toeplitz(rr.reshape(-1, TOEP_LEN), MOBA_BLOCK)
    return out.reshape(H_B, ntile, TOEP_ROWS, MOBA_BLOCK)


def _head_rows():
    row = lax.broadcasted_iota(jnp.int32, (LANES, 1), 0)
    return row < HEAD_DIM, row >= HEAD_DIM


DIL_UNROLL = 16


def _dilated_kernel(q_ref, k_ref, v_ref, bm_ref, o_ref, og_s, lse_s, sblk_s, pblk_s, vts_s, *, seq):
    w = DIL_W
    own_rows = _head_rows()
    for g, (_, dil) in enumerate(DIL_PATTERNS):
        nb = (seq // dil) // w

        def rows(start, size, dil=dil):
            return pl.ds(start, size) if dil == 1 else pl.ds(start, size, stride=dil)

        def trip(it, carry, g=g, dil=dil, nb=nb, rows=rows):
            metas = []
            k = None
            for u in range(DIL_UNROLL):
                idx = it * DIL_UNROLL + u
                r = idx // nb
                n = idx % nb
                q_rows = rows(n * w * dil + r, w)
                var = jnp.where(n == 0, 1, 0)
                qt = q_ref[q_rows, :].T * (ATTN_SCALE * LOG2E)
                kv_slot = u - 1 if (nb == 2 and u % 2 == 1) else u
                if kv_slot == u:
                    kv_rows = rows(jnp.maximum(n - 1, 0) * w * dil + r, 2 * w)
                    k = k_ref[kv_rows, :].astype(BF16)
                    vt = v_ref[kv_rows, :].T
                    for h in range(2):
                        vts_s[u, h] = jnp.concatenate(
                            [vt[h * HEAD_DIM:(h + 1) * HEAD_DIM, :], _ones_rows(2 * w)],
                            axis=0).astype(BF16)
                tops = []
                for h in range(2):
                    s = jnp.dot(k, jnp.where(own_rows[h], qt, 0.0).astype(BF16),
                                preferred_element_type=F32) + bm_ref[g, var, h]
                    sblk_s[u, h] = s
                    tops.append(jnp.max(s, axis=0, keepdims=True))
                metas.append((q_rows, tops, kv_slot))
            pvs = []
            for u, (_, tops, kv_slot) in enumerate(metas):
                for h in range(2):
                    pblk_s[u, h] = jnp.exp2(sblk_s[u, h] - tops[h]).astype(BF16)
                    pvs.append(jnp.dot(vts_s[kv_slot, h], pblk_s[u, h], preferred_element_type=F32))
            for u, (q_rows, tops, _) in enumerate(metas):
                pair = pvs[2 * u:2 * u + 2]
                o_t = jnp.concatenate([pv[:HEAD_DIM] / pv[HEAD_DIM:HEAD_DIM + 1] for pv in pair], axis=0)
                lse_t = jnp.concatenate(
                    [jnp.broadcast_to(m + jnp.log(pv[HEAD_DIM:HEAD_DIM + 1]) * LOG2E, (HEAD_DIM, w))
                     for m, pv in zip(tops, pair)], axis=0)
                og_s[g, q_rows, :] = o_t.T
                lse_s[g, q_rows, :] = lse_t.T
            return carry

        lax.fori_loop(0, nb * dil // DIL_UNROLL, trip, 0)

    top = jnp.maximum(jnp.maximum(lse_s[0], lse_s[1]), lse_s[2])
    num = jnp.zeros_like(top)
    den = jnp.zeros_like(top)
    for g in range(len(DIL_PATTERNS)):
        wg = jnp.exp2(lse_s[g] - top)
        num = num + wg * og_s[g]
        den = den + wg
    o_ref[...] = num / den


def _dilated_attention(qkv, bm_a, bsz, seq):
    npair = H_A // 2
    npat = len(DIL_PATTERNS)
    kern = functools.partial(_dilated_kernel, seq=seq)

    def col(off):
        return pl.BlockSpec((None, seq, LANES), lambda p, b, off=off: (b, 0, off + p))

    return pl.pallas_call(
        kern,
        grid=(npair, bsz),
        in_specs=[col(0), col(npair), col(2 * npair),
                  pl.BlockSpec((npat, 2, 2, 2 * DIL_W, DIL_W), lambda p, b: (0, 0, p, 0, 0))],
        out_specs=pl.BlockSpec((None, seq, LANES), lambda p, b: (b, 0, p)),
        out_shape=jax.ShapeDtypeStruct((bsz, seq, W_A), F32),
        scratch_shapes=[pltpu.VMEM((npat, seq, LANES), F32)] * 2 + [
            pltpu.VMEM((DIL_UNROLL, 2, 2 * DIL_W, DIL_W), F32),
            pltpu.VMEM((DIL_UNROLL, 2, 2 * DIL_W, DIL_W), BF16),
            pltpu.VMEM((DIL_UNROLL, 2, VT_ROWS, 2 * DIL_W), BF16)],
        compiler_params=_cparams("parallel", "parallel"),
    )(qkv, qkv, qkv, bm_a)


VT_ROWS = HEAD_DIM + 16


def _ones_rows(width):
    row = lax.broadcasted_iota(jnp.int32, (VT_ROWS - HEAD_DIM, width), 0)
    return jnp.where(row == 0, 1.0, 0.0)


def _transpose_to(dst_ref, src_ref, chunk=512):
    for c in range(src_ref.shape[0] // chunk):
        cols = slice(c * chunk, (c + 1) * chunk)
        t = src_ref[cols, :].T
        for h in range(2):
            dst_ref[h, :, cols] = jnp.concatenate(
                [t[h * HEAD_DIM:(h + 1) * HEAD_DIM, :], _ones_rows(chunk)], axis=0).astype(BF16)


FLASH_CHUNK = 2


def _own_tile(qi, tk, tq):
    col = lax.broadcasted_iota(jnp.int32, (1, tq), 1)
    return qi * (tq // tk) + col // tk


def _causal_rows(qi, tk, nblk, o_ref, kaug_s, vt_s, qa_s, ssc, psc, acc_s, bias_fn):
    ck = FLASH_CHUNK * tk
    tq = ck

    def stage(c, u):
        r0 = (c % 2) * ck + u * tk
        return slice(r0, r0 + tk)

    def scores(c, h, maybe_diag):
        top = None
        for u in range(FLASH_CHUNK):
            kj = c * FLASH_CHUNK + u
            s = jnp.dot(kaug_s[h, kj * tk:(kj + 1) * tk, :], qa_s[h], preferred_element_type=F32)
            s = bias_fn(h, kj, s, maybe_diag)
            ssc[h, stage(c, u), :] = s
            cmax = jnp.max(s, axis=0, keepdims=True)
            top = cmax if top is None else jnp.maximum(top, cmax)
        return top

    def values(c, h, top, m):
        for u in range(FLASH_CHUNK):
            psc[h, stage(c, u), :] = jnp.exp2(ssc[h, stage(c, u), :] - top).astype(BF16)
        slot = slice((c % 2) * ck, (c % 2 + 1) * ck)
        pv = jnp.dot(vt_s[h, :, c * ck:(c + 1) * ck], psc[h, slot, :], preferred_element_type=F32)
        acc_s[h] = pv if m is None else jnp.exp2(m - top) * acc_s[h] + pv

    def row_pass(nchunk):
        tops = {(0, h): scores(0, h, nchunk == 1) for h in range(2)}
        for c in range(nchunk):
            for h in range(2):
                if c + 1 < nchunk:
                    tops[(c + 1, h)] = jnp.maximum(tops[(c, h)], scores(c + 1, h, c + 2 == nchunk))
                values(c, h, tops[(c, h)], tops[(c - 1, h)] if c else None)
        return jnp.concatenate([acc_s[h, :HEAD_DIM] / acc_s[h, HEAD_DIM:HEAD_DIM + 1]
                                for h in range(2)], axis=0).T

    for nchunk in range(1, nblk // FLASH_CHUNK + 1):
        @pl.when(qi == nchunk - 1)
        def _(nchunk=nchunk):
            o_ref[...] = row_pass(nchunk)


def _moba_kernel(q_ref, k_ref, v_ref, bt_ref, o_ref,
                 kaug_s, vt_s, kmean_s, qa_s, ssc, psc, acc_s, *, nblk, ntile):
    blk = MOBA_BLOCK
    qi = pl.program_id(2)
    spare = LANES - HEAD_DIM - nblk

    @pl.when(qi == 0)
    def _():
        lane = lax.broadcasted_iota(jnp.int32, (1, LANES), 1)
        for n in range(nblk):
            k = k_ref[n * blk:(n + 1) * blk, :]
            kaug_s[0, n * blk:(n + 1) * blk, :] = jnp.where(
                lane < HEAD_DIM, k, (lane == HEAD_DIM + n).astype(F32)).astype(BF16)
            kaug_s[1, n * blk:(n + 1) * blk, :] = jnp.where(
                lane >= HEAD_DIM, k, (lane == n).astype(F32)).astype(BF16)
            kmean_s[n:n + 1, :] = jnp.mean(k, axis=0, keepdims=True)
        _transpose_to(vt_s, v_ref)

    tq = FLASH_CHUNK * blk
    qt = q_ref[...].T
    own_rows = _head_rows()
    blk_id = lax.broadcasted_iota(jnp.int32, (nblk, 1), 0)
    own_blk = _own_tile(qi, blk, tq)
    past = blk_id < own_blk
    for h in range(2):
        qh = jnp.where(own_rows[h], qt, 0.0)
        gate = jnp.dot(kmean_s[...], qh, precision=lax.Precision.HIGHEST,
                       preferred_element_type=F32)
        gate = jnp.where(past, gate, NEG)
        beaten_by = jnp.zeros(gate.shape, F32)
        for m in range(nblk):
            gm = gate[m:m + 1, :]
            wins = jnp.logical_or(gm > gate, jnp.logical_and(gm == gate, blk_id > m))
            beaten_by = beaten_by + jnp.where(wins, 1.0, 0.0)
        chosen = beaten_by < float(MOBA_TOPK)
        keep = jnp.logical_or(jnp.logical_and(chosen, past), blk_id == own_blk)
        selb = jnp.where(keep, 0.0, NEG)
        qs = qt[h * HEAD_DIM:(h + 1) * HEAD_DIM, :] * (ATTN_SCALE * LOG2E)
        pad = jnp.zeros((spare, tq), F32)
        parts = [qs, selb, pad] if h == 0 else [selb, pad, qs]
        qa_s[h] = jnp.concatenate(parts, axis=0).astype(BF16)

    def bias_fn(h, kj, s, _):
        tiles = [bt_ref[h, jnp.clip(qi * FLASH_CHUNK + a - kj, 0, ntile - 1)]
                 for a in range(FLASH_CHUNK)]
        return s + jnp.concatenate(tiles, axis=1)

    _causal_rows(qi, blk, nblk, o_ref, kaug_s, vt_s, qa_s, ssc, psc, acc_s, bias_fn)


def _moba_attention(qkv, bt_b, bsz, seq, col0):
    npair = H_B // 2
    nblk = seq // MOBA_BLOCK
    ntile = bt_b.shape[1]
    assert nblk % 8 == 0 and nblk <= LANES - HEAD_DIM
    assert nblk % FLASH_CHUNK == 0
    kern = functools.partial(_moba_kernel, nblk=nblk, ntile=ntile)
    blk = MOBA_BLOCK
    tq = FLASH_CHUNK * blk

    def col(off):
        return pl.BlockSpec((None, seq, LANES), lambda p, b, i, off=off: (b, 0, off + p))

    return pl.pallas_call(
        kern,
        grid=(npair, bsz, seq // tq),
        in_specs=[pl.BlockSpec((None, tq, LANES), lambda p, b, i: (b, i, col0 + p)),
                  col(col0 + npair), col(col0 + 2 * npair),
                  pl.BlockSpec((2, ntile, blk, blk), lambda p, b, i: (p, 0, 0, 0))],
        out_specs=pl.BlockSpec((None, tq, LANES), lambda p, b, i: (b, i, p)),
        out_shape=jax.ShapeDtypeStruct((bsz, seq, W_B), F32),
        scratch_shapes=[pltpu.VMEM((2, seq, LANES), BF16), pltpu.VMEM((2, VT_ROWS, seq), BF16),
                        pltpu.VMEM((nblk, LANES), F32), pltpu.VMEM((2, LANES, tq), BF16),
                        pltpu.VMEM((2, 2 * tq, tq), F32), pltpu.VMEM((2, 2 * tq, tq), BF16),
                        pltpu.VMEM((2, VT_ROWS, tq), F32)],
        compiler_params=_cparams("parallel", "parallel", "arbitrary"),
    )(qkv, qkv, qkv, bt_b)


CUM_SPLIT = 3
CUM_CHUNK = 256
FOX_TILE = 256


def _fgate_kernel(f_ref, bf_ref, pm_ref, o_ref, cum_s):
    seq = f_ref.shape[0]
    ch = CUM_CHUNK
    i = lax.broadcasted_iota(jnp.int32, (ch, ch), 0)
    j = lax.broadcasted_iota(jnp.int32, (ch, ch), 1)
    lower = (j <= i).astype(F32)
    carry = jnp.zeros((1, LANES), F32)
    for c in range(seq // ch):
        z = f_ref[c * ch:(c + 1) * ch, :] + bf_ref[...]
        ls = jnp.minimum(z, 0.0) - jnp.log(1.0 + jnp.exp(-jnp.abs(z)))
        cum = jnp.dot(lower, ls, precision=lax.Precision.HIGHEST, preferred_element_type=F32) + carry
        cum_s[c * ch:(c + 1) * ch, :] = cum
        carry = cum[ch - 1:ch, :]
    rest = cum_s[...] * (-LOG2E)
    out = None
    for t in range(CUM_SPLIT):
        term = rest.astype(BF16)
        rest = rest - term.astype(F32)
        part = jnp.dot(term, pm_ref[t], preferred_element_type=F32)
        out = part if out is None else out + part
    o_ref[...] = out.astype(BF16)


def _forget_keys(f, b_f):
    bsz, seq, _ = f.shape
    npair = H_C // 2
    nblk = seq // FOX_TILE
    pm = np.zeros((CUM_SPLIT, LANES, npair * LANES), np.float32)
    for t in range(CUM_SPLIT):
        for p in range(npair):
            pm[t, 2 * p, p * LANES + HEAD_DIM + nblk + t] = 1.0
            pm[t, 2 * p + 1, p * LANES + nblk + t] = 1.0
    bf = jnp.pad(b_f, (0, LANES - b_f.shape[0])).reshape(1, LANES)
    return pl.pallas_call(
        _fgate_kernel,
        grid=(bsz,),
        in_specs=[pl.BlockSpec((None, seq, LANES), lambda b: (b, 0, 0)),
                  pl.BlockSpec((1, LANES), lambda b: (0, 0)),
                  pl.BlockSpec((CUM_SPLIT, LANES, npair * LANES), lambda b: (0, 0, 0))],
        out_specs=pl.BlockSpec((None, seq, npair * LANES), lambda b: (b, 0, 0)),
        out_shape=jax.ShapeDtypeStruct((bsz, seq, npair * LANES), BF16),
        scratch_shapes=[pltpu.VMEM((seq, LANES), F32)],
        compiler_params=_cparams("parallel"),
    )(f, bf, jnp.asarray(pm, BF16))


def _fox_kernel(q_ref, k_ref, v_ref, ca_ref, o_ref,
                kaug_s, vt_s, qa_s, cm_s, ssc, psc, acc_s, *, tk, nblk):
    qi = pl.program_id(2)
    spare = LANES - HEAD_DIM - nblk - 8

    @pl.when(qi == 0)
    def _():
        lane = lax.broadcasted_iota(jnp.int32, (1, LANES), 1)
        for n in range(nblk):
            k = k_ref[n * tk:(n + 1) * tk, :]
            ca = ca_ref[n * tk:(n + 1) * tk, :].astype(F32)
            kaug_s[0, n * tk:(n + 1) * tk, :] = jnp.where(
                lane < HEAD_DIM, k, ca + (lane == HEAD_DIM + n).astype(F32)).astype(BF16)
            kaug_s[1, n * tk:(n + 1) * tk, :] = jnp.where(
                lane >= HEAD_DIM, k, ca + (lane == n).astype(F32)).astype(BF16)
        _transpose_to(vt_s, v_ref)
        key = lax.broadcasted_iota(jnp.int32, (tk, tk), 0)
        qry = lax.broadcasted_iota(jnp.int32, (tk, tk), 1)
        cm_s[0] = jnp.zeros((tk, tk), F32)
        cm_s[1] = jnp.where(key <= qry, 0.0, NEG)

    tq = FLASH_CHUNK * tk
    qt = q_ref[...].T * (ATTN_SCALE * LOG2E)
    blk_id = lax.broadcasted_iota(jnp.int32, (nblk, 1), 0)
    future = jnp.where(blk_id <= _own_tile(qi, tk, tq), 0.0, NEG)
    sub = lax.broadcasted_iota(jnp.int32, (8, 1), 0)
    ones = jnp.broadcast_to(jnp.where(sub < CUM_SPLIT, 1.0, 0.0), (8, tq))
    pad = jnp.zeros((spare, tq), F32)
    for h in range(2):
        qs = qt[h * HEAD_DIM:(h + 1) * HEAD_DIM, :]
        parts = [qs, future, ones, pad] if h == 0 else [future, ones, pad, qs]
        qa_s[h] = jnp.concatenate(parts, axis=0).astype(BF16)

    def bias_fn(h, kj, s, maybe_diag):
        if not maybe_diag:
            return s
        tiles = [cm_s[jnp.where(kj == qi * FLASH_CHUNK + a, 1, 0)] for a in range(FLASH_CHUNK)]
        return s + jnp.concatenate(tiles, axis=1)

    _causal_rows(qi, tk, nblk, o_ref, kaug_s, vt_s, qa_s, ssc, psc, acc_s, bias_fn)


def _fox_attention(qkv, caug, bsz, seq, col0):
    npair = H_C // 2
    tk = FOX_TILE
    tq = FLASH_CHUNK * tk
    nblk = seq // tk
    assert nblk % 8 == 0 and nblk % FLASH_CHUNK == 0 and nblk + 8 <= LANES - HEAD_DIM
    kern = functools.partial(_fox_kernel, tk=tk, nblk=nblk)

    def col(off):
        return pl.BlockSpec((None, seq, LANES), lambda p, b, i, off=off: (b, 0, off + p))

    return pl.pallas_call(
        kern,
        grid=(npair, bsz, seq // tq),
        in_specs=[pl.BlockSpec((None, tq, LANES), lambda p, b, i: (b, i, col0 + p)),
                  col(col0 + npair), col(col0 + 2 * npair),
                  pl.BlockSpec((None, seq, LANES), lambda p, b, i: (b, 0, p))],
        out_specs=pl.BlockSpec((None, tq, LANES), lambda p, b, i: (b, i, p)),
        out_shape=jax.ShapeDtypeStruct((bsz, seq, W_C), F32),
        scratch_shapes=[pltpu.VMEM((2, seq, LANES), BF16), pltpu.VMEM((2, VT_ROWS, seq), BF16),
                        pltpu.VMEM((2, LANES, tq), BF16), pltpu.VMEM((2, tk, tk), F32),
                        pltpu.VMEM((2, 2 * tq, tq), F32), pltpu.VMEM((2, 2 * tq, tq), BF16),
                        pltpu.VMEM((2, VT_ROWS, tq), F32)],
        compiler_params=_cparams("parallel", "parallel", "arbitrary"),
    )(qkv, qkv, qkv, caug)


def _layer_norm_rows(z, g, b):
    mu = jnp.mean(z, axis=-1, keepdims=True)
    zc = z - mu
    var = jnp.mean(zc * zc, axis=-1, keepdims=True)
    return zc * lax.rsqrt(var + LN_EPS) * g + b


def _mixout_kernel(oa_ref, ob_ref, oc_ref, ga_ref, gb_ref, gc_ref, wa_ref, wb_ref, wc_ref,
                   x_ref, g1_ref, lng_ref, lnb_ref, sc_ref, sh_ref, x_out, h_out, *, alpha):
    def normed(o_ref, g_ref, rows):
        o = o_ref[rows, :]
        ms = jnp.mean(o * o, axis=-1, keepdims=True)
        return (o * lax.rsqrt(ms + LN_EPS) * g_ref[...]).astype(BF16)

    blocks = [slice(r0, r0 + MIX_ROWS) for r0 in range(0, x_ref.shape[0], MIX_ROWS)]
    ys = []
    for rows in blocks:
        y = jnp.dot(normed(oa_ref, ga_ref, rows), wa_ref[...], preferred_element_type=F32)
        y = y + jnp.dot(normed(ob_ref, gb_ref, rows), wb_ref[...], preferred_element_type=F32)
        ys.append(y + jnp.dot(normed(oc_ref, gc_ref, rows), wc_ref[...], preferred_element_type=F32))
    for rows, y in zip(blocks, ys):
        x1 = _layer_norm_rows(alpha * x_ref[rows, :] + (1.0 + g1_ref[...]) * y,
                              lng_ref[...], lnb_ref[...])
        x_out[rows, :] = x1
        h_out[rows, :] = (x1 * (1.0 + sc_ref[...]) + sh_ref[...]).astype(BF16)


MIX_ROWS = 256


def _mix_out(o_a, o_b, o_c, g_a, g_b, g_c, w_out_bf, x, g1, ln_g, ln_b, sc2, sh2, alpha):
    bsz, seq, d = x.shape
    tm = 512
    w_a, w_b, w_c = w_out_bf[:W_A], w_out_bf[W_A:W_A + W_B], w_out_bf[W_A + W_B:]

    def act(width):
        return pl.BlockSpec((None, tm, width), lambda b, i: (b, i, 0))

    def const(shape):
        return pl.BlockSpec(shape, lambda b, i: (0,) * len(shape), pipeline_mode=pl.Buffered(1))

    bvec = pl.BlockSpec((None, 1, d), lambda b, i: (b, 0, 0))
    return pl.pallas_call(
        functools.partial(_mixout_kernel, alpha=alpha),
        grid=(bsz, seq // tm),
        in_specs=[act(W_A), act(W_B), act(W_C),
                  const((1, W_A)), const((1, W_B)), const((1, W_C)),
                  const((W_A, d)), const((W_B, d)), const((W_C, d)),
                  act(d), bvec, const((1, d)), const((1, d)), bvec, bvec],
        out_specs=[act(d), act(d)],
        out_shape=[jax.ShapeDtypeStruct((bsz, seq, d), F32),
                   jax.ShapeDtypeStruct((bsz, seq, d), BF16)],
        compiler_params=_cparams("parallel", "parallel"),
    )(o_a, o_b, o_c, g_a.reshape(1, -1), g_b.reshape(1, -1), g_c.reshape(1, -1),
      w_a, w_b, w_c, x, g1, ln_g.reshape(1, d), ln_b.reshape(1, d), sc2, sh2)


FFN_HALO = 8
FFN_SUB = 256
FFN_ROWS = 256


def _ffn_up_kernel(h_ref, wa_ref, wb_ref, cwa_ref, cwb_ref, cba_ref, cbb_ref, o_ref,
                   bufa, bufb, wa_bf, wb_bf):
    i = pl.program_id(2)
    tm, tn = o_ref.shape

    @pl.when(jnp.logical_and(pl.program_id(1) == 0, i == 0))
    def _():
        wa_bf[...] = wa_ref[...].astype(BF16)
        wb_bf[...] = wb_ref[...].astype(BF16)

    @pl.when(i == 0)
    def _():
        bufa[0:FFN_HALO, :] = jnp.zeros((FFN_HALO, tn), F32)
        bufb[0:FFN_HALO, :] = jnp.zeros((FFN_HALO, tn), F32)

    subs = [(r * FFN_ROWS, slice(c * FFN_SUB, (c + 1) * FFN_SUB))
            for r in range(tm // FFN_ROWS) for c in range(tn // FFN_SUB)]
    prods = []
    for r0, cs in subs:
        h = h_ref[r0:r0 + FFN_ROWS, :]
        prods.append((jnp.dot(h, wa_bf[:, cs], preferred_element_type=F32),
                      jnp.dot(h, wb_bf[:, cs], preferred_element_type=F32)))

    def conv(u, cw_ref, cb_ref, buf, r0, cs):
        base = FFN_HALO + r0
        buf[base:base + FFN_ROWS, cs] = u
        return (cw_ref[2:3, cs] * u
                + cw_ref[1:2, cs] * buf[base - 1:base - 1 + FFN_ROWS, cs]
                + cw_ref[0:1, cs] * buf[base - 2:base - 2 + FFN_ROWS, cs]
                + cb_ref[:, cs])

    for (r0, cs), (ua, ub) in zip(subs, prods):
        a = conv(ua, cwa_ref, cba_ref, bufa, r0, cs)
        b = conv(ub, cwb_ref, cbb_ref, bufb, r0, cs)
        o_ref[r0:r0 + FFN_ROWS, cs] = (a * (1.0 / (1.0 + jnp.exp(-a))) * b).astype(o_ref.dtype)
    for buf in (bufa, bufb):
        buf[0:FFN_HALO, :] = buf[tm:tm + FFN_HALO, :]


def _ffn_up(h, w_up, layer, conv_w, conv_b):
    bsz, seq, d = h.shape
    dff = w_up.shape[2] // 2
    tm, tn = 1024, 512
    nj = dff // tn
    cb = conv_b.reshape(1, -1)
    return pl.pallas_call(
        _ffn_up_kernel,
        grid=(nj, bsz, seq // tm),
        in_specs=[pl.BlockSpec((None, tm, d), lambda j, b, i: (b, i, 0)),
                  pl.BlockSpec((None, d, tn), lambda j, b, i: (layer, 0, j)),
                  pl.BlockSpec((None, d, tn), lambda j, b, i: (layer, 0, nj + j)),
                  pl.BlockSpec((CONV_WIDTH, tn), lambda j, b, i: (0, j)),
                  pl.BlockSpec((CONV_WIDTH, tn), lambda j, b, i: (0, nj + j)),
                  pl.BlockSpec((1, tn), lambda j, b, i: (0, j)),
                  pl.BlockSpec((1, tn), lambda j, b, i: (0, nj + j))],
        out_specs=pl.BlockSpec((None, tm, tn), lambda j, b, i: (b, i, j)),
        out_shape=jax.ShapeDtypeStruct((bsz, seq, dff), BF16),
        scratch_shapes=[pltpu.VMEM((tm + FFN_HALO, tn), F32)] * 2 + [pltpu.VMEM((d, tn), BF16)] * 2,
        compiler_params=_cparams("parallel", "arbitrary", "arbitrary"),
    )(h, w_up, w_up, conv_w, conv_w, cb, cb)


def _ffn_down_kernel(g_ref, w_ref, x_ref, g2_ref, lng_ref, lnb_ref, sc_ref, sh_ref,
                     x_out, *maybe_h_out, alpha):
    y = jnp.dot(g_ref[...], w_ref[...], preferred_element_type=F32)
    x2 = _layer_norm_rows(alpha * x_ref[...] + (1.0 + g2_ref[...]) * y, lng_ref[...], lnb_ref[...])
    x_out[...] = x2
    for h_out in maybe_h_out:
        h_out[...] = (x2 * (1.0 + sc_ref[...]) + sh_ref[...]).astype(BF16)


def _ffn_down(g, w_down_bf, layer, x, g2, ln_g, ln_b, sc_next, sh_next, alpha, with_h):
    bsz, seq, d = x.shape
    dff = g.shape[2]
    tm = 256
    act = pl.BlockSpec((None, tm, d), lambda b, i: (b, i, 0))
    bvec = pl.BlockSpec((None, 1, d), lambda b, i: (b, 0, 0))
    const = pl.BlockSpec((1, d), lambda b, i: (0, 0))
    out_specs = [act, act] if with_h else [act]
    out_shape = [jax.ShapeDtypeStruct((bsz, seq, d), F32)]
    if with_h:
        out_shape.append(jax.ShapeDtypeStruct((bsz, seq, d), BF16))
    return pl.pallas_call(
        functools.partial(_ffn_down_kernel, alpha=alpha),
        grid=(bsz, seq // tm),
        in_specs=[pl.BlockSpec((None, tm, dff), lambda b, i: (b, i, 0)),
                  pl.BlockSpec((None, dff, d), lambda b, i: (layer, 0, 0),
                               pipeline_mode=pl.Buffered(1)),
                  act, bvec, const, const, bvec, bvec],
        out_specs=out_specs,
        out_shape=out_shape,
        compiler_params=_cparams("parallel", "parallel"),
    )(g, w_down_bf, x, g2, ln_g.reshape(1, d), ln_b.reshape(1, d), sc_next, sh_next)


def kernel(x, c, rel_bias, w_ada, b_ada, w_in, b_f, g_mix_a, g_mix_b, g_mix_c, w_out,
           ln1_g, ln1_b, w_up, conv_w, conv_b, w_down, ln2_g, ln2_b):
    bsz, seq, d = x.shape
    depth = w_ada.shape[0]
    alpha = (2 * depth) ** 0.25
    n_qkv = 3 * (W_A + W_B + W_C)

    mod = _ada_mod(c, w_ada, b_ada)

    def mod_vec(l, idx):
        return mod[l, :, idx * d:(idx + 1) * d].reshape(bsz, 1, d)

    bm_a = _dilated_bias_tiles(rel_bias[:, :H_A])
    bt_b = _moba_bias_tiles(rel_bias[:, H_A:], seq)

    w_in_bf, w_down_bf = w_in.astype(BF16), w_down.astype(BF16)

    for l in range(depth):
        w_f = jnp.pad(w_in_bf[l, :, n_qkv:], ((0, 0), (0, LANES - H_C)))
        if l == 0:
            qkv, f = _inproj_modulated(x, mod_vec(0, 1), mod_vec(0, 0), w_in_bf, l, n_qkv, w_f)
        else:
            qkv, f = _inproj(h.reshape(bsz * seq, d), w_in_bf, l, n_qkv, w_f)
        qkv = qkv.reshape(bsz, seq, n_qkv)
        caug = _forget_keys(f.reshape(bsz, seq, LANES), b_f[l])

        o_a = _dilated_attention(qkv, bm_a, bsz, seq)
        o_b = _moba_attention(qkv, bt_b, bsz, seq, 3 * W_A // LANES)
        o_c = _fox_attention(qkv, caug, bsz, seq, 3 * (W_A + W_B) // LANES)

        x, h = _mix_out(o_a, o_b, o_c, g_mix_a[l], g_mix_b[l], g_mix_c[l], w_out[l].astype(BF16),
                        x, mod_vec(l, 2), ln1_g[l], ln1_b[l], mod_vec(l, 4), mod_vec(l, 3), alpha)
        g = _ffn_up(h, w_up, l, conv_w[l], conv_b[l])
        last = l == depth - 1
        nxt = min(l + 1, depth - 1)
        outs = _ffn_down(g, w_down_bf, l, x, mod_vec(l, 5), ln2_g[l], ln2_b[l],
                         mod_vec(nxt, 1), mod_vec(nxt, 0), alpha, with_h=not last)
        x = outs[0]
        if not last:
            h = outs[1]
    return x
```

```python
import functools
import math

import numpy as np
import jax
import jax.numpy as jnp
from jax import lax
from jax.experimental import pallas as pl
from jax.experimental.pallas import tpu as pltpu

F32 = jnp.float32
BF16 = jnp.bfloat16

HEAD_DIM = 64
LANES = 128
H_A, H_B, H_C = 12, 10, 10
W_A, W_B, W_C = H_A * HEAD_DIM, H_B * HEAD_DIM, H_C * HEAD_DIM
DIL_PATTERNS = ((128, 1), (512, 4), (2048, 16))
DIL_W = 128
MOBA_BLOCK = 256
MOBA_TOPK = 3
N_BUCKETS = 32
REL_MAX_DIST = 2048
CONV_WIDTH = 3
LN_EPS = 1e-5
NEG = -1e30
ATTN_SCALE = HEAD_DIM ** -0.5
LOG2E = math.log2(math.e)
VMEM_LIMIT = 56 * 1024 * 1024


def _cparams(*sem):
    return pltpu.CompilerParams(dimension_semantics=sem, vmem_limit_bytes=VMEM_LIMIT)


def _t5_bucket(dist):
    n = jnp.maximum(dist, 0)
    max_exact = N_BUCKETS // 2
    nf = jnp.maximum(n, 1).astype(F32)
    large = max_exact + (jnp.log(nf / max_exact) / math.log(REL_MAX_DIST / max_exact)
                         * (N_BUCKETS - max_exact)).astype(jnp.int32)
    large = jnp.minimum(large, N_BUCKETS - 1)
    return jnp.where(n < max_exact, n, large)


def _ada_kernel(c_ref, w_ref, b_ref, o_ref):
    c = c_ref[...]
    cond = c * (1.0 / (1.0 + jnp.exp(-c)))
    o_ref[...] = jnp.dot(cond, w_ref[...], precision=lax.Precision.HIGHEST,
                         preferred_element_type=F32) + b_ref[...]


def _ada_mod(c, w_ada, b_ada):
    depth, d, n = w_ada.shape
    bsz = c.shape[0]
    tn = 1024
    return pl.pallas_call(
        _ada_kernel,
        grid=(depth, n // tn),
        in_specs=[pl.BlockSpec((bsz, d), lambda l, j: (0, 0)),
                  pl.BlockSpec((None, d, tn), lambda l, j: (l, 0, j)),
                  pl.BlockSpec((None, 1, tn), lambda l, j: (l, 0, j))],
        out_specs=pl.BlockSpec((None, bsz, tn), lambda l, j: (l, 0, j)),
        out_shape=jax.ShapeDtypeStruct((depth, bsz, n), F32),
        compiler_params=_cparams("parallel", "parallel"),
    )(c, w_ada, b_ada.reshape(depth, 1, n))


def _inproj_kernel(a_ref, w_ref, wf_ref, o_ref, f_ref):
    a = a_ref[...]
    o_ref[...] = jnp.dot(a, w_ref[...], preferred_element_type=F32)

    @pl.when(pl.program_id(1) == 0)
    def _():
        f_ref[...] = jnp.dot(a, wf_ref[...], preferred_element_type=F32)


def _inproj_modulated_kernel(x_ref, sc_ref, sh_ref, w_ref, wf_ref, o_ref, f_ref, h_s):
    @pl.when(pl.program_id(1) == 0)
    def _():
        h_s[...] = (x_ref[...] * (1.0 + sc_ref[...]) + sh_ref[...]).astype(BF16)
        f_ref[...] = jnp.dot(h_s[...], wf_ref[...], preferred_element_type=F32)

    o_ref[...] = jnp.dot(h_s[...], w_ref[...], preferred_element_type=F32)


def _inproj_modulated(x, sc, sh, w_in_bf, layer, n, w_f):
    bsz, seq, d = x.shape
    tm, tn = 1024, 1024
    per_b = seq // tm
    vec = pl.BlockSpec((None, 1, d), lambda i, j: (i // per_b, 0, 0))
    return pl.pallas_call(
        _inproj_modulated_kernel,
        grid=(bsz * per_b, n // tn),
        in_specs=[pl.BlockSpec((None, tm, d), lambda i, j: (i // per_b, i % per_b, 0)), vec, vec,
                  pl.BlockSpec((None, d, tn), lambda i, j: (layer, 0, j)),
                  pl.BlockSpec((d, LANES), lambda i, j: (0, 0))],
        out_specs=[pl.BlockSpec((tm, tn), lambda i, j: (i, j)),
                   pl.BlockSpec((tm, LANES), lambda i, j: (i, 0))],
        out_shape=[jax.ShapeDtypeStruct((bsz * seq, n), F32),
                   jax.ShapeDtypeStruct((bsz * seq, LANES), F32)],
        scratch_shapes=[pltpu.VMEM((tm, d), BF16)],
        compiler_params=_cparams("parallel", "arbitrary"),
    )(x, sc, sh, w_in_bf, w_f)


def _inproj(h, w_in_bf, layer, n, w_f):
    t, d = h.shape
    tm, tn = 1024, 1024
    return pl.pallas_call(
        _inproj_kernel,
        grid=(t // tm, n // tn),
        in_specs=[pl.BlockSpec((tm, d), lambda i, j: (i, 0)),
                  pl.BlockSpec((None, d, tn), lambda i, j: (layer, 0, j)),
                  pl.BlockSpec((d, LANES), lambda i, j: (0, 0))],
        out_specs=[pl.BlockSpec((tm, tn), lambda i, j: (i, j)),
                   pl.BlockSpec((tm, LANES), lambda i, j: (i, 0))],
        out_shape=[jax.ShapeDtypeStruct((t, n), F32),
                   jax.ShapeDtypeStruct((t, LANES), F32)],
        compiler_params=_cparams("parallel", "arbitrary"),
    )(h, w_in_bf, w_f)


TOEP_LEN = 512
TOEP_ROWS = 256


TOEP_BATCH = 8


def _toeplitz_kernel(rr_ref, o_ref):
    for t in range(TOEP_BATCH):
        x = jnp.broadcast_to(rr_ref[t], (TOEP_ROWS, TOEP_LEN))
        y = pltpu.roll(x, TOEP_LEN - TOEP_ROWS + 1, 1, stride=1, stride_axis=0)
        o_ref[t] = y[:, :o_ref.shape[2]]


def _toeplitz(rr, cols):
    g = rr.shape[0]
    assert g % TOEP_BATCH == 0
    return pl.pallas_call(
        _toeplitz_kernel,
        grid=(g // TOEP_BATCH,),
        in_specs=[pl.BlockSpec((TOEP_BATCH, 1, TOEP_LEN), lambda i: (i, 0, 0))],
        out_specs=pl.BlockSpec((TOEP_BATCH, TOEP_ROWS, cols), lambda i: (i, 0, 0)),
        out_shape=jax.ShapeDtypeStruct((g, TOEP_ROWS, cols), F32),
        compiler_params=_cparams("parallel"),
    )(rr.reshape(g, 1, TOEP_LEN))


def _bias_of_distance(rel, dist, valid):
    bias = jnp.moveaxis(rel[_t5_bucket(dist)], -1, 0)
    return jnp.where(valid, bias * LOG2E, NEG).astype(F32)


def _dilated_bias_tiles(rel_a):
    c = jnp.arange(TOEP_LEN)
    rows = []
    for (_, dil) in DIL_PATTERNS:
        d0 = c - (TOEP_ROWS - 1) + DIL_W
        d1 = c - (TOEP_ROWS - 1)
        rows.append(jnp.stack([_bias_of_distance(rel_a, d0 * dil, (d0 >= 0) & (d0 <= DIL_W)),
                               _bias_of_distance(rel_a, d1 * dil, d1 >= 0)]))
    rr = jnp.stack(rows)
    out = _toeplitz(rr.reshape(-1, TOEP_LEN), DIL_W)
    return out.reshape(len(DIL_PATTERNS), 2, H_A, TOEP_ROWS, DIL_W)


def _moba_num_tiles(seq):
    blk = MOBA_BLOCK
    d_all = np.arange(max(seq, 2))
    nf = np.maximum(d_all, 1).astype(np.float32)
    half = N_BUCKETS // 2
    big = half + (np.log(nf / half) / math.log(REL_MAX_DIST / half) * (N_BUCKETS - half)).astype(np.int32)
    bucket = np.where(d_all < half, d_all, np.minimum(big, N_BUCKETS - 1))
    last_change = int(np.max(np.nonzero(np.diff(bucket))[0])) + 1
    wanted = -(-(last_change + 2 + blk - 1) // blk)
    first_const = min(seq // blk - 1, wanted)
    return first_const + 1, first_const == wanted


def _moba_bias_tiles(rel_b, seq):
    ntile, _ = _moba_num_tiles(seq)
    c = jnp.arange(TOEP_LEN)[None, :]
    d = jnp.arange(ntile)[:, None] * MOBA_BLOCK + c - (TOEP_ROWS - 1)
    rr = _bias_of_distance(rel_b, d, d >= 0)
    out = _toeplitz(rr.reshape(-1, TOEP_LEN), MOBA_BLOCK)
    return out.reshape(H_B, ntile, TOEP_ROWS, MOBA_BLOCK)


def _head_rows():
    row = lax.broadcasted_iota(jnp.int32, (LANES, 1), 0)
    return row < HEAD_DIM, row >= HEAD_DIM


DIL_UNROLL = 16


def _dilated_kernel(q_ref, k_ref, v_ref, bm_ref, o_ref, og_s, lse_s, sblk_s, pblk_s, vts_s, *, seq):
    w = DIL_W
    own_rows = _head_rows()
    for g, (_, dil) in enumerate(DIL_PATTERNS):
        nb = (seq // dil) // w

        def rows(start, size, dil=dil):
            return pl.ds(start, size) if dil == 1 else pl.ds(start, size, stride=dil)

        def trip(it, carry, g=g, dil=dil, nb=nb, rows=rows):
            metas = []
            k = None
            for u in range(DIL_UNROLL):
                idx = it * DIL_UNROLL + u
                r = idx // nb
                n = idx % nb
                q_rows = rows(n * w * dil + r, w)
                var = jnp.where(n == 0, 1, 0)
                qt = q_ref[q_rows, :].T * (ATTN_SCALE * LOG2E)
                kv_slot = u - 1 if (nb == 2 and u % 2 == 1) else u
                if kv_slot == u:
                    kv_rows = rows(jnp.maximum(n - 1, 0) * w * dil + r, 2 * w)
                    k = k_ref[kv_rows, :].astype(BF16)
                    vt = v_ref[kv_rows, :].T
                    for h in range(2):
                        vts_s[u, h] = jnp.concatenate(
                            [vt[h * HEAD_DIM:(h + 1) * HEAD_DIM, :], _ones_rows(2 * w)],
                            axis=0).astype(BF16)
                tops = []
                for h in range(2):
                    s = jnp.dot(k, jnp.where(own_rows[h], qt, 0.0).astype(BF16),
                                preferred_element_type=F32) + bm_ref[g, var, h]
                    sblk_s[u, h] = s
                    tops.append(jnp.max(s, axis=0, keepdims=True))
                metas.append((q_rows, tops, kv_slot))
            pvs = []
            for u, (_, tops, kv_slot) in enumerate(metas):
                for h in range(2):
                    pblk_s[u, h] = jnp.exp2(sblk_s[u, h] - tops[h]).astype(BF16)
                    pvs.append(jnp.dot(vts_s[kv_slot, h], pblk_s[u, h], preferred_element_type=F32))
            for u, (q_rows, tops, _) in enumerate(metas):
                pair = pvs[2 * u:2 * u + 2]
                o_t = jnp.concatenate([pv[:HEAD_DIM] / pv[HEAD_DIM:HEAD_DIM + 1] for pv in pair], axis=0)
                lse_t = jnp.concatenate(
                    [jnp.broadcast_to(m + jnp.log(pv[HEAD_DIM:HEAD_DIM + 1]) * LOG2E, (HEAD_DIM, w))
                     for m, pv in zip(tops, pair)], axis=0)
                og_s[g, q_rows, :] = o_t.T
                lse_s[g, q_rows, :] = lse_t.T
            return carry

        lax.fori_loop(0, nb * dil // DIL_UNROLL, trip, 0)

    top = jnp.maximum(jnp.maximum(lse_s[0], lse_s[1]), lse_s[2])
    num = jnp.zeros_like(top)
    den = jnp.zeros_like(top)
    for g in range(len(DIL_PATTERNS)):
        wg = jnp.exp2(lse_s[g] - top)
        num = num + wg * og_s[g]
        den = den + wg
    o_ref[...] = num / den


def _dilated_attention(qkv, bm_a, bsz, seq):
    npair = H_A // 2
    npat = len(DIL_PATTERNS)
    kern = functools.partial(_dilated_kernel, seq=seq)

    def col(off):
        return pl.BlockSpec((None, seq, LANES), lambda p, b, off=off: (b, 0, off + p))

    return pl.pallas_call(
        kern,
        grid=(npair, bsz),
        in_specs=[col(0), col(npair), col(2 * npair),
                  pl.BlockSpec((npat, 2, 2, 2 * DIL_W, DIL_W), lambda p, b: (0, 0, p, 0, 0))],
        out_specs=pl.BlockSpec((None, seq, LANES), lambda p, b: (b, 0, p)),
        out_shape=jax.ShapeDtypeStruct((bsz, seq, W_A), F32),
        scratch_shapes=[pltpu.VMEM((npat, seq, LANES), F32)] * 2 + [
            pltpu.VMEM((DIL_UNROLL, 2, 2 * DIL_W, DIL_W), F32),
            pltpu.VMEM((DIL_UNROLL, 2, 2 * DIL_W, DIL_W), BF16),
            pltpu.VMEM((DIL_UNROLL, 2, VT_ROWS, 2 * DIL_W), BF16)],
        compiler_params=_cparams("parallel", "parallel"),
    )(qkv, qkv, qkv, bm_a)


VT_ROWS = HEAD_DIM + 16


def _ones_rows(width):
    row = lax.broadcasted_iota(jnp.int32, (VT_ROWS - HEAD_DIM, width), 0)
    return jnp.where(row == 0, 1.0, 0.0)


def _transpose_to(dst_ref, src_ref, chunk=512):
    for c in range(src_ref.shape[0] // chunk):
        cols = slice(c * chunk, (c + 1) * chunk)
        t = src_ref[cols, :].T
        for h in range(2):
            dst_ref[h, :, cols] = jnp.concatenate(
                [t[h * HEAD_DIM:(h + 1) * HEAD_DIM, :], _ones_rows(chunk)], axis=0).astype(BF16)


FLASH_CHUNK = 2


def _own_tile(qi, tk, tq):
    col = lax.broadcasted_iota(jnp.int32, (1, tq), 1)
    return qi * (tq // tk) + col // tk


def _causal_rows(qi, tk, nblk, o_ref, kaug_s, vt_s, qa_s, ssc, psc, acc_s, bias_fn):
    ck = FLASH_CHUNK * tk
    tq = ck

    def stage(c, u):
        r0 = (c % 2) * ck + u * tk
        return slice(r0, r0 + tk)

    def scores(c, h, q_tile):
        top = None
        for u in range(FLASH_CHUNK):
            kj = c * FLASH_CHUNK + u
            s = jnp.dot(kaug_s[h, kj * tk:(kj + 1) * tk, :], qa_s[h], preferred_element_type=F32)
            s = bias_fn(h, kj, s, q_tile)
            ssc[h, stage(c, u), :] = s
            cmax = jnp.max(s, axis=0, keepdims=True)
            top = cmax if top is None else jnp.maximum(top, cmax)
        return top

    def values(c, h, top, m):
        for u in range(FLASH_CHUNK):
            psc[h, stage(c, u), :] = jnp.exp2(ssc[h, stage(c, u), :] - top).astype(BF16)
        slot = slice((c % 2) * ck, (c % 2 + 1) * ck)
        pv = jnp.dot(vt_s[h, :, c * ck:(c + 1) * ck], psc[h, slot, :], preferred_element_type=F32)
        acc_s[h] = pv if m is None else jnp.exp2(m - top) * acc_s[h] + pv

    def row_pass(nchunk):
        tops = {(0, h): scores(0, h, nchunk - 1) for h in range(2)}
        for c in range(nchunk):
            if c + 1 < nchunk:
                for h in range(2):
                    tops[(c + 1, h)] = jnp.maximum(tops[(c, h)], scores(c + 1, h, nchunk - 1))
            for h in range(2):
                values(c, h, tops[(c, h)], tops[(c - 1, h)] if c else None)
        return jnp.concatenate([acc_s[h, :HEAD_DIM] / acc_s[h, HEAD_DIM:HEAD_DIM + 1]
                                for h in range(2)], axis=0).T

    for nchunk in range(1, nblk // FLASH_CHUNK + 1):
        @pl.when(qi == nchunk - 1)
        def _(nchunk=nchunk):
            o_ref[...] = row_pass(nchunk)


def _moba_kernel(q_ref, k_ref, v_ref, bt_ref, o_ref,
                 kaug_s, vt_s, kmean_s, qa_s, ssc, psc, acc_s, *, nblk, ntile, last_const):
    blk = MOBA_BLOCK
    qi = pl.program_id(2)
    spare = LANES - HEAD_DIM - nblk

    @pl.when(qi == 0)
    def _():
        lane = lax.broadcasted_iota(jnp.int32, (1, LANES), 1)
        for n in range(nblk):
            k = k_ref[n * blk:(n + 1) * blk, :]
            kaug_s[0, n * blk:(n + 1) * blk, :] = jnp.where(
                lane < HEAD_DIM, k, (lane == HEAD_DIM + n).astype(F32)).astype(BF16)
            kaug_s[1, n * blk:(n + 1) * blk, :] = jnp.where(
                lane >= HEAD_DIM, k, (lane == n).astype(F32)).astype(BF16)
            kmean_s[n:n + 1, :] = jnp.mean(k, axis=0, keepdims=True)
        _transpose_to(vt_s, v_ref)

    tq = FLASH_CHUNK * blk
    qt = q_ref[...].T
    own_rows = _head_rows()
    blk_id = lax.broadcasted_iota(jnp.int32, (nblk, 1), 0)
    own_blk = _own_tile(qi, blk, tq)
    past = blk_id < own_blk
    for h in range(2):
        qh = jnp.where(own_rows[h], qt, 0.0)
        gate = jnp.dot(kmean_s[...], qh, precision=lax.Precision.HIGHEST,
                       preferred_element_type=F32)
        gate = jnp.where(past, gate, NEG)
        beaten_by = jnp.zeros(gate.shape, F32)
        for m in range(nblk):
            gm = gate[m:m + 1, :]
            wins = jnp.logical_or(gm > gate, jnp.logical_and(gm == gate, blk_id > m))
            beaten_by = beaten_by + jnp.where(wins, 1.0, 0.0)
        chosen = beaten_by < float(MOBA_TOPK)
        keep = jnp.logical_or(jnp.logical_and(chosen, past), blk_id == own_blk)
        selb = jnp.where(keep, 0.0, NEG)
        qs = qt[h * HEAD_DIM:(h + 1) * HEAD_DIM, :] * (ATTN_SCALE * LOG2E)
        pad = jnp.zeros((spare, tq), F32)
        parts = [qs, selb, pad] if h == 0 else [selb, pad, qs]
        qa_s[h] = jnp.concatenate(parts, axis=0).astype(BF16)

    def bias_fn(h, kj, s, q_tile):
        offs = [min(max(q_tile * FLASH_CHUNK + a - kj, 0), ntile - 1) for a in range(FLASH_CHUNK)]
        if last_const and all(o == ntile - 1 for o in offs):
            return s + bt_ref[h, ntile - 1, 0:1, 0:1]
        return s + jnp.concatenate([bt_ref[h, o] for o in offs], axis=1)

    _causal_rows(qi, blk, nblk, o_ref, kaug_s, vt_s, qa_s, ssc, psc, acc_s, bias_fn)


def _moba_attention(qkv, bt_b, bsz, seq, col0):
    npair = H_B // 2
    nblk = seq // MOBA_BLOCK
    ntile = bt_b.shape[1]
    assert nblk % 8 == 0 and nblk <= LANES - HEAD_DIM
    assert nblk % FLASH_CHUNK == 0
    last_const = _moba_num_tiles(seq) == (ntile, True)
    kern = functools.partial(_moba_kernel, nblk=nblk, ntile=ntile, last_const=last_const)
    blk = MOBA_BLOCK
    tq = FLASH_CHUNK * blk

    def col(off):
        return pl.BlockSpec((None, seq, LANES), lambda p, b, i, off=off: (b, 0, off + p))

    return pl.pallas_call(
        kern,
        grid=(npair, bsz, seq // tq),
        in_specs=[pl.BlockSpec((None, tq, LANES), lambda p, b, i: (b, i, col0 + p)),
                  col(col0 + npair), col(col0 + 2 * npair),
                  pl.BlockSpec((2, ntile, blk, blk), lambda p, b, i: (p, 0, 0, 0))],
        out_specs=pl.BlockSpec((None, tq, LANES), lambda p, b, i: (b, i, p)),
        out_shape=jax.ShapeDtypeStruct((bsz, seq, W_B), F32),
        scratch_shapes=[pltpu.VMEM((2, seq, LANES), BF16), pltpu.VMEM((2, VT_ROWS, seq), BF16),
                        pltpu.VMEM((nblk, LANES), F32), pltpu.VMEM((2, LANES, tq), BF16),
                        pltpu.VMEM((2, 2 * tq, tq), F32), pltpu.VMEM((2, 2 * tq, tq), BF16),
                        pltpu.VMEM((2, VT_ROWS, tq), F32)],
        compiler_params=_cparams("parallel", "parallel", "arbitrary"),
    )(qkv, qkv, qkv, bt_b)


CUM_SPLIT = 3
CUM_CHUNK = 256
FOX_TILE = 256


def _fgate_kernel(f_ref, bf_ref, pm_ref, o_ref, cum_s):
    seq = f_ref.shape[0]
    ch = CUM_CHUNK
    i = lax.broadcasted_iota(jnp.int32, (ch, ch), 0)
    j = lax.broadcasted_iota(jnp.int32, (ch, ch), 1)
    lower = (j <= i).astype(F32)
    carry = jnp.zeros((1, LANES), F32)
    for c in range(seq // ch):
        z = f_ref[c * ch:(c + 1) * ch, :] + bf_ref[...]
        ls = jnp.minimum(z, 0.0) - jnp.log(1.0 + jnp.exp(-jnp.abs(z)))
        cum = jnp.dot(lower, ls, precision=lax.Precision.HIGHEST, preferred_element_type=F32) + carry
        cum_s[c * ch:(c + 1) * ch, :] = cum
        carry = cum[ch - 1:ch, :]
    rest = cum_s[...] * (-LOG2E)
    out = None
    for t in range(CUM_SPLIT):
        term = rest.astype(BF16)
        rest = rest - term.astype(F32)
        part = jnp.dot(term, pm_ref[t], preferred_element_type=F32)
        out = part if out is None else out + part
    o_ref[...] = out.astype(BF16)


def _forget_keys(f, b_f):
    bsz, seq, _ = f.shape
    npair = H_C // 2
    nblk = seq // FOX_TILE
    pm = np.zeros((CUM_SPLIT, LANES, npair * LANES), np.float32)
    for t in range(CUM_SPLIT):
        for p in range(npair):
            pm[t, 2 * p, p * LANES + HEAD_DIM + nblk + t] = 1.0
            pm[t, 2 * p + 1, p * LANES + nblk + t] = 1.0
    bf = jnp.pad(b_f, (0, LANES - b_f.shape[0])).reshape(1, LANES)
    return pl.pallas_call(
        _fgate_kernel,
        grid=(bsz,),
        in_specs=[pl.BlockSpec((None, seq, LANES), lambda b: (b, 0, 0)),
                  pl.BlockSpec((1, LANES), lambda b: (0, 0)),
                  pl.BlockSpec((CUM_SPLIT, LANES, npair * LANES), lambda b: (0, 0, 0))],
        out_specs=pl.BlockSpec((None, seq, npair * LANES), lambda b: (b, 0, 0)),
        out_shape=jax.ShapeDtypeStruct((bsz, seq, npair * LANES), BF16),
        scratch_shapes=[pltpu.VMEM((seq, LANES), F32)],
        compiler_params=_cparams("parallel"),
    )(f, bf, jnp.asarray(pm, BF16))


def _fox_kernel(q_ref, k_ref, v_ref, ca_ref, o_ref,
                kaug_s, vt_s, qa_s, cm_s, ssc, psc, acc_s, *, tk, nblk):
    qi = pl.program_id(2)
    spare = LANES - HEAD_DIM - nblk - 8

    @pl.when(qi == 0)
    def _():
        lane = lax.broadcasted_iota(jnp.int32, (1, LANES), 1)
        for n in range(nblk):
            k = k_ref[n * tk:(n + 1) * tk, :]
            ca = ca_ref[n * tk:(n + 1) * tk, :].astype(F32)
            kaug_s[0, n * tk:(n + 1) * tk, :] = jnp.where(
                lane < HEAD_DIM, k, ca + (lane == HEAD_DIM + n).astype(F32)).astype(BF16)
            kaug_s[1, n * tk:(n + 1) * tk, :] = jnp.where(
                lane >= HEAD_DIM, k, ca + (lane == n).astype(F32)).astype(BF16)
        _transpose_to(vt_s, v_ref)
        key = lax.broadcasted_iota(jnp.int32, (tk, tk), 0)
        qry = lax.broadcasted_iota(jnp.int32, (tk, tk), 1)
        cm_s[0] = jnp.zeros((tk, tk), F32)
        cm_s[1] = jnp.where(key <= qry, 0.0, NEG)

    tq = FLASH_CHUNK * tk
    qt = q_ref[...].T * (ATTN_SCALE * LOG2E)
    blk_id = lax.broadcasted_iota(jnp.int32, (nblk, 1), 0)
    future = jnp.where(blk_id <= _own_tile(qi, tk, tq), 0.0, NEG)
    sub = lax.broadcasted_iota(jnp.int32, (8, 1), 0)
    ones = jnp.broadcast_to(jnp.where(sub < CUM_SPLIT, 1.0, 0.0), (8, tq))
    pad = jnp.zeros((spare, tq), F32)
    for h in range(2):
        qs = qt[h * HEAD_DIM:(h + 1) * HEAD_DIM, :]
        parts = [qs, future, ones, pad] if h == 0 else [future, ones, pad, qs]
        qa_s[h] = jnp.concatenate(parts, axis=0).astype(BF16)

    def bias_fn(h, kj, s, q_tile):
        if kj < q_tile * FLASH_CHUNK:
            return s
        tiles = [cm_s[1 if kj == q_tile * FLASH_CHUNK + a else 0] for a in range(FLASH_CHUNK)]
        return s + jnp.concatenate(tiles, axis=1)

    _causal_rows(qi, tk, nblk, o_ref, kaug_s, vt_s, qa_s, ssc, psc, acc_s, bias_fn)


def _fox_attention(qkv, caug, bsz, seq, col0):
    npair = H_C // 2
    tk = FOX_TILE
    tq = FLASH_CHUNK * tk
    nblk = seq // tk
    assert nblk % 8 == 0 and nblk % FLASH_CHUNK == 0 and nblk + 8 <= LANES - HEAD_DIM
    kern = functools.partial(_fox_kernel, tk=tk, nblk=nblk)

    def col(off):
        return pl.BlockSpec((None, seq, LANES), lambda p, b, i, off=off: (b, 0, off + p))

    return pl.pallas_call(
        kern,
        grid=(npair, bsz, seq // tq),
        in_specs=[pl.BlockSpec((None, tq, LANES), lambda p, b, i: (b, i, col0 + p)),
                  col(col0 + npair), col(col0 + 2 * npair),
                  pl.BlockSpec((None, seq, LANES), lambda p, b, i: (b, 0, p))],
        out_specs=pl.BlockSpec((None, tq, LANES), lambda p, b, i: (b, i, p)),
        out_shape=jax.ShapeDtypeStruct((bsz, seq, W_C), F32),
        scratch_shapes=[pltpu.VMEM((2, seq, LANES), BF16), pltpu.VMEM((2, VT_ROWS, seq), BF16),
                        pltpu.VMEM((2, LANES, tq), BF16), pltpu.VMEM((2, tk, tk), F32),
                        pltpu.VMEM((2, 2 * tq, tq), F32), pltpu.VMEM((2, 2 * tq, tq), BF16),
                        pltpu.VMEM((2, VT_ROWS, tq), F32)],
        compiler_params=_cparams("parallel", "parallel", "arbitrary"),
    )(qkv, qkv, qkv, caug)


def _layer_norm_rows(z, g, b):
    mu = jnp.mean(z, axis=-1, keepdims=True)
    zc = z - mu
    var = jnp.mean(zc * zc, axis=-1, keepdims=True)
    return zc * lax.rsqrt(var + LN_EPS) * g + b


def _mixout_kernel(oa_ref, ob_ref, oc_ref, ga_ref, gb_ref, gc_ref, wa_ref, wb_ref, wc_ref,
                   x_ref, g1_ref, lng_ref, lnb_ref, sc_ref, sh_ref, x_out, h_out, *, alpha):
    def normed(o_ref, g_ref, rows):
        o = o_ref[rows, :]
        ms = jnp.mean(o * o, axis=-1, keepdims=True)
        return (o * lax.rsqrt(ms + LN_EPS) * g_ref[...]).astype(BF16)

    blocks = [slice(r0, r0 + MIX_ROWS) for r0 in range(0, x_ref.shape[0], MIX_ROWS)]
    ys = []
    for rows in blocks:
        y = jnp.dot(normed(oa_ref, ga_ref, rows), wa_ref[...], preferred_element_type=F32)
        y = y + jnp.dot(normed(ob_ref, gb_ref, rows), wb_ref[...], preferred_element_type=F32)
        ys.append(y + jnp.dot(normed(oc_ref, gc_ref, rows), wc_ref[...], preferred_element_type=F32))
    for rows, y in zip(blocks, ys):
        x1 = _layer_norm_rows(alpha * x_ref[rows, :] + (1.0 + g1_ref[...]) * y,
                              lng_ref[...], lnb_ref[...])
        x_out[rows, :] = x1
        h_out[rows, :] = (x1 * (1.0 + sc_ref[...]) + sh_ref[...]).astype(BF16)


MIX_ROWS = 256


def _mix_out(o_a, o_b, o_c, g_a, g_b, g_c, w_out_bf, x, g1, ln_g, ln_b, sc2, sh2, alpha):
    bsz, seq, d = x.shape
    tm = 512
    w_a, w_b, w_c = w_out_bf[:W_A], w_out_bf[W_A:W_A + W_B], w_out_bf[W_A + W_B:]

    def act(width):
        return pl.BlockSpec((None, tm, width), lambda b, i: (b, i, 0))

    def const(shape):
        return pl.BlockSpec(shape, lambda b, i: (0,) * len(shape), pipeline_mode=pl.Buffered(1))

    bvec = pl.BlockSpec((None, 1, d), lambda b, i: (b, 0, 0))
    return pl.pallas_call(
        functools.partial(_mixout_kernel, alpha=alpha),
        grid=(bsz, seq // tm),
        in_specs=[act(W_A), act(W_B), act(W_C),
                  const((1, W_A)), const((1, W_B)), const((1, W_C)),
                  const((W_A, d)), const((W_B, d)), const((W_C, d)),
                  act(d), bvec, const((1, d)), const((1, d)), bvec, bvec],
        out_specs=[act(d), act(d)],
        out_shape=[jax.ShapeDtypeStruct((bsz, seq, d), F32),
                   jax.ShapeDtypeStruct((bsz, seq, d), BF16)],
        compiler_params=_cparams("parallel", "parallel"),
    )(o_a, o_b, o_c, g_a.reshape(1, -1), g_b.reshape(1, -1), g_c.reshape(1, -1),
      w_a, w_b, w_c, x, g1, ln_g.reshape(1, d), ln_b.reshape(1, d), sc2, sh2)


FFN_HALO = 8
FFN_SUB = 256
FFN_ROWS = 256


def _ffn_up_kernel(h_ref, wa_ref, wb_ref, cwa_ref, cwb_ref, cba_ref, cbb_ref, o_ref,
                   bufa, bufb, wa_bf, wb_bf):
    i = pl.program_id(2)
    tm, tn = o_ref.shape

    @pl.when(jnp.logical_and(pl.program_id(1) == 0, i == 0))
    def _():
        wa_bf[...] = wa_ref[...].astype(BF16)
        wb_bf[...] = wb_ref[...].astype(BF16)

    @pl.when(i == 0)
    def _():
        bufa[0:FFN_HALO, :] = jnp.zeros((FFN_HALO, tn), F32)
        bufb[0:FFN_HALO, :] = jnp.zeros((FFN_HALO, tn), F32)

    subs = [(r * FFN_ROWS, slice(c * FFN_SUB, (c + 1) * FFN_SUB))
            for r in range(tm // FFN_ROWS) for c in range(tn // FFN_SUB)]
    prods = []
    for r0, cs in subs:
        h = h_ref[r0:r0 + FFN_ROWS, :]
        prods.append((jnp.dot(h, wa_bf[:, cs], preferred_element_type=F32),
                      jnp.dot(h, wb_bf[:, cs], preferred_element_type=F32)))

    def conv(u, cw_ref, cb_ref, buf, r0, cs):
        base = FFN_HALO + r0
        buf[base:base + FFN_ROWS, cs] = u
        return (cw_ref[2:3, cs] * u
                + cw_ref[1:2, cs] * buf[base - 1:base - 1 + FFN_ROWS, cs]
                + cw_ref[0:1, cs] * buf[base - 2:base - 2 + FFN_ROWS, cs]
                + cb_ref[:, cs])

    for (r0, cs), (ua, ub) in zip(subs, prods):
        a = conv(ua, cwa_ref, cba_ref, bufa, r0, cs)
        b = conv(ub, cwb_ref, cbb_ref, bufb, r0, cs)
        o_ref[r0:r0 + FFN_ROWS, cs] = (a * (1.0 / (1.0 + jnp.exp(-a))) * b).astype(o_ref.dtype)
    for buf in (bufa, bufb):
        buf[0:FFN_HALO, :] = buf[tm:tm + FFN_HALO, :]


def _ffn_up(h, w_up, layer, conv_w, conv_b):
    bsz, seq, d = h.shape
    dff = w_up.shape[2] // 2
    tm, tn = 1024, 512
    nj = dff // tn
    cb = conv_b.reshape(1, -1)
    return pl.pallas_call(
        _ffn_up_kernel,
        grid=(nj, bsz, seq // tm),
        in_specs=[pl.BlockSpec((None, tm, d), lambda j, b, i: (b, i, 0)),
                  pl.BlockSpec((None, d, tn), lambda j, b, i: (layer, 0, j)),
                  pl.BlockSpec((None, d, tn), lambda j, b, i: (layer, 0, nj + j)),
                  pl.BlockSpec((CONV_WIDTH, tn), lambda j, b, i: (0, j)),
                  pl.BlockSpec((CONV_WIDTH, tn), lambda j, b, i: (0, nj + j)),
                  pl.BlockSpec((1, tn), lambda j, b, i: (0, j)),
                  pl.BlockSpec((1, tn), lambda j, b, i: (0, nj + j))],
        out_specs=pl.BlockSpec((None, tm, tn), lambda j, b, i: (b, i, j)),
        out_shape=jax.ShapeDtypeStruct((bsz, seq, dff), BF16),
        scratch_shapes=[pltpu.VMEM((tm + FFN_HALO, tn), F32)] * 2 + [pltpu.VMEM((d, tn), BF16)] * 2,
        compiler_params=_cparams("parallel", "arbitrary", "arbitrary"),
    )(h, w_up, w_up, conv_w, conv_w, cb, cb)


def _ffn_down_kernel(g_ref, w_ref, x_ref, g2_ref, lng_ref, lnb_ref, sc_ref, sh_ref,
                     x_out, *maybe_h_out, alpha):
    y = jnp.dot(g_ref[...], w_ref[...], preferred_element_type=F32)
    x2 = _layer_norm_rows(alpha * x_ref[...] + (1.0 + g2_ref[...]) * y, lng_ref[...], lnb_ref[...])
    x_out[...] = x2
    for h_out in maybe_h_out:
        h_out[...] = (x2 * (1.0 + sc_ref[...]) + sh_ref[...]).astype(BF16)


def _ffn_down(g, w_down_bf, layer, x, g2, ln_g, ln_b, sc_next, sh_next, alpha, with_h):
    bsz, seq, d = x.shape
    dff = g.shape[2]
    tm = 256
    act = pl.BlockSpec((None, tm, d), lambda b, i: (b, i, 0))
    bvec = pl.BlockSpec((None, 1, d), lambda b, i: (b, 0, 0))
    const = pl.BlockSpec((1, d), lambda b, i: (0, 0))
    out_specs = [act, act] if with_h else [act]
    out_shape = [jax.ShapeDtypeStruct((bsz, seq, d), F32)]
    if with_h:
        out_shape.append(jax.ShapeDtypeStruct((bsz, seq, d), BF16))
    return pl.pallas_call(
        functools.partial(_ffn_down_kernel, alpha=alpha),
        grid=(bsz, seq // tm),
        in_specs=[pl.BlockSpec((None, tm, dff), lambda b, i: (b, i, 0)),
                  pl.BlockSpec((None, dff, d), lambda b, i: (layer, 0, 0),
                               pipeline_mode=pl.Buffered(1)),
                  act, bvec, const, const, bvec, bvec],
        out_specs=out_specs,
        out_shape=out_shape,
        compiler_params=_cparams("parallel", "parallel"),
    )(g, w_down_bf, x, g2, ln_g.reshape(1, d), ln_b.reshape(1, d), sc_next, sh_next)


def kernel(x, c, rel_bias, w_ada, b_ada, w_in, b_f, g_mix_a, g_mix_b, g_mix_c, w_out,
           ln1_g, ln1_b, w_up, conv_w, conv_b, w_down, ln2_g, ln2_b):
    bsz, seq, d = x.shape
    depth = w_ada.shape[0]
    alpha = (2 * depth) ** 0.25
    n_qkv = 3 * (W_A + W_B + W_C)

    mod = _ada_mod(c, w_ada, b_ada)

    def mod_vec(l, idx):
        return mod[l, :, idx * d:(idx + 1) * d].reshape(bsz, 1, d)

    bm_a = _dilated_bias_tiles(rel_bias[:, :H_A])
    bt_b = _moba_bias_tiles(rel_bias[:, H_A:], seq)

    w_in_bf, w_down_bf = w_in.astype(BF16), w_down.astype(BF16)

    for l in range(depth):
        w_f = jnp.pad(w_in_bf[l, :, n_qkv:], ((0, 0), (0, LANES - H_C)))
        if l == 0:
            qkv, f = _inproj_modulated(x, mod_vec(0, 1), mod_vec(0, 0), w_in_bf, l, n_qkv, w_f)
        else:
            qkv, f = _inproj(h.reshape(bsz * seq, d), w_in_bf, l, n_qkv, w_f)
        qkv = qkv.reshape(bsz, seq, n_qkv)
        caug = _forget_keys(f.reshape(bsz, seq, LANES), b_f[l])

        o_a = _dilated_attention(qkv, bm_a, bsz, seq)
        o_b = _moba_attention(qkv, bt_b, bsz, seq, 3 * W_A // LANES)
        o_c = _fox_attention(qkv, caug, bsz, seq, 3 * (W_A + W_B) // LANES)

        x, h = _mix_out(o_a, o_b, o_c, g_mix_a[l], g_mix_b[l], g_mix_c[l], w_out[l].astype(BF16),
                        x, mod_vec(l, 2), ln1_g[l], ln1_b[l], mod_vec(l, 4), mod_vec(l, 3), alpha)
        g = _ffn_up(h, w_up, l, conv_w[l], conv_b[l])
        last = l == depth - 1
        nxt = min(l + 1, depth - 1)
        outs = _ffn_down(g, w_down_bf, l, x, mod_vec(l, 5), ln2_g[l], ln2_b[l],
                         mod_vec(nxt, 1), mod_vec(nxt, 0), alpha, with_h=not last)
        x = outs[0]
        if not last:
            h = outs[1]
    return x
```
